```python
import math
import jax
import jax.numpy as jnp
from jax import lax
import numpy as np

D_MODEL = 2048
BATCH = 4
SEQ = 2048
DEPTH = 2
DEC_BATCH = 8
DEC_SEQ = 4
PAST_LEN = 16384
PAGE_SIZE = 128

HEAD_DIM = 64
GROUP_WIDTH = D_MODEL // 4
D_MIX = 4 * GROUP_WIDTH
D_FF = ((8 * D_MODEL // 3 + 127) // 128) * 128
EPS = 1e-6
TINY = 1e-30

SSD_HEADS = GROUP_WIDTH // HEAD_DIM
SSD_GROUPS = 2
SSD_STATE = 64
CONV_W = 4
SSD_CONV_DIM = GROUP_WIDTH + 2 * SSD_GROUPS * SSD_STATE
SSD_CHUNK = 128

FOX_HEADS = GROUP_WIDTH // HEAD_DIM
Q_BLOCK = 128

NSA_HEADS = GROUP_WIDTH // HEAD_DIM
NSA_KV_HEADS = 2
NSA_GROUP = NSA_HEADS // NSA_KV_HEADS
NSA_KV_DIM = NSA_KV_HEADS * HEAD_DIM
CMP_BLOCK = 32
SLC_BLOCK = 64
TOP_N = 16
WINDOW = 512
SLC_Q_BLOCK = 64
FORCE_SCORE = 1e4

S5_CH = 16
S5_GROUPS = GROUP_WIDTH // S5_CH
S5_STATE = 64

SSD_IN = GROUP_WIDTH + SSD_CONV_DIM + SSD_HEADS
FOX_IN = 3 * GROUP_WIDTH + FOX_HEADS
NSA_IN = GROUP_WIDTH + 6 * NSA_KV_DIM + 3 * NSA_HEADS
S5_IN = GROUP_WIDTH
D_IN = SSD_IN + FOX_IN + NSA_IN + S5_IN
IN_SPLITS = (SSD_IN, SSD_IN + FOX_IN, SSD_IN + FOX_IN + NSA_IN)

kernel_name = 'hybrid_ssd_fox_nsa_s5_decode_step'


def rms_norm(x, g):
    xf = x.astype(jnp.float32)
    y = xf * lax.rsqrt(jnp.mean(xf * xf, axis=-1, keepdims=True) + EPS)
    return (y * g.astype(jnp.float32)).astype(x.dtype)


def swiglu(h, w1, w3, w2):
    return (jax.nn.silu(h @ w1) * (h @ w3)) @ w2


def masked_softmax(s, mask):
    s = jnp.where(mask, s, -jnp.inf)
    m = jnp.max(s, axis=-1, keepdims=True)
    m = jnp.where(jnp.isfinite(m), m, 0.0)
    e = jnp.where(mask, jnp.exp(s - m), 0.0)
    return e / jnp.maximum(jnp.sum(e, axis=-1, keepdims=True), TINY)


def row_block(n_rows, block):
    return block if n_rows % block == 0 else n_rows


def block_map(fn, n_rows, block):
    out = lax.map(fn, jnp.arange(n_rows // block))
    out = jnp.moveaxis(out, 0, 1)
    return out.reshape(out.shape[0], n_rows, *out.shape[3:])


def gather_pages(cache, page_table, layer):
    g = cache[page_table, layer]
    return g.reshape(g.shape[0], g.shape[1] * g.shape[2], *g.shape[3:])


def complex_affine_combine(e1, e2):
    a1r, a1i, b1r, b1i = e1
    a2r, a2i, b2r, b2i = e2
    return (a2r * a1r - a2i * a1i, a2r * a1i + a2i * a1r,
            a2r * b1r - a2i * b1i + b2r, a2r * b1i + a2i * b1r + b2i)


def ssd_mixer(p, lp, conv_buf, ssm0):
    dtype = p.dtype
    bsz, T, _ = p.shape
    f32 = jnp.float32
    p = p.astype(f32)
    z, xbc, dt = jnp.split(p, (GROUP_WIDTH, GROUP_WIDTH + SSD_CONV_DIM), axis=-1)
    xpad = jnp.concatenate([conv_buf.astype(f32), xbc], axis=1)
    w = lp['ssd_conv_w'].astype(f32)
    conv = lp['ssd_conv_b'].astype(f32)
    for k in range(CONV_W):
        conv = conv + xpad[:, k:k + T] * w[k]
    new_conv = xpad[:, T:]
    xbc = jax.nn.silu(conv)
    xs, bm, cm = jnp.split(xbc, (GROUP_WIDTH, GROUP_WIDTH + SSD_GROUPS * SSD_STATE), axis=-1)
    rep = SSD_HEADS // SSD_GROUPS
    xs = xs.reshape(bsz, T, SSD_HEADS, HEAD_DIM)
    bm = jnp.repeat(bm.reshape(bsz, T, SSD_GROUPS, SSD_STATE), rep, axis=2)
    cm = jnp.repeat(cm.reshape(bsz, T, SSD_GROUPS, SSD_STATE), rep, axis=2)
    dt = jax.nn.softplus(dt + lp['ssd_dt_bias'].astype(f32))
    a = -jnp.exp(lp['ssd_a_log'].astype(f32))
    q = row_block(T, SSD_CHUNK)
    nc = T // q
    xs_c = xs.reshape(bsz, nc, q, SSD_HEADS, HEAD_DIM)
    b_c = bm.reshape(bsz, nc, q, SSD_HEADS, SSD_STATE)
    c_c = cm.reshape(bsz, nc, q, SSD_HEADS, SSD_STATE)
    dt_c = dt.reshape(bsz, nc, q, SSD_HEADS)
    acs = jnp.cumsum(dt_c * a, axis=2)
    acs_h = jnp.swapaxes(acs, 2, 3)
    causal = jnp.tril(jnp.ones((q, q), dtype=bool))
    decay = jnp.exp(jnp.where(causal, acs_h[..., :, None] - acs_h[..., None, :], -jnp.inf))
    cb = jnp.einsum('bcthn,bcshn->bchts', c_c, b_c)
    w_ts = cb * decay * jnp.swapaxes(dt_c, 2, 3)[:, :, :, None, :]
    y_diag = jnp.einsum('bchts,bcshp->bcthp', w_ts, xs_c)
    decay_end = jnp.exp(acs[:, :, -1:] - acs)
    chunk_states = jnp.einsum('bcshn,bcsh,bcshp->bchpn', b_c, decay_end * dt_c, xs_c)
    chunk_decay = jnp.exp(acs[:, :, -1])

    def carry_step(s, inp):
        st, dec = inp
        return dec[:, :, None, None] * s + st, s

    final, s_in = lax.scan(carry_step, ssm0.astype(f32),
                           (jnp.moveaxis(chunk_states, 1, 0), jnp.moveaxis(chunk_decay, 1, 0)))
    s_in = jnp.moveaxis(s_in, 0, 1)
    y_off = jnp.einsum('bcthn,bchpn,bcth->bcthp', c_c, s_in, jnp.exp(acs))
    y = y_diag + y_off + lp['ssd_d'].astype(f32)[:, None] * xs_c
    y = y.reshape(bsz, T, GROUP_WIDTH) * jax.nn.silu(z)
    return rms_norm(y, lp['ssd_norm']).astype(dtype), new_conv, final


def fox_mixer(p, lp, past):
    dtype = p.dtype
    bsz, T, _ = p.shape
    q, k, v, fl = jnp.split(p, (GROUP_WIDTH, 2 * GROUP_WIDTH, 3 * GROUP_WIDTH), axis=-1)
    shp = (bsz, T, FOX_HEADS, HEAD_DIM)
    q, k, v = q.reshape(shp), k.reshape(shp), v.reshape(shp)
    logf = jax.nn.log_sigmoid((fl + lp['fox_f_bias']).astype(jnp.float32))
    if past is None:
        k_all, v_all, lf_all, q_off = k, v, logf, 0
    else:
        past_kv, past_lf = past
        q_off = past_kv.shape[1]
        k_all = jnp.concatenate([past_kv[:, :, 0].astype(dtype), k], axis=1)
        v_all = jnp.concatenate([past_kv[:, :, 1].astype(dtype), v], axis=1)
        lf_all = jnp.concatenate([past_lf.astype(jnp.float32), logf], axis=1)
    L = k_all.shape[1]
    suffix = lax.cumsum(lf_all, axis=1, reverse=True)
    r = jnp.concatenate([suffix[:, 1:], jnp.zeros_like(suffix[:, :1])], axis=1)
    r_key = jnp.swapaxes(r, 1, 2)[:, :, None, :]
    key_pos = jnp.arange(L)
    qb = row_block(T, Q_BLOCK)
    scale = HEAD_DIM ** -0.5

    def attend(i):
        s0 = i * qb
        q_blk = lax.dynamic_slice_in_dim(q, s0, qb, axis=1)
        r_q = lax.dynamic_slice_in_dim(r, q_off + s0, qb, axis=1)
        q_pos = q_off + s0 + jnp.arange(qb)
        logits = jnp.einsum('bthd,bshd->bhts', q_blk, k_all).astype(jnp.float32) * scale
        logits = logits + r_key - jnp.swapaxes(r_q, 1, 2)[..., None]
        logits = jnp.where(key_pos[None, :] <= q_pos[:, None], logits, -jnp.inf)
        probs = jax.nn.softmax(logits, axis=-1)
        return jnp.einsum('bhts,bshd->bthd', probs.astype(dtype), v_all)

    o = block_map(attend, T, qb).reshape(bsz, T, GROUP_WIDTH)
    return rms_norm(o, lp['fox_out_norm']), jnp.stack([k, v], axis=2), logf


def nsa_mixer(p, lp, past):
    dtype = p.dtype
    f32 = jnp.float32
    bsz, T, _ = p.shape
    q, kv, gl = jnp.split(p, (GROUP_WIDTH, GROUP_WIDTH + 6 * NSA_KV_DIM), axis=-1)
    q = q.reshape(bsz, T, NSA_KV_HEADS, NSA_GROUP, HEAD_DIM)
    kv = kv.reshape(bsz, T, 6, NSA_KV_HEADS, HEAD_DIM)
    new_rows = kv[:, :, :4]
    win_rows = kv[:, :, 4:]
    if past is None:
        full, win, q_off = new_rows, win_rows, 0
        keep = min(WINDOW, T)
    else:
        past_kv, past_win = past
        q_off = past_kv.shape[1]
        full = jnp.concatenate([past_kv.astype(dtype), new_rows], axis=1)
        win = jnp.concatenate([past_win.astype(dtype), win_rows], axis=1)
        keep = past_win.shape[1]
    w_pos0 = q_off + T - win.shape[1]
    new_win = win[:, win.shape[1] - keep:]
    L = full.shape[1]
    scale = HEAD_DIM ** -0.5
    q_pos = q_off + jnp.arange(T)
    kc, vc, ks, vs = full[:, :, 0], full[:, :, 1], full[:, :, 2], full[:, :, 3]

    nbc = L // CMP_BLOCK

    def compress(rows, pe, w):
        blk = rows[:, :nbc * CMP_BLOCK].reshape(bsz, nbc, CMP_BLOCK, NSA_KV_HEADS, HEAD_DIM) + pe[:, None, :]
        blk = jnp.swapaxes(blk, 2, 3).reshape(bsz, nbc, NSA_KV_HEADS, CMP_BLOCK * HEAD_DIM)
        return blk @ w

    pe, cw = lp['nsa_cmp_pe'], lp['nsa_cmp_w']
    k_cmp = compress(kc, pe[0], cw[0])
    v_cmp = compress(vc, pe[1], cw[1])
    s_cmp = jnp.einsum('btghd,bngd->bghtn', q, k_cmp).astype(f32) * scale
    cmp_end = (jnp.arange(nbc) + 1) * CMP_BLOCK - 1
    p_cmp = masked_softmax(s_cmp, cmp_end[None, :] <= q_pos[:, None])
    o_cmp = jnp.einsum('bghtn,bngd->btghd', p_cmp.astype(dtype), v_cmp)

    ratio = SLC_BLOCK // CMP_BLOCK
    nbs = -(-L // SLC_BLOCK)
    imp = jnp.sum(p_cmp, axis=2)
    imp = jnp.pad(imp, ((0, 0), (0, 0), (0, 0), (0, nbs * ratio - nbc)))
    imp = imp.reshape(bsz, NSA_KV_HEADS, T, nbs, ratio).sum(-1)
    blk = jnp.arange(nbs)[None, :]
    cur = (q_pos // SLC_BLOCK)[:, None]
    forced = (blk == 0) | (blk == cur) | (blk == cur - 1)
    avail = blk * SLC_BLOCK <= q_pos[:, None]
    score = jnp.where(avail, jnp.where(forced, FORCE_SCORE, imp), -1.0)
    top_val, top_idx = lax.top_k(score, min(TOP_N, nbs))
    sel_ok = top_val >= 0.0
    n_sel = top_idx.shape[-1]
    pad = nbs * SLC_BLOCK - L

    def to_blocks(rows):
        rows = jnp.pad(rows, ((0, 0), (0, pad), (0, 0), (0, 0)))
        return jnp.transpose(rows.reshape(bsz, nbs, SLC_BLOCK, NSA_KV_HEADS, HEAD_DIM), (0, 3, 1, 2, 4))

    ks_b, vs_b = to_blocks(ks), to_blocks(vs)
    bi = jnp.arange(bsz)[:, None, None, None]
    gi = jnp.arange(NSA_KV_HEADS)[None, :, None, None]
    qbs = row_block(T, SLC_Q_BLOCK)

    def select_attend(i):
        s0 = i * qbs
        q_blk = lax.dynamic_slice_in_dim(q, s0, qbs, axis=1)
        idx = lax.dynamic_slice_in_dim(top_idx, s0, qbs, axis=2)
        ok = lax.dynamic_slice_in_dim(sel_ok, s0, qbs, axis=2)
        kg = ks_b[bi, gi, idx]
        vg = vs_b[bi, gi, idx]
        t_pos = q_off + s0 + jnp.arange(qbs)
        k_pos = idx[..., None] * SLC_BLOCK + jnp.arange(SLC_BLOCK)
        mask = ok[..., None] & (k_pos <= t_pos[None, None, :, None, None])
        logits = jnp.einsum('btghd,bgtjrd->bghtjr', q_blk, kg).astype(f32) * scale
        logits = jnp.where(mask[:, :, None], logits, -jnp.inf)
        logits = logits.reshape(bsz, NSA_KV_HEADS, NSA_GROUP, qbs, n_sel * SLC_BLOCK)
        probs = jax.nn.softmax(logits, axis=-1).reshape(bsz, NSA_KV_HEADS, NSA_GROUP, qbs, n_sel, SLC_BLOCK)
        return jnp.einsum('bghtjr,bgtjrd->btghd', probs.astype(dtype), vg)

    o_slc = block_map(select_attend, T, qbs)

    kw = jnp.pad(win[:, :, 0], ((0, 0), (WINDOW, 0), (0, 0), (0, 0)))
    vw = jnp.pad(win[:, :, 1], ((0, 0), (WINDOW, 0), (0, 0), (0, 0)))
    qbw = row_block(T, Q_BLOCK)
    span = WINDOW + qbw

    def window_attend(i):
        s0 = i * qbw
        start = q_off + s0 - w_pos0
        kb = lax.dynamic_slice_in_dim(kw, start, span, axis=1)
        vb = lax.dynamic_slice_in_dim(vw, start, span, axis=1)
        q_blk = lax.dynamic_slice_in_dim(q, s0, qbw, axis=1)
        t_pos = q_off + s0 + jnp.arange(qbw)
        k_pos = q_off + s0 - WINDOW + jnp.arange(span)
        diff = t_pos[:, None] - k_pos[None, :]
        mask = (diff >= 0) & (diff < WINDOW) & (k_pos[None, :] >= w_pos0)
        logits = jnp.einsum('btghd,bsgd->bghts', q_blk, kb).astype(f32) * scale
        probs = jax.nn.softmax(jnp.where(mask, logits, -jnp.inf), axis=-1)
        return jnp.einsum('bghts,bsgd->btghd', probs.astype(dtype), vb)

    o_win = block_map(window_attend, T, qbw)
    gate = jax.nn.sigmoid(gl.astype(f32)).reshape(bsz, T, NSA_KV_HEADS, NSA_GROUP, 3)
    o = gate[..., 0:1] * o_cmp + gate[..., 1:2] * o_slc + gate[..., 2:3] * o_win
    y = rms_norm(o.reshape(bsz, T, GROUP_WIDTH), lp['nsa_out_norm'])
    return y.astype(dtype), new_rows, new_win


def s5_mixer(u, lp, x0_re, x0_im):
    dtype = u.dtype
    f32 = jnp.float32
    bsz, T, _ = u.shape
    uf = u.astype(f32)
    ug = uf.reshape(bsz, T, S5_GROUPS, S5_CH)
    lam_re = lp['s5_lambda_re'].astype(f32)
    lam_im = lp['s5_lambda_im'].astype(f32)
    dt = jnp.exp(lp['s5_log_dt'].astype(f32))[:, None]
    mag = jnp.exp(lam_re * dt)
    ab_re = mag * jnp.cos(lam_im * dt)
    ab_im = mag * jnp.sin(lam_im * dt)
    den = lam_re * lam_re + lam_im * lam_im
    zr = ((ab_re - 1.0) * lam_re + ab_im * lam_im) / den
    zi = (ab_im * lam_re - (ab_re - 1.0) * lam_im) / den
    b_re = lp['s5_b_re'].astype(f32)
    b_im = lp['s5_b_im'].astype(f32)
    bb_re = zr[..., None] * b_re - zi[..., None] * b_im
    bb_im = zr[..., None] * b_im + zi[..., None] * b_re
    bu_re = jnp.einsum('btgc,gnc->btgn', ug, bb_re)
    bu_im = jnp.einsum('btgc,gnc->btgn', ug, bb_im)
    x0r = x0_re.astype(f32)
    x0i = x0_im.astype(f32)
    bu_re = bu_re.at[:, 0].add(ab_re * x0r - ab_im * x0i)
    bu_im = bu_im.at[:, 0].add(ab_re * x0i + ab_im * x0r)
    a_re = jnp.broadcast_to(ab_re, bu_re.shape)
    a_im = jnp.broadcast_to(ab_im, bu_im.shape)
    _, _, xr, xi = lax.associative_scan(complex_affine_combine, (a_re, a_im, bu_re, bu_im), axis=1)
    y = (jnp.einsum('gcn,btgn->btgc', lp['s5_c_re'].astype(f32), xr)
         - jnp.einsum('gcn,btgn->btgc', lp['s5_c_im'].astype(f32), xi))
    y = y.reshape(bsz, T, GROUP_WIDTH) + lp['s5_d'].astype(f32) * uf
    g = jax.nn.gelu(y) @ lp['s5_w_glu'].astype(f32)
    ga, gb = jnp.split(g, 2, axis=-1)
    y = rms_norm(ga * jax.nn.sigmoid(gb), lp['s5_out_norm'])
    return y.astype(dtype), xr[:, -1], xi[:, -1]


def decoder_layer(x, lp, past):
    bsz = x.shape[0]
    x = x + 0.5 * swiglu(rms_norm(x, lp['ffn1_norm']), lp['ffn1_w1'], lp['ffn1_w3'], lp['ffn1_w2'])
    h = rms_norm(x, lp['mix_norm'])
    proj = h @ lp['w_in']
    p_ssd, p_fox, p_nsa, p_s5 = jnp.split(proj, IN_SPLITS, axis=-1)
    if past is None:
        conv0 = jnp.zeros((bsz, CONV_W - 1, SSD_CONV_DIM), jnp.float32)
        ssm0 = jnp.zeros((bsz, SSD_HEADS, HEAD_DIM, SSD_STATE), jnp.float32)
        s5r0 = jnp.zeros((bsz, S5_GROUPS, S5_STATE), jnp.float32)
        s5i0 = jnp.zeros((bsz, S5_GROUPS, S5_STATE), jnp.float32)
        fox_past = None
        nsa_past = None
    else:
        conv0, ssm0, s5r0, s5i0 = past['conv'], past['ssd'], past['s5_re'], past['s5_im']
        fox_past = (past['fox_kv'], past['fox_logf'])
        nsa_past = (past['nsa_kv'], past['nsa_win'])
    y_ssd, conv1, ssm1 = ssd_mixer(p_ssd, lp, conv0, ssm0)
    y_fox, fox_kv, fox_lf = fox_mixer(p_fox, lp, fox_past)
    y_nsa, nsa_kv, nsa_win = nsa_mixer(p_nsa, lp, nsa_past)
    y_s5, s5r1, s5i1 = s5_mixer(p_s5, lp, s5r0, s5i0)
    y = jnp.concatenate([y_ssd, y_fox, y_nsa, y_s5], axis=-1) @ lp['w_out']
    x = x + y
    x = x + 0.5 * swiglu(rms_norm(x, lp['ffn2_norm']), lp['ffn2_w1'], lp['ffn2_w3'], lp['ffn2_w2'])
    return x, (fox_kv, fox_lf, nsa_kv, nsa_win, ssm1, conv1, s5r1, s5i1)


def setup_inputs(seed: int = 0) -> dict:
    key = jax.random.key(seed)
    keys = iter(jax.random.split(key, 64))
    f32 = jnp.float32

    def normal(shape, scale=1.0):
        return jax.random.normal(next(keys), shape, f32) * scale

    def uniform(shape, lo, hi):
        return jax.random.uniform(next(keys), shape, f32, lo, hi)

    def gain(shape):
        return 1.0 + normal(shape, 0.01)

    n_pages = PAST_LEN // PAGE_SIZE
    n_pool = (5 * DEC_BATCH * n_pages) // 4
    win_buf = min(WINDOW, PAST_LEN)
    perm = jax.random.permutation(next(keys), n_pool)
    page_table = perm[:DEC_BATCH * n_pages].reshape(DEC_BATCH, n_pages).astype(jnp.int32)
    dt_ssd = jnp.exp(uniform((DEPTH, SSD_HEADS), math.log(1e-3), math.log(1e-1)))
    s5_shape = (DEPTH, S5_GROUPS, S5_STATE)
    return {
        'x_prompt': normal((BATCH, SEQ, D_MODEL)),
        'x_sample': normal((DEC_BATCH, DEC_SEQ, D_MODEL)),
        'cache_fox_kv': normal((n_pool, DEPTH, PAGE_SIZE, 2, FOX_HEADS, HEAD_DIM)),
        'cache_fox_logf': jax.nn.log_sigmoid(3.5 + normal((n_pool, DEPTH, PAGE_SIZE, FOX_HEADS))),
        'cache_nsa_kv': normal((n_pool, DEPTH, PAGE_SIZE, 4, NSA_KV_HEADS, HEAD_DIM)),
        'cache_nsa_win_kv': normal((DEC_BATCH, DEPTH, win_buf, 2, NSA_KV_HEADS, HEAD_DIM)),
        'state_ssd': normal((DEC_BATCH, DEPTH, SSD_HEADS, HEAD_DIM, SSD_STATE), 0.1),
        'state_ssd_conv': normal((DEC_BATCH, DEPTH, CONV_W - 1, SSD_CONV_DIM)),
        'state_s5_re': normal((DEC_BATCH, DEPTH, S5_GROUPS, S5_STATE), 0.3),
        'state_s5_im': normal((DEC_BATCH, DEPTH, S5_GROUPS, S5_STATE), 0.3),
        'page_table': page_table,
        'ffn1_norm': gain((DEPTH, D_MODEL)),
        'ffn1_w1': normal((DEPTH, D_MODEL, D_FF), D_MODEL ** -0.5),
        'ffn1_w3': normal((DEPTH, D_MODEL, D_FF), D_MODEL ** -0.5),
        'ffn1_w2': normal((DEPTH, D_FF, D_MODEL), D_FF ** -0.5),
        'mix_norm': gain((DEPTH, D_MODEL)),
        'w_in': normal((DEPTH, D_MODEL, D_IN), D_MODEL ** -0.5),
        'ssd_conv_w': normal((DEPTH, CONV_W, SSD_CONV_DIM), CONV_W ** -0.5),
        'ssd_conv_b': normal((DEPTH, SSD_CONV_DIM), 0.02),
        'ssd_dt_bias': dt_ssd + jnp.log(-jnp.expm1(-dt_ssd)),
        'ssd_a_log': jnp.log(uniform((DEPTH, SSD_HEADS), 1.0, 16.0)),
        'ssd_d': 1.0 + normal((DEPTH, SSD_HEADS), 0.1),
        'ssd_norm': gain((DEPTH, GROUP_WIDTH)),
        'fox_f_bias': uniform((DEPTH, FOX_HEADS), 1.0, 6.0),
        'fox_out_norm': gain((DEPTH, GROUP_WIDTH)),
        'nsa_cmp_pe': normal((DEPTH, 2, CMP_BLOCK, HEAD_DIM), 0.1),
        'nsa_cmp_w': normal((DEPTH, 2, CMP_BLOCK * HEAD_DIM, HEAD_DIM), (CMP_BLOCK * HEAD_DIM) ** -0.5),
        'nsa_out_norm': gain((DEPTH, GROUP_WIDTH)),
        's5_lambda_re': -0.5 + normal(s5_shape, 0.01),
        's5_lambda_im': jnp.pi * jnp.arange(S5_STATE, dtype=f32) + normal(s5_shape, 0.01),
        's5_log_dt': uniform((DEPTH, S5_GROUPS), math.log(1e-3), math.log(1e-1)),
        's5_b_re': normal((DEPTH, S5_GROUPS, S5_STATE, S5_CH), (2 * S5_CH) ** -0.5),
        's5_b_im': normal((DEPTH, S5_GROUPS, S5_STATE, S5_CH), (2 * S5_CH) ** -0.5),
        's5_c_re': normal((DEPTH, S5_GROUPS, S5_CH, S5_STATE), (2 * S5_STATE) ** -0.5),
        's5_c_im': normal((DEPTH, S5_GROUPS, S5_CH, S5_STATE), (2 * S5_STATE) ** -0.5),
        's5_d': normal((DEPTH, GROUP_WIDTH), 0.5),
        's5_w_glu': normal((DEPTH, GROUP_WIDTH, 2 * GROUP_WIDTH), GROUP_WIDTH ** -0.5),
        's5_out_norm': gain((DEPTH, GROUP_WIDTH)),
        'w_out': normal((DEPTH, D_MIX, D_MODEL), D_MIX ** -0.5),
        'ffn2_norm': gain((DEPTH, D_MODEL)),
        'ffn2_w1': normal((DEPTH, D_MODEL, D_FF), D_MODEL ** -0.5),
        'ffn2_w3': normal((DEPTH, D_MODEL, D_FF), D_MODEL ** -0.5),
        'ffn2_w2': normal((DEPTH, D_FF, D_MODEL), D_FF ** -0.5),
        'final_norm': gain((D_MODEL,)),
    }


def stack_layers(states, i):
    return jnp.stack([s[i] for s in states], axis=1)


def reference(x_prompt, x_sample, cache_fox_kv, cache_fox_logf, cache_nsa_kv, cache_nsa_win_kv,
              state_ssd, state_ssd_conv, state_s5_re, state_s5_im, page_table,
              ffn1_norm, ffn1_w1, ffn1_w3, ffn1_w2, mix_norm, w_in,
              ssd_conv_w, ssd_conv_b, ssd_dt_bias, ssd_a_log, ssd_d, ssd_norm,
              fox_f_bias, fox_out_norm, nsa_cmp_pe, nsa_cmp_w, nsa_out_norm,
              s5_lambda_re, s5_lambda_im, s5_log_dt, s5_b_re, s5_b_im, s5_c_re, s5_c_im,
              s5_d, s5_w_glu, s5_out_norm, w_out, ffn2_norm, ffn2_w1, ffn2_w3, ffn2_w2, final_norm):
    y_p, y_s = x_prompt, x_sample
    st_p, st_s = [], []
    for l in range(DEPTH):
        lp = {
            'ffn1_norm': ffn1_norm[l], 'ffn1_w1': ffn1_w1[l], 'ffn1_w3': ffn1_w3[l], 'ffn1_w2': ffn1_w2[l],
            'mix_norm': mix_norm[l], 'w_in': w_in[l],
            'ssd_conv_w': ssd_conv_w[l], 'ssd_conv_b': ssd_conv_b[l], 'ssd_dt_bias': ssd_dt_bias[l],
            'ssd_a_log': ssd_a_log[l], 'ssd_d': ssd_d[l], 'ssd_norm': ssd_norm[l],
            'fox_f_bias': fox_f_bias[l], 'fox_out_norm': fox_out_norm[l],
            'nsa_cmp_pe': nsa_cmp_pe[l], 'nsa_cmp_w': nsa_cmp_w[l], 'nsa_out_norm': nsa_out_norm[l],
            's5_lambda_re': s5_lambda_re[l], 's5_lambda_im': s5_lambda_im[l], 's5_log_dt': s5_log_dt[l],
            's5_b_re': s5_b_re[l], 's5_b_im': s5_b_im[l], 's5_c_re': s5_c_re[l], 's5_c_im': s5_c_im[l],
            's5_d': s5_d[l], 's5_w_glu': s5_w_glu[l], 's5_out_norm': s5_out_norm[l],
            'w_out': w_out[l],
            'ffn2_norm': ffn2_norm[l], 'ffn2_w1': ffn2_w1[l], 'ffn2_w3': ffn2_w3[l], 'ffn2_w2': ffn2_w2[l],
        }
        y_p, new_p = decoder_layer(y_p, lp, None)
        past = {
            'fox_kv': gather_pages(cache_fox_kv, page_table, l),
            'fox_logf': gather_pages(cache_fox_logf, page_table, l),
            'nsa_kv': gather_pages(cache_nsa_kv, page_table, l),
            'nsa_win': cache_nsa_win_kv[:, l],
            'ssd': state_ssd[:, l], 'conv': state_ssd_conv[:, l],
            's5_re': state_s5_re[:, l], 's5_im': state_s5_im[:, l],
        }
        y_s, new_s = decoder_layer(y_s, lp, past)
        st_p.append(new_p)
        st_s.append(new_s)
    y_p = rms_norm(y_p, final_norm)
    y_s = rms_norm(y_s, final_norm)
    return (y_p, y_s,
            stack_layers(st_p, 0), stack_layers(st_s, 0),
            stack_layers(st_p, 1), stack_layers(st_s, 1),
            stack_layers(st_p, 2), stack_layers(st_s, 2),
            stack_layers(st_p, 3), stack_layers(st_s, 3),
            stack_layers(st_p, 4), stack_layers(st_s, 4),
            stack_layers(st_p, 5), stack_layers(st_s, 5),
            stack_layers(st_p, 6), stack_layers(st_s, 6),
            stack_layers(st_p, 7), stack_layers(st_s, 7))
```

```python
import functools
import math

import jax
import jax.numpy as jnp
from jax import lax
from jax.experimental import pallas as pl
from jax.experimental.pallas import tpu as pltpu

F32 = jnp.float32
BF16 = jnp.bfloat16
HIGHEST = lax.Precision.HIGHEST

D_MODEL = 2048
DEPTH = 2
HEAD_DIM = 64
GW = D_MODEL // 4
D_FF = ((8 * D_MODEL // 3 + 127) // 128) * 128
EPS = 1e-6
TINY = 1e-30
SSD_HEADS = GW // HEAD_DIM
SSD_GROUPS = 2
SSD_STATE = 64
CONV_W = 4
SSD_CONV_DIM = GW + 2 * SSD_GROUPS * SSD_STATE
SSD_CHUNK = 128
FOX_HEADS = GW // HEAD_DIM
NSA_HEADS = GW // HEAD_DIM
NSA_KV_HEADS = 2
NSA_GROUP = NSA_HEADS // NSA_KV_HEADS
CMP_BLOCK = 32
SLC_BLOCK = 64
TOP_N = 16
WINDOW = 512
FORCE_SCORE = 1e4
S5_CH = 16
S5_GROUPS = GW // S5_CH
S5_STATE = 64
S5_N = S5_GROUPS * S5_STATE
PAGE = 128

LANES = 128
SUBLANES = 8
VMEM_LIMIT = 56 * 1024 * 1024

C_XBC = 0
C_DT = 768
C_FF = 896
C_Z = 1024
C_U = 1536
C_FQ = 2048
C_FK = 2560
C_FV = 3072
C_NQ = 3584
C_NKV = 4096
C_NG = 4864
P_W = 5120
FF_PAD = 5632
FF_TILE = 512


def _cparams(sem):
    return pltpu.CompilerParams(dimension_semantics=sem, vmem_limit_bytes=VMEM_LIMIT)


def _rms(x, g):
    ms = jnp.mean(x * x, axis=-1, keepdims=True)
    return x * lax.rsqrt(ms + EPS) * g


def _sigmoid(x):
    return 1.0 / (1.0 + jnp.exp(-x))


def _silu(x):
    return x * _sigmoid(x)


def _softplus(x):
    return jnp.maximum(x, 0.0) + jnp.log(1.0 + jnp.exp(-jnp.abs(x)))


def _log_sigmoid(x):
    return jnp.minimum(x, 0.0) - jnp.log(1.0 + jnp.exp(-jnp.abs(x)))


def _dot(a, b):
    return jnp.dot(a, b, preferred_element_type=F32)


def _dot_nt(a, b):
    return lax.dot_general(a, b, (((1,), (1,)), ((), ())), preferred_element_type=F32)


def _dot_tn(a, b):
    return lax.dot_general(a, b, (((0,), (0,)), ((), ())), preferred_element_type=F32)


def _ffn_body(x_ref, g_ref, w1_ref, w3_ref, w2_ref, fg_ref, o_ref, h_ref, acc_ref, *, final_norm):
    k = pl.program_id(1)

    @pl.when(k == 0)
    def _():
        h_ref[...] = _rms(x_ref[...], g_ref[...]).astype(BF16)
        acc_ref[...] = jnp.zeros_like(acc_ref)

    h = h_ref[...]
    a = _dot(h, w1_ref[...])
    b = _dot(h, w3_ref[...])
    act = (_silu(a) * b).astype(BF16)
    acc_ref[...] += _dot(act, w2_ref[...])

    @pl.when(k == pl.num_programs(1) - 1)
    def _():
        y = x_ref[...] + 0.5 * acc_ref[...]
        if final_norm:
            y = _rms(y, fg_ref[...])
        o_ref[...] = y


def ffn_call(x, g, w1, w3, w2, fg, *, tm, final_norm):
    m = x.shape[0]
    assert m % tm == 0 and w1.shape == (D_MODEL, FF_PAD)
    return pl.pallas_call(
        functools.partial(_ffn_body, final_norm=final_norm),
        grid=(m // tm, FF_PAD // FF_TILE),
        in_specs=[
            pl.BlockSpec((tm, D_MODEL), lambda i, k: (i, 0)),
            pl.BlockSpec((1, D_MODEL), lambda i, k: (0, 0)),
            pl.BlockSpec((D_MODEL, FF_TILE), lambda i, k: (0, k)),
            pl.BlockSpec((D_MODEL, FF_TILE), lambda i, k: (0, k)),
            pl.BlockSpec((FF_TILE, D_MODEL), lambda i, k: (k, 0)),
            pl.BlockSpec((1, D_MODEL), lambda i, k: (0, 0)),
        ],
        out_specs=pl.BlockSpec((tm, D_MODEL), lambda i, k: (i, 0)),
        out_shape=jax.ShapeDtypeStruct((m, D_MODEL), F32),
        scratch_shapes=[pltpu.VMEM((tm, D_MODEL), BF16), pltpu.VMEM((tm, D_MODEL), F32)],
        compiler_params=_cparams(("parallel", "arbitrary")),
        name="ffn",
    )(x, g, w1, w3, w2, fg)


def _inproj_body(x_ref, g_ref, w_ref, o_ref, h_ref):
    @pl.when(pl.program_id(1) == 0)
    def _():
        h_ref[...] = _rms(x_ref[...], g_ref[...]).astype(BF16)

    o_ref[...] = _dot(h_ref[...], w_ref[...])


def inproj_call(x, g, w, *, tm, tn=1024):
    m = x.shape[0]
    assert m % tm == 0 and P_W % tn == 0
    return pl.pallas_call(
        _inproj_body,
        grid=(m // tm, P_W // tn),
        in_specs=[
            pl.BlockSpec((tm, D_MODEL), lambda i, j: (i, 0)),
            pl.BlockSpec((1, D_MODEL), lambda i, j: (0, 0)),
            pl.BlockSpec((D_MODEL, tn), lambda i, j: (0, j)),
        ],
        out_specs=pl.BlockSpec((tm, tn), lambda i, j: (i, j)),
        out_shape=jax.ShapeDtypeStruct((m, P_W), F32),
        scratch_shapes=[pltpu.VMEM((tm, D_MODEL), BF16)],
        compiler_params=_cparams(("parallel", "arbitrary")),
        name="inproj",
    )(x, g, w)


def _outproj_body(x_ref, a_ref, b_ref, c_ref, d_ref, gn_ref, w_ref, o_ref, h_ref):
    j = pl.program_id(1)
    tn = o_ref.shape[1]

    @pl.when(j == 0)
    def _():
        for i, r in enumerate((a_ref, b_ref, c_ref, d_ref)):
            h_ref[:, i * GW:(i + 1) * GW] = _rms(r[...], gn_ref[i:i + 1, :]).astype(BF16)

    col = pl.multiple_of(j * tn, tn)
    o_ref[...] = x_ref[:, pl.ds(col, tn)] + _dot(h_ref[...], w_ref[...])


def outproj_call(x, ya, yb, yc, yd, gains, w, *, tm, tn=1024):
    m = x.shape[0]
    assert m % tm == 0
    yspec = pl.BlockSpec((tm, GW), lambda i, j: (i, 0))
    return pl.pallas_call(
        _outproj_body,
        grid=(m // tm, D_MODEL // tn),
        in_specs=[
            pl.BlockSpec((tm, D_MODEL), lambda i, j: (i, 0)),
            yspec, yspec, yspec, yspec,
            pl.BlockSpec((4, GW), lambda i, j: (0, 0)),
            pl.BlockSpec((D_MODEL, tn), lambda i, j: (0, j)),
        ],
        out_specs=pl.BlockSpec((tm, tn), lambda i, j: (i, j)),
        out_shape=jax.ShapeDtypeStruct((m, D_MODEL), F32),
        scratch_shapes=[pltpu.VMEM((tm, D_MODEL), BF16)],
        compiler_params=_cparams(("parallel", "arbitrary")),
        name="outproj",
    )(x, ya, yb, yc, yd, gains, w)


def _ssd_body(xbc_ref, dt_ref, z_ref, conv0_ref, ssm0_ref, cw_ref, cb_ref, dtb_ref, alog_ref, dd_ref,
              y_ref, ssm_ref, conv_ref, xp_ref, act_ref, st_ref, *, n_real):
    c = pl.program_id(1)
    nc = pl.num_programs(1)
    q = SSD_CHUNK
    halo = SUBLANES

    @pl.when(c == 0)
    def _():
        xp_ref[halo - 3:halo, :] = conv0_ref[...]
        st_ref[...] = ssm0_ref[...]

    xr = xbc_ref[...]
    xp_ref[halo:halo + q, :] = xr
    conv = (cb_ref[...] + cw_ref[3:4, :] * xr
            + cw_ref[2:3, :] * xp_ref[halo - 1:halo - 1 + q, :]
            + cw_ref[1:2, :] * xp_ref[halo - 2:halo - 2 + q, :]
            + cw_ref[0:1, :] * xp_ref[halo - 3:halo - 3 + q, :])
    act_ref[...] = _silu(conv)

    row = lax.broadcasted_iota(jnp.int32, (q, LANES), 0)
    dt = jnp.where(row < n_real, _softplus(dt_ref[...] + dtb_ref[...]), 0.0)
    a = -jnp.exp(alog_ref[...])
    ti = lax.broadcasted_iota(jnp.int32, (q, q), 0)
    si = lax.broadcasted_iota(jnp.int32, (q, q), 1)
    causal = si <= ti
    acs = jnp.dot(causal.astype(F32), dt * a, preferred_element_type=F32, precision=HIGHEST)
    acs_t = acs.T
    e_acs = jnp.exp(acs)
    acs_last = acs[q - 1:q, :]
    w_end = jnp.exp(acs_last - acs) * dt
    e_last = jnp.exp(acs_last)

    for g in range(SSD_GROUPS):
        bm = act_ref[:, GW + g * SSD_STATE:GW + (g + 1) * SSD_STATE]
        cm = act_ref[:, GW + (SSD_GROUPS + g) * SSD_STATE:GW + (SSD_GROUPS + g + 1) * SSD_STATE]
        bm16 = bm.astype(BF16)
        cm16 = cm.astype(BF16)
        cb = _dot_nt(cm16, bm16)
        for hh in range(SSD_HEADS // SSD_GROUPS):
            h = g * (SSD_HEADS // SSD_GROUPS) + hh
            xs = act_ref[:, h * HEAD_DIM:(h + 1) * HEAD_DIM]
            seg = acs[:, h:h + 1] - acs_t[h:h + 1, :]
            decay = jnp.exp(jnp.where(causal, seg, -jnp.inf))
            y = _dot((cb * decay).astype(BF16), (xs * dt[:, h:h + 1]).astype(BF16))
            s_in = st_ref[h]
            y = y + _dot_nt(cm16, s_in.astype(BF16)) * e_acs[:, h:h + 1]
            y = y + dd_ref[:, h:h + 1] * xs
            cs = _dot_tn((xs * w_end[:, h:h + 1]).astype(BF16), bm16)
            st_ref[h] = e_last[:, h:h + 1] * s_in + cs
            zs = z_ref[:, h * HEAD_DIM:(h + 1) * HEAD_DIM]
            y_ref[:, h * HEAD_DIM:(h + 1) * HEAD_DIM] = y * _silu(zs)

    last_real = min(n_real, q)
    conv_ref[...] = xp_ref[halo + last_real - 3:halo + last_real, :]
    xp_ref[halo - 3:halo, :] = xp_ref[halo + q - 3:halo + q, :]

    @pl.when(c == nc - 1)
    def _():
        ssm_ref[...] = st_ref[...]


def ssd_call(p, conv0, ssm0, cw, cb, dtb, alog, dd, *, bsz, t_len, n_real):
    q = SSD_CHUNK
    nc = t_len // q
    assert t_len % q == 0 and (nc == 1 or n_real == q)
    row = lambda b, c: b * nc + c
    vec = lambda shape: pl.BlockSpec(shape, lambda b, c: (0, 0))
    return pl.pallas_call(
        functools.partial(_ssd_body, n_real=n_real),
        grid=(bsz, nc),
        in_specs=[
            pl.BlockSpec((q, SSD_CONV_DIM), lambda b, c: (row(b, c), C_XBC // SSD_CONV_DIM)),
            pl.BlockSpec((q, LANES), lambda b, c: (row(b, c), C_DT // LANES)),
            pl.BlockSpec((q, GW), lambda b, c: (row(b, c), C_Z // GW)),
            pl.BlockSpec((None, CONV_W - 1, SSD_CONV_DIM), lambda b, c: (b, 0, 0)),
            pl.BlockSpec((None, SSD_HEADS, HEAD_DIM, SSD_STATE), lambda b, c: (b, 0, 0, 0)),
            vec((CONV_W, SSD_CONV_DIM)), vec((1, SSD_CONV_DIM)), vec((1, LANES)), vec((1, LANES)), vec((1, LANES)),
        ],
        out_specs=[
            pl.BlockSpec((q, GW), lambda b, c: (row(b, c), 0)),
            pl.BlockSpec((None, SSD_HEADS, HEAD_DIM, SSD_STATE), lambda b, c: (b, 0, 0, 0)),
            pl.BlockSpec((None, CONV_W - 1, SSD_CONV_DIM), lambda b, c: (b, 0, 0)),
        ],
        out_shape=[
            jax.ShapeDtypeStruct((bsz * t_len, GW), F32),
            jax.ShapeDtypeStruct((bsz, SSD_HEADS, HEAD_DIM, SSD_STATE), F32),
            jax.ShapeDtypeStruct((bsz, CONV_W - 1, SSD_CONV_DIM), F32),
        ],
        scratch_shapes=[
            pltpu.VMEM((SUBLANES + q, SSD_CONV_DIM), F32),
            pltpu.VMEM((q, SSD_CONV_DIM), F32),
            pltpu.VMEM((SSD_HEADS, HEAD_DIM, SSD_STATE), F32),
        ],
        compiler_params=_cparams(("parallel", "arbitrary")),
        name="ssd",
    )(p, p, p, conv0, ssm0, cw, cb, dtb, alog, dd)


def _s5_body(u_ref, bre_ref, bim_ref, are_ref, aim_ref, x0r_ref, x0i_ref, cre_ref, cim_ref, d_ref, wg_ref,
             o_ref, xr_out, xi_out, xr_ref, xi_ref, sr_ref, si_ref, *, nb, tiles, t_last):
    c = pl.program_id(0)
    per = SUBLANES // nb

    @pl.when(c == 0)
    def _():
        sr_ref[...] = x0r_ref[...]
        si_ref[...] = x0i_ref[...]

    u = u_ref[...]
    u16 = u.astype(BF16)
    xr_ref[...] = _dot(u16, bre_ref[...])
    xi_ref[...] = _dot(u16, bim_ref[...])
    ar = are_ref[...]
    ai = aim_ref[...]
    first = lax.broadcasted_iota(jnp.int32, (SUBLANES, S5_N), 0) < nb

    def step(j, carry):
        sr, si = carry
        rows = pl.ds(pl.multiple_of(j * SUBLANES, SUBLANES), SUBLANES)
        br = xr_ref[rows, :]
        bi = xi_ref[rows, :]
        vr = ar * sr - ai * si + br
        vi = ar * si + ai * sr + bi
        if per == 2:
            pr = pltpu.roll(vr, nb, 0)
            pi = pltpu.roll(vi, nb, 0)
            wr = ar * pr - ai * pi + br
            wi = ar * pi + ai * pr + bi
            outr = jnp.where(first, vr, wr)
            outi = jnp.where(first, vi, wi)
            nxt = (pltpu.roll(wr, nb, 0), pltpu.roll(wi, nb, 0))
        else:
            outr, outi, nxt = vr, vi, (vr, vi)
        xr_ref[rows, :] = outr
        xi_ref[rows, :] = outi

        @pl.when(c * tiles + j == t_last // per)
        def _():
            xr_out[...] = outr
            xi_out[...] = outi

        return nxt

    sr, si = lax.fori_loop(0, tiles, step, (sr_ref[...], si_ref[...]))
    sr_ref[...] = sr
    si_ref[...] = si

    y = _dot(xr_ref[...].astype(BF16), cre_ref[...]) - _dot(xi_ref[...].astype(BF16), cim_ref[...])
    y = y + d_ref[...] * u
    g = _dot(jax.nn.gelu(y).astype(BF16), wg_ref[...])
    o_ref[...] = g[:, :GW] * _sigmoid(g[:, GW:])


def s5_call(u, bre, bim, are, aim, x0r, x0i, cre, cim, d, wg, *, nb, t_len, steps, t_last):
    assert t_len % steps == 0 and nb in (4, 8)
    rows = steps * nb
    assert rows % SUBLANES == 0
    const = lambda shape: pl.BlockSpec(shape, lambda c: (0, 0))
    return pl.pallas_call(
        functools.partial(_s5_body, nb=nb, tiles=rows // SUBLANES, t_last=t_last),
        grid=(t_len // steps,),
        in_specs=[
            pl.BlockSpec((rows, GW), lambda c: (c, 0)),
            const((GW, S5_N)), const((GW, S5_N)), const((1, S5_N)), const((1, S5_N)),
            const((SUBLANES, S5_N)), const((SUBLANES, S5_N)),
            const((S5_N, GW)), const((S5_N, GW)), const((1, GW)), const((GW, 2 * GW)),
        ],
        out_specs=[pl.BlockSpec((rows, GW), lambda c: (c, 0)), const((SUBLANES, S5_N)), const((SUBLANES, S5_N))],
        out_shape=[jax.ShapeDtypeStruct((t_len * nb, GW), F32),
                   jax.ShapeDtypeStruct((SUBLANES, S5_N), F32), jax.ShapeDtypeStruct((SUBLANES, S5_N), F32)],
        scratch_shapes=[pltpu.VMEM((rows, S5_N), F32), pltpu.VMEM((rows, S5_N), F32),
                        pltpu.VMEM((SUBLANES, S5_N), F32), pltpu.VMEM((SUBLANES, S5_N), F32)],
        compiler_params=_cparams(("arbitrary",)),
        name="s5",
    )(u, bre, bim, are, aim, x0r, x0i, cre, cim, d, wg)


def s5_params(lam_re, lam_im, log_dt, b_re, b_im, c_re, c_im):
    dt = jnp.exp(log_dt)[:, None]
    mag = jnp.exp(lam_re * dt)
    ab_re = mag * jnp.cos(lam_im * dt)
    ab_im = mag * jnp.sin(lam_im * dt)
    den = lam_re * lam_re + lam_im * lam_im
    zr = ((ab_re - 1.0) * lam_re + ab_im * lam_im) / den
    zi = (ab_im * lam_re - (ab_re - 1.0) * lam_im) / den
    bb_re = zr[..., None] * b_re - zi[..., None] * b_im
    bb_im = zr[..., None] * b_im + zi[..., None] * b_re
    eye = jnp.eye(S5_GROUPS, dtype=F32)

    def in_mat(bb):
        return jnp.einsum('gnc,gh->gchn', bb, eye).reshape(GW, S5_N).astype(BF16)

    def out_mat(cc):
        return jnp.einsum('gcn,gh->gnhc', cc, eye).reshape(S5_N, GW).astype(BF16)

    return (in_mat(bb_re), in_mat(bb_im), ab_re.reshape(1, S5_N), ab_im.reshape(1, S5_N),
            out_mat(c_re), out_mat(c_im))


def _fox_prep_body(f_ref, b_ref, lf_ref, cum_ref, cumt_ref, carry_ref):
    c = pl.program_id(1)

    @pl.when(c == 0)
    def _():
        carry_ref[...] = jnp.zeros_like(carry_ref)

    lf = _log_sigmoid(f_ref[...] + b_ref[...])
    lf_ref[...] = lf
    tc = lf.shape[0]
    ti = lax.broadcasted_iota(jnp.int32, (tc, tc), 0)
    si = lax.broadcasted_iota(jnp.int32, (tc, tc), 1)
    cum = jnp.dot((si <= ti).astype(F32), lf, preferred_element_type=F32, precision=HIGHEST) + carry_ref[...]
    cum_ref[...] = cum
    cumt_ref[...] = cum.T[:FOX_HEADS, :]
    carry_ref[...] = cum[tc - 1:tc, :]


def fox_prep_call(p, bias, *, bsz, t_len, tc=256):
    nc = t_len // tc
    assert t_len % tc == 0
    return pl.pallas_call(
        _fox_prep_body,
        grid=(bsz, nc),
        in_specs=[pl.BlockSpec((tc, LANES), lambda b, c: (b * nc + c, C_FF // LANES)),
                  pl.BlockSpec((1, LANES), lambda b, c: (0, 0))],
        out_specs=[pl.BlockSpec((tc, LANES), lambda b, c: (b * nc + c, 0)),
                   pl.BlockSpec((tc, LANES), lambda b, c: (b * nc + c, 0)),
                   pl.BlockSpec((None, FOX_HEADS, tc), lambda b, c: (b, 0, c))],
        out_shape=[jax.ShapeDtypeStruct((bsz * t_len, LANES), F32),
                   jax.ShapeDtypeStruct((bsz * t_len, LANES), F32),
                   jax.ShapeDtypeStruct((bsz, FOX_HEADS, t_len), F32)],
        scratch_shapes=[pltpu.VMEM((1, LANES), F32)],
        compiler_params=_cparams(("parallel", "arbitrary")),
        name="fox_prep",
    )(p, bias)


def _fox_attn_body(q_ref, k_ref, v_ref, cum_ref, cumt_ref, o_ref, *, tq):
    hp = pl.program_id(1)
    qi = pl.program_id(2)
    tk = tq
    scale = HEAD_DIM ** -0.5
    lane = lax.broadcasted_iota(jnp.int32, (tq, LANES), 1)
    q = q_ref[...]
    cum = cum_ref[...]
    t_pos = qi * tq + lax.broadcasted_iota(jnp.int32, (tq, tk), 0)
    s_off = lax.broadcasted_iota(jnp.int32, (tq, tk), 1)
    res = []
    for hh in range(2):
        h = 2 * hp + hh
        qm = jnp.where((lane >= hh * HEAD_DIM) & (lane < (hh + 1) * HEAD_DIM), q, 0.0).astype(BF16)
        cq = jnp.sum(jnp.where(lane == h, cum, 0.0), axis=1, keepdims=True)

        def body(kc, carry, qm=qm, cq=cq, h=h):
            m, l, acc = carry
            ks = pl.ds(pl.multiple_of(kc * tk, tk), tk)
            s = _dot_nt(qm, k_ref[ks, :].astype(BF16)) * scale
            s = s + (cq - cumt_ref[pl.ds(h, 1), ks])
            s = jnp.where(kc * tk + s_off <= t_pos, s, -jnp.inf)
            m_new = jnp.maximum(m, jnp.max(s, axis=1, keepdims=True))
            alpha = jnp.exp(m - m_new)
            pr = jnp.exp(s - m_new)
            l = alpha * l + jnp.sum(pr, axis=1, keepdims=True)
            acc = alpha * acc + _dot(pr.astype(BF16), v_ref[ks, :].astype(BF16))
            return m_new, l, acc

        init = (jnp.full((tq, 1), -jnp.inf, F32), jnp.zeros((tq, 1), F32), jnp.zeros((tq, LANES), F32))
        m, l, acc = lax.fori_loop(0, qi + 1, body, init)
        res.append(acc / l)
    o_ref[...] = jnp.where(lane < HEAD_DIM, res[0], res[1])


def fox_attn_call(p, cum, cumt, *, bsz, t_len, tq=256):
    nq = t_len // tq
    assert t_len % tq == 0
    pairs = FOX_HEADS // 2
    return pl.pallas_call(
        functools.partial(_fox_attn_body, tq=tq),
        grid=(bsz, pairs, nq),
        in_specs=[
            pl.BlockSpec((tq, LANES), lambda b, hp, qi: (b * nq + qi, C_FQ // LANES + hp)),
            pl.BlockSpec((t_len, LANES), lambda b, hp, qi: (b, C_FK // LANES + hp)),
            pl.BlockSpec((t_len, LANES), lambda b, hp, qi: (b, C_FV // LANES + hp)),
            pl.BlockSpec((tq, LANES), lambda b, hp, qi: (b * nq + qi, 0)),
            pl.BlockSpec((None, FOX_HEADS, t_len), lambda b, hp, qi: (b, 0, 0)),
        ],
        out_specs=pl.BlockSpec((tq, LANES), lambda b, hp, qi: (b * nq + qi, hp)),
        out_shape=jax.ShapeDtypeStruct((bsz * t_len, GW), F32),
        compiler_params=_cparams(("parallel", "parallel", "arbitrary")),
        name="fox_attn",
    )(p, p, p, cum, cumt)


def _compress_rows(xk_ref, xv_ref, pe_ref, w_ref, nbc):
    half = nbc // 2
    acc = jnp.zeros((nbc, 2 * LANES), F32)
    for r in range(CMP_BLOCK):
        ev = pl.ds(r, half, stride=2 * CMP_BLOCK)
        od = pl.ds(CMP_BLOCK + r, half, stride=2 * CMP_BLOCK)
        rows = jnp.concatenate([jnp.concatenate([xk_ref[ev, :], xv_ref[ev, :]], axis=1),
                                jnp.concatenate([xk_ref[od, :], xv_ref[od, :]], axis=1)], axis=0)
        acc = acc + _dot((rows + pe_ref[r:r + 1, :]).astype(BF16), w_ref[r])
    return acc


def _nsa_cmp_p_body(xk_ref, xv_ref, pe_ref, w_ref, o_ref, *, nbc, hp):
    half = nbc // 2
    acc = _compress_rows(xk_ref, xv_ref, pe_ref, w_ref, nbc)
    o_ref[...] = jnp.zeros_like(o_ref)
    o_ref[0:half, :] = acc[0:half]
    o_ref[hp:hp + half, :] = acc[half:nbc]


def nsa_cmp_p_call(p, pe4, w4, *, bsz, t_len):
    nbc = t_len // CMP_BLOCK
    hp = max(nbc // 2, HEAD_DIM)
    return pl.pallas_call(
        functools.partial(_nsa_cmp_p_body, nbc=nbc, hp=hp),
        grid=(bsz,),
        in_specs=[pl.BlockSpec((t_len, LANES), lambda b: (b, C_NKV // LANES)),
                  pl.BlockSpec((t_len, LANES), lambda b: (b, C_NKV // LANES + 1)),
                  pl.BlockSpec((CMP_BLOCK, 2 * LANES), lambda b: (0, 0)),
                  pl.BlockSpec((CMP_BLOCK, 2 * LANES, 2 * LANES), lambda b: (0, 0, 0))],
        out_specs=pl.BlockSpec((None, 2 * hp, 2 * LANES), lambda b: (b, 0, 0)),
        out_shape=jax.ShapeDtypeStruct((bsz, 2 * hp, 2 * LANES), F32),
        compiler_params=_cparams(("parallel",)),
        name="nsa_cmp_p",
    )(p, p, pe4, w4)


def _cmp_attend(qs, cmp_ref, g, t_pos, hp, half):
    scale = HEAD_DIM ** -0.5
    kc = cmp_ref[:, g * HEAD_DIM:(g + 1) * HEAD_DIM].astype(BF16)
    vc = cmp_ref[:, 2 * HEAD_DIM + g * HEAD_DIM:2 * HEAD_DIM + (g + 1) * HEAD_DIM].astype(BF16)
    s = _dot_nt(qs, kc) * scale
    col = lax.broadcasted_iota(jnp.int32, s.shape, 1)
    slot = jnp.where(col < hp, col, col - hp)
    blk = 2 * slot + jnp.where(col < hp, 0, 1)
    valid = (slot < half) & ((blk + 1) * CMP_BLOCK - 1 <= t_pos)
    s = jnp.where(valid, s, -jnp.inf)
    m = jnp.max(s, axis=1, keepdims=True)
    m = jnp.where(m > -jnp.inf, m, 0.0)
    e = jnp.where(valid, jnp.exp(s - m), 0.0)
    pc = e / jnp.maximum(jnp.sum(e, axis=1, keepdims=True), TINY)
    return _dot(pc.astype(BF16), vc), pc


def _select_mask(imp, t_pos):
    tq, w = imp.shape
    blk = lax.broadcasted_iota(jnp.int32, (tq, w), 1)
    cur = t_pos // SLC_BLOCK
    forced = (blk == 0) | (blk == cur) | (blk == cur - 1)
    avail = blk * SLC_BLOCK <= t_pos
    score = jnp.where(avail, jnp.where(forced, FORCE_SCORE, imp), -1.0)
    return score, blk


def _rank_lt(score, blk, n_blocks, top_n):
    rank = jnp.zeros(score.shape, jnp.int32)
    for i in range(n_blocks):
        col = score[:, i:i + 1]
        ahead = (col > score) | ((col == score) & (i < blk))
        rank = rank + ahead.astype(jnp.int32)
    return (rank < top_n) & (score >= 0.0)


def _masked_flash(qs, k_ref, v_ref, g, lo, hi, tk, mask_fn):
    scale = HEAD_DIM ** -0.5
    r = qs.shape[0]
    neg = -1e30

    def body(kc, carry):
        m, l, acc = carry
        ks = pl.ds(pl.multiple_of(kc * tk, tk), tk)
        kk = k_ref[ks, g * HEAD_DIM:(g + 1) * HEAD_DIM].astype(BF16)
        vv = v_ref[ks, g * HEAD_DIM:(g + 1) * HEAD_DIM].astype(BF16)
        msk = mask_fn(kc)
        s = jnp.where(msk, _dot_nt(qs, kk) * scale, neg)
        m_new = jnp.maximum(m, jnp.max(s, axis=1, keepdims=True))
        alpha = jnp.exp(m - m_new)
        pr = jnp.where(msk, jnp.exp(s - m_new), 0.0)
        l = alpha * l + jnp.sum(pr, axis=1, keepdims=True)
        acc = alpha * acc + _dot(pr.astype(BF16), vv)
        return m_new, l, acc

    init = (jnp.full((r, 1), neg, F32), jnp.zeros((r, 1), F32), jnp.zeros((r, HEAD_DIM), F32))
    m, l, acc = lax.fori_loop(lo, hi, body, init)
    return acc / l


def _nsa_attn_p_body(q_ref, cmp_ref, ks_ref, vs_ref, kw_ref, vw_ref, gl_ref, o_ref, *, tq, nbc, hp):
    qi = pl.program_id(1)
    hg_n = NSA_GROUP
    rows = hg_n * tq
    tk = 2 * LANES
    half = nbc // 2
    t_col = qi * tq + lax.broadcasted_iota(jnp.int32, (tq, 1), 0)
    t_stack = jnp.concatenate([t_col] * hg_n, axis=0)
    gate = _sigmoid(gl_ref[...])
    n_sel_blocks = (qi * tq + tq + SLC_BLOCK - 1) // SLC_BLOCK

    for g in range(NSA_KV_HEADS):
        qs = jnp.concatenate(
            [q_ref[:, (g * hg_n + hg) * HEAD_DIM:(g * hg_n + hg + 1) * HEAD_DIM] for hg in range(hg_n)],
            axis=0).astype(BF16)

        o_cmp, pc = _cmp_attend(qs, cmp_ref, g, t_stack, hp, half)
        imp = pc[0:tq]
        for hg in range(1, hg_n):
            imp = imp + pc[hg * tq:(hg + 1) * tq]
        imp = imp[:, 0:hp] + imp[:, hp:2 * hp]
        if hp < LANES:
            imp = jnp.concatenate([imp, jnp.zeros((tq, LANES - hp), F32)], axis=1)
        score, blk = _select_mask(imp, t_col)
        sel = _rank_lt(score, blk, (nbc + 1) // 2, TOP_N).astype(BF16)
        w = sel.shape[1]

        def slc_mask(kc, sel=sel, w=w):
            jb = lax.broadcasted_iota(jnp.int32, (w, tk), 0)
            sp = kc * tk + lax.broadcasted_iota(jnp.int32, (w, tk), 1)
            expand = (jb == sp // SLC_BLOCK).astype(BF16)
            hit = _dot(sel, expand) > 0.5
            s_pos = kc * tk + lax.broadcasted_iota(jnp.int32, (tq, tk), 1)
            mk = hit & (s_pos <= t_col)
            return jnp.concatenate([mk] * hg_n, axis=0)

        o_slc = _masked_flash(qs, ks_ref, vs_ref, g, 0, (qi * tq + tq + tk - 1) // tk, tk, slc_mask)

        def win_mask(kc):
            s_pos = kc * tq + lax.broadcasted_iota(jnp.int32, (tq, tq), 1)
            diff = t_col - s_pos
            mk = (diff >= 0) & (diff < WINDOW)
            return jnp.concatenate([mk] * hg_n, axis=0)

        o_win = _masked_flash(qs, kw_ref, vw_ref, g, jnp.maximum(qi - WINDOW // tq, 0), qi + 1, tq, win_mask)

        for hg in range(hg_n):
            h = g * hg_n + hg
            rs = slice(hg * tq, (hg + 1) * tq)
            o = (gate[:, 3 * h:3 * h + 1] * o_cmp[rs] + gate[:, 3 * h + 1:3 * h + 2] * o_slc[rs]
                 + gate[:, 3 * h + 2:3 * h + 3] * o_win[rs])
            o_ref[:, h * HEAD_DIM:(h + 1) * HEAD_DIM] = o


def nsa_attn_p_call(p, cmp, *, bsz, t_len, tq=128):
    nq = t_len // tq
    nbc = t_len // CMP_BLOCK
    hp = cmp.shape[1] // 2
    assert t_len % (2 * LANES) == 0 and WINDOW % tq == 0
    kv = lambda off: pl.BlockSpec((t_len, LANES), lambda b, qi: (b, (C_NKV + off) // LANES))
    return pl.pallas_call(
        functools.partial(_nsa_attn_p_body, tq=tq, nbc=nbc, hp=hp),
        grid=(bsz, nq),
        in_specs=[
            pl.BlockSpec((tq, GW), lambda b, qi: (b * nq + qi, C_NQ // GW)),
            pl.BlockSpec((None, 2 * hp, 2 * LANES), lambda b, qi: (b, 0, 0)),
            kv(2 * LANES), kv(3 * LANES), kv(4 * LANES), kv(5 * LANES),
            pl.BlockSpec((tq, LANES), lambda b, qi: (b * nq + qi, C_NG // LANES)),
        ],
        out_specs=pl.BlockSpec((tq, GW), lambda b, qi: (b * nq + qi, 0)),
        out_shape=jax.ShapeDtypeStruct((bsz * t_len, GW), F32),
        compiler_params=_cparams(("parallel", "arbitrary")),
        name="nsa_attn_p",
    )(p, cmp, p, p, p, p, p)


def nsa_cmp_params(pe, cw):
    pe4 = jnp.concatenate([pe[0], pe[0], pe[1], pe[1]], axis=1)
    wk = cw[0].reshape(CMP_BLOCK, HEAD_DIM, HEAD_DIM)
    wv = cw[1].reshape(CMP_BLOCK, HEAD_DIM, HEAD_DIM)
    z = jnp.zeros_like(wk)
    rows = [jnp.concatenate([m if i == j else z for j in range(4)], axis=2) for i, m in enumerate((wk, wk, wv, wv))]
    return pe4, jnp.concatenate(rows, axis=1).astype(BF16)


T_PAD = SUBLANES
FOX_LF_PAGES = 16
FOX_KV_PAGES = 8
NSA_CMP_PAGES = 16


def _page_spec(block, layer, pages_per_step, i, tail, first_step=0):
    def index_map(b, s, pt):
        j = jnp.maximum(s - first_step, 0) * pages_per_step + i
        return (pt[b, j], layer) + tail
    return pl.BlockSpec(block, index_map)


def _fox_prep_s_body(pt_ref, *refs, n_pages):
    pg = FOX_LF_PAGES
    page_refs, o_ref, a_ref = refs[:pg], refs[pg], refs[pg + 1]
    s = pl.program_id(1)
    eye = (lax.broadcasted_iota(jnp.int32, (PAGE, PAGE), 0)
           == lax.broadcasted_iota(jnp.int32, (PAGE, PAGE), 1)).astype(F32)
    for i in range(pg):
        lft = lax.dot_general(page_refs[i][...], eye, (((0,), (0,)), ((), ())),
                              preferred_element_type=F32, precision=HIGHEST)
        a_ref[pl.ds(pl.multiple_of((s * pg + i) * FOX_HEADS, FOX_HEADS), FOX_HEADS), :] = lft

    @pl.when(s == pl.num_programs(1) - 1)
    def _():
        n = n_pages * FOX_HEADS
        a = a_ref[...]
        ji = lax.broadcasted_iota(jnp.int32, (PAGE, PAGE), 0)
        si = lax.broadcasted_iota(jnp.int32, (PAGE, PAGE), 1)
        within = jnp.dot(a, (ji > si).astype(F32), preferred_element_type=F32, precision=HIGHEST)
        tot = jnp.broadcast_to(jnp.sum(a, axis=1, keepdims=True), (n, LANES))
        ri = lax.broadcasted_iota(jnp.int32, (n, n), 0)
        ci = lax.broadcasted_iota(jnp.int32, (n, n), 1)
        later = ((ci > ri) & ((ci - ri) % FOX_HEADS == 0)).astype(F32)
        o_ref[...] = within + jnp.dot(later, tot, preferred_element_type=F32, precision=HIGHEST)


def fox_prep_s_call(page_table, cache_lf, *, layer, bsz, n_pages):
    pg = FOX_LF_PAGES
    assert n_pages % pg == 0
    n = n_pages * FOX_HEADS
    grid_spec = pltpu.PrefetchScalarGridSpec(
        num_scalar_prefetch=1,
        grid=(bsz, n_pages // pg),
        in_specs=[_page_spec((None, None, PAGE, FOX_HEADS), layer, pg, i, (0, 0)) for i in range(pg)],
        out_specs=pl.BlockSpec((None, n, LANES), lambda b, s, pt: (b, 0, 0)),
        scratch_shapes=[pltpu.VMEM((n, LANES), F32)],
    )
    return pl.pallas_call(
        functools.partial(_fox_prep_s_body, n_pages=n_pages),
        grid_spec=grid_spec,
        out_shape=jax.ShapeDtypeStruct((bsz, n, LANES), F32),
        compiler_params=_cparams(("parallel", "arbitrary")),
        name="fox_prep_s",
    )(page_table, *([cache_lf] * pg))


def _fox_attn_s_body(pt_ref, q_ref, k_ref, v_ref, f_ref, b_ref, rp_ref, *refs, n_real):
    pg = FOX_KV_PAGES
    page_refs = refs[:pg]
    o_ref, lf_ref, qbd_ref, rqp_ref, m_ref, l_ref, acc_ref = refs[pg:]
    s = pl.program_id(1)
    scale = HEAD_DIM ** -0.5
    nh = FOX_HEADS
    rows = nh * T_PAD
    row_h = lax.broadcasted_iota(jnp.int32, (rows, 1), 0) // T_PAD
    row_t = lax.broadcasted_iota(jnp.int32, (rows, 1), 0) % T_PAD

    def rep_heads(x8):
        return jnp.concatenate([jnp.broadcast_to(x8[h:h + 1, :], (T_PAD, x8.shape[1])) for h in range(nh)], axis=0)

    def attend(kk, vv, bias, mask, m, l, acc):
        sc = _dot_nt(qbd_ref[...], kk.astype(BF16)) * scale + bias
        if mask is not None:
            sc = jnp.where(mask, sc, -jnp.inf)
        m_new = jnp.maximum(m, jnp.max(sc, axis=1, keepdims=True))
        alpha = jnp.exp(m - m_new)
        pr = jnp.exp(sc - m_new)
        return m_new, alpha * l + jnp.sum(pr, axis=1, keepdims=True), alpha * acc + _dot(pr.astype(BF16), vv.astype(BF16))

    @pl.when(s == 0)
    def _():
        lf = _log_sigmoid(f_ref[...] + b_ref[...])
        lf_ref[...] = lf
        tok = lax.broadcasted_iota(jnp.int32, (T_PAD, LANES), 0)
        lfm = jnp.where(tok < n_real, lf, 0.0)
        r_new = jnp.zeros((T_PAD, LANES), F32)
        for j in range(1, n_real):
            r_new = r_new + jnp.where(tok < j, lfm[j:j + 1, :], 0.0)
        tot = jnp.sum(lfm, axis=0, keepdims=True)
        lane = lax.broadcasted_iota(jnp.int32, (rows, LANES), 1)
        pick = lane == row_h
        r_q = jnp.sum(jnp.where(pick, jnp.concatenate([r_new] * nh, axis=0), 0.0), axis=1, keepdims=True)
        t_q = jnp.sum(jnp.where(pick, jnp.broadcast_to(tot, (rows, LANES)), 0.0), axis=1, keepdims=True)
        rqp_ref[...] = r_q - t_q
        col = lax.broadcasted_iota(jnp.int32, (rows, GW), 1)
        qbd = jnp.where(col // HEAD_DIM == row_h, jnp.concatenate([q_ref[...]] * nh, axis=0), 0.0)
        qbd_ref[...] = qbd.astype(BF16)
        zrow = jnp.zeros((PAGE - T_PAD, GW), F32)
        kk = jnp.concatenate([k_ref[...], zrow], axis=0)
        vv = jnp.concatenate([v_ref[...], zrow], axis=0)
        r_pad = jnp.concatenate([r_new, jnp.zeros((PAGE - T_PAD, LANES), F32)], axis=0)
        bias = rep_heads(r_pad.T[:nh, :]) - r_q
        key = lax.broadcasted_iota(jnp.int32, (rows, PAGE), 1)
        mask = (key <= row_t) & (key < n_real)
        m0 = jnp.full((rows, 1), -jnp.inf, F32)
        m, l, acc = attend(kk, vv, bias, mask, m0, jnp.zeros((rows, 1), F32), jnp.zeros((rows, GW), F32))
        m_ref[...] = m
        l_ref[...] = l
        acc_ref[...] = acc

    @pl.when(s > 0)
    def _():
        m, l, acc = m_ref[...], l_ref[...], acc_ref[...]
        rqp = rqp_ref[...]
        for i in range(pg):
            page = page_refs[i]
            bias = rep_heads(rp_ref[i * nh:(i + 1) * nh, :]) - rqp
            m, l, acc = attend(page[:, 0:GW], page[:, GW:2 * GW], bias, None, m, l, acc)
        m_ref[...] = m
        l_ref[...] = l
        acc_ref[...] = acc

    @pl.when(s == pl.num_programs(1) - 1)
    def _():
        col = lax.broadcasted_iota(jnp.int32, (rows, GW), 1)
        o = jnp.where(col // HEAD_DIM == row_h, acc_ref[...] / l_ref[...], 0.0)
        o_ref[...] = jnp.sum(o.reshape(nh, T_PAD, GW), axis=0)


def fox_attn_s_call(page_table, p, bias, r_past, cache_kv, *, layer, bsz, n_pages, n_real):
    pg = FOX_KV_PAGES
    assert n_pages % pg == 0
    rows = FOX_HEADS * T_PAD
    new = lambda width, off: pl.BlockSpec((T_PAD, width), lambda b, s, pt: (b, off // width))
    grid_spec = pltpu.PrefetchScalarGridSpec(
        num_scalar_prefetch=1,
        grid=(bsz, 1 + n_pages // pg),
        in_specs=[new(GW, C_FQ), new(GW, C_FK), new(GW, C_FV), new(LANES, C_FF),
                  pl.BlockSpec((1, LANES), lambda b, s, pt: (0, 0)),
                  pl.BlockSpec((None, pg * FOX_HEADS, LANES), lambda b, s, pt: (b, jnp.maximum(s - 1, 0), 0))]
                 + [_page_spec((None, None, PAGE, 2 * GW), layer, pg, i, (0, 0), first_step=1) for i in range(pg)],
        out_specs=[pl.BlockSpec((T_PAD, GW), lambda b, s, pt: (b, 0)),
                   pl.BlockSpec((T_PAD, LANES), lambda b, s, pt: (b, 0))],
        scratch_shapes=[pltpu.VMEM((rows, GW), BF16), pltpu.VMEM((rows, 1), F32), pltpu.VMEM((rows, 1), F32),
                        pltpu.VMEM((rows, 1), F32), pltpu.VMEM((rows, GW), F32)],
    )
    return pl.pallas_call(
        functools.partial(_fox_attn_s_body, n_real=n_real),
        grid_spec=grid_spec,
        out_shape=[jax.ShapeDtypeStruct((bsz * T_PAD, GW), F32), jax.ShapeDtypeStruct((bsz * T_PAD, LANES), F32)],
        compiler_params=_cparams(("parallel", "arbitrary")),
        name="fox_attn_s",
    )(page_table, p, p, p, p, bias, r_past, *([cache_kv] * pg))


def _nsa_cmp_s_body(pt_ref, *refs, n_pages):
    pg = NSA_CMP_PAGES
    k_pages, v_pages = refs[:pg], refs[pg:2 * pg]
    pe_ref, w_ref, o_ref, xk_ref, xv_ref = refs[2 * pg:]
    s = pl.program_id(1)
    for i in range(pg):
        rows = pl.ds(pl.multiple_of((s * pg + i) * PAGE, PAGE), PAGE)
        xk_ref[rows, :] = k_pages[i][...]
        xv_ref[rows, :] = v_pages[i][...]

    @pl.when(s == pl.num_programs(1) - 1)
    def _():
        o_ref[...] = _compress_rows(xk_ref, xv_ref, pe_ref, w_ref, n_pages * PAGE // CMP_BLOCK)


def nsa_cmp_s_call(page_table, cache_kv, pe4, w4, *, layer, bsz, n_pages):
    pg = NSA_CMP_PAGES
    assert n_pages % pg == 0
    nbc = n_pages * PAGE // CMP_BLOCK
    grid_spec = pltpu.PrefetchScalarGridSpec(
        num_scalar_prefetch=1,
        grid=(bsz, n_pages // pg),
        in_specs=[_page_spec((None, None, PAGE, LANES), layer, pg, i, (0, 0)) for i in range(pg)]
                 + [_page_spec((None, None, PAGE, LANES), layer, pg, i, (0, 1)) for i in range(pg)]
                 + [pl.BlockSpec((CMP_BLOCK, 2 * LANES), lambda b, s, pt: (0, 0)),
                    pl.BlockSpec((CMP_BLOCK, 2 * LANES, 2 * LANES), lambda b, s, pt: (0, 0, 0))],
        out_specs=pl.BlockSpec((None, nbc, 2 * LANES), lambda b, s, pt: (b, 0, 0)),
        scratch_shapes=[pltpu.VMEM((n_pages * PAGE, LANES), F32), pltpu.VMEM((n_pages * PAGE, LANES), F32)],
    )
    return pl.pallas_call(
        functools.partial(_nsa_cmp_s_body, n_pages=n_pages),
        grid_spec=grid_spec,
        out_shape=jax.ShapeDtypeStruct((bsz, nbc, 2 * LANES), F32),
        compiler_params=_cparams(("parallel", "arbitrary")),
        name="nsa_cmp_s",
    )(page_table, *([cache_kv] * (2 * pg)), pe4, w4)


def _stack_heads(q_ref, g):
    hg_n = NSA_GROUP
    return jnp.concatenate(
        [q_ref[:, (g * hg_n + hg) * HEAD_DIM:(g * hg_n + hg + 1) * HEAD_DIM] for hg in range(hg_n)], axis=0)


def _nsa_topk_s_body(q_ref, cmp_ref, o_ref, idx_ref, *, q_off, nbc):
    hg_n = NSA_GROUP
    half = nbc // 2
    w = half + LANES
    t_col = q_off + lax.broadcasted_iota(jnp.int32, (T_PAD, 1), 0)
    t_stack = jnp.concatenate([t_col] * hg_n, axis=0)
    ii = lax.broadcasted_iota(jnp.int32, (w, w), 0)
    jj = lax.broadcasted_iota(jnp.int32, (w, w), 1)
    rr = lax.broadcasted_iota(jnp.int32, (TOP_N, w), 0)
    jr = lax.broadcasted_iota(jnp.int32, (TOP_N, w), 1).astype(F32)
    lane = lax.broadcasted_iota(jnp.int32, (TOP_N, LANES), 1)
    for g in range(NSA_KV_HEADS):
        qs = _stack_heads(q_ref, g).astype(BF16)
        o_cmp, pc = _cmp_attend(qs, cmp_ref, g, t_stack, half, half)
        for hg in range(hg_n):
            h = g * hg_n + hg
            o_ref[:, h * HEAD_DIM:(h + 1) * HEAD_DIM] = o_cmp[hg * T_PAD:(hg + 1) * T_PAD]
        imp = pc[0:T_PAD]
        for hg in range(1, hg_n):
            imp = imp + pc[hg * T_PAD:(hg + 1) * T_PAD]
        imp = jnp.concatenate([imp[:, 0:half] + imp[:, half:2 * half], jnp.zeros((T_PAD, LANES), F32)], axis=1)
        score, _ = _select_mask(imp, t_col)
        score_t = jnp.concatenate([score, jnp.zeros((LANES - T_PAD, w), F32)], axis=0).T
        out = jnp.zeros((TOP_N, LANES), F32)
        for t in range(T_PAD):
            col = score_t[:, t:t + 1]
            row = score[t:t + 1, :]
            ahead = (col > row) | ((col == row) & (ii < jj))
            rank = jnp.sum(ahead.astype(F32), axis=0, keepdims=True)
            hit = (rank == rr.astype(F32)) & (row >= 0.0)
            found = jnp.sum(hit.astype(F32), axis=1, keepdims=True)
            which = jnp.sum(jnp.where(hit, jr, 0.0), axis=1, keepdims=True)
            out = jnp.where(lane == t, jnp.where(found > 0.5, which, -1.0), out)
        idx_ref[g] = out.astype(jnp.int32)


def nsa_topk_s_call(p, cmp, *, bsz, q_off):
    nbc = cmp.shape[1]
    return pl.pallas_call(
        functools.partial(_nsa_topk_s_body, q_off=q_off, nbc=nbc),
        grid=(bsz,),
        in_specs=[pl.BlockSpec((T_PAD, GW), lambda b: (b, C_NQ // GW)),
                  pl.BlockSpec((None, nbc, 2 * LANES), lambda b: (b, 0, 0))],
        out_specs=[pl.BlockSpec((T_PAD, GW), lambda b: (b, 0)),
                   pl.BlockSpec((None, NSA_KV_HEADS, TOP_N, LANES), lambda b: (b, 0, 0, 0))],
        out_shape=[jax.ShapeDtypeStruct((bsz * T_PAD, GW), F32),
                   jax.ShapeDtypeStruct((bsz, NSA_KV_HEADS, TOP_N, LANES), jnp.int32)],
        compiler_params=_cparams(("parallel",)),
        name="nsa_topk_s",
    )(p, cmp)


def _nsa_sel_s_body(pt_ref, idx_ref, q_ref, kn_ref, vn_ref, *refs, n_real, n_tok, q_off):
    k_blocks, v_blocks, o_ref = refs[:TOP_N], refs[TOP_N:2 * TOP_N], refs[2 * TOP_N]
    b, g, t = pl.program_id(0), pl.program_id(1), pl.program_id(2)
    scale = HEAD_DIM ** -0.5
    hg_n = NSA_GROUP
    base = ((b * NSA_KV_HEADS + g) * n_tok + t) * TOP_N
    new_blk = q_off // SLC_BLOCK
    qrow = q_ref[pl.ds(t, 1), :]
    qs = jnp.concatenate([qrow[:, hg * HEAD_DIM:(hg + 1) * HEAD_DIM] for hg in range(hg_n)]
                         + [jnp.zeros((SUBLANES - hg_n, HEAD_DIM), F32)], axis=0).astype(BF16)

    def pick(blk):
        return jnp.where(g == 0, blk[:, 0:HEAD_DIM], blk[:, HEAD_DIM:2 * HEAD_DIM])

    logits, values = [], []
    has_new = jnp.bool_(False)
    for r in range(TOP_N):
        j = idx_ref[base + r]
        from_cache = (j >= 0) & (j < new_blk)
        has_new = has_new | (j == new_blk)
        sc = _dot_nt(qs, pick(k_blocks[r][...]).astype(BF16)) * scale
        logits.append(jnp.where(from_cache, sc, -jnp.inf))
        values.append(pick(v_blocks[r][...]).astype(BF16))
    sc = _dot_nt(qs, pick(kn_ref[...]).astype(BF16)) * scale
    key = lax.broadcasted_iota(jnp.int32, (SUBLANES, T_PAD), 1)
    logits.append(jnp.where(has_new & (key <= t) & (key < n_real), sc, -jnp.inf))
    values.append(pick(vn_ref[...]).astype(BF16))
    m = logits[0].max(axis=1, keepdims=True)
    for x in logits[1:]:
        m = jnp.maximum(m, x.max(axis=1, keepdims=True))
    l = jnp.zeros((SUBLANES, 1), F32)
    acc = jnp.zeros((SUBLANES, HEAD_DIM), F32)
    for x, v in zip(logits, values):
        pr = jnp.exp(x - m)
        l = l + jnp.sum(pr, axis=1, keepdims=True)
        acc = acc + _dot(pr.astype(BF16), v)
    o_ref[...] = acc / l


def nsa_sel_s_call(page_table, idx, p, cache_blocks, *, layer, bsz, n_tok, n_real, q_off):
    n_cached = q_off // SLC_BLOCK

    def blk_spec(r, col):
        def index_map(b, g, t, pt, ix):
            j = jnp.clip(ix[((b * NSA_KV_HEADS + g) * n_tok + t) * TOP_N + r], 0, n_cached - 1)
            return (pt[b, j // 2], layer, j % 2, 0, col)
        return pl.BlockSpec((None, None, None, SLC_BLOCK, LANES), index_map)

    grid_spec = pltpu.PrefetchScalarGridSpec(
        num_scalar_prefetch=2,
        grid=(bsz, NSA_KV_HEADS, n_tok),
        in_specs=[pl.BlockSpec((T_PAD, 2 * LANES), lambda b, g, t, pt, ix: (b, C_NQ // (2 * LANES) + g)),
                  pl.BlockSpec((T_PAD, LANES), lambda b, g, t, pt, ix: (b, C_NKV // LANES + 2)),
                  pl.BlockSpec((T_PAD, LANES), lambda b, g, t, pt, ix: (b, C_NKV // LANES + 3))]
                 + [blk_spec(r, 2) for r in range(TOP_N)] + [blk_spec(r, 3) for r in range(TOP_N)],
        out_specs=pl.BlockSpec((None, None, None, SUBLANES, HEAD_DIM), lambda b, g, t, pt, ix: (b, g, t, 0, 0)),
    )
    return pl.pallas_call(
        functools.partial(_nsa_sel_s_body, n_real=n_real, n_tok=n_tok, q_off=q_off),
        grid_spec=grid_spec,
        out_shape=jax.ShapeDtypeStruct((bsz, NSA_KV_HEADS, n_tok, SUBLANES, HEAD_DIM), F32),
        compiler_params=_cparams(("parallel", "arbitrary", "arbitrary")),
        name="nsa_sel_s",
    )(page_table, idx, p, p, p, *([cache_blocks] * (2 * TOP_N)))


def _nsa_win_s_body(q_ref, gl_ref, oc_ref, os_ref, win_ref, new_ref, o_ref, *, n_real, win_len):
    hg_n = NSA_GROUP
    rows = hg_n * T_PAD
    scale = HEAD_DIM ** -0.5
    gate = _sigmoid(gl_ref[...])
    t_row = lax.broadcasted_iota(jnp.int32, (rows, 1), 0) % T_PAD
    n_keys = win_len + PAGE
    key = lax.broadcasted_iota(jnp.int32, (rows, n_keys), 1)
    new_i = key - win_len
    mask = (((key < win_len) & (key + WINDOW > t_row + win_len))
            | ((new_i >= 0) & (new_i <= t_row) & (new_i < n_real)))
    zpad = jnp.zeros((PAGE - T_PAD, HEAD_DIM), F32)
    for g in range(NSA_KV_HEADS):
        qs = _stack_heads(q_ref, g).astype(BF16)
        kk = jnp.concatenate([win_ref[:, g * HEAD_DIM:(g + 1) * HEAD_DIM],
                              new_ref[:, g * HEAD_DIM:(g + 1) * HEAD_DIM], zpad], axis=0).astype(BF16)
        vv = jnp.concatenate([win_ref[:, LANES + g * HEAD_DIM:LANES + (g + 1) * HEAD_DIM],
                              new_ref[:, LANES + g * HEAD_DIM:LANES + (g + 1) * HEAD_DIM], zpad], axis=0).astype(BF16)
        sc = jnp.where(mask, _dot_nt(qs, kk) * scale, -jnp.inf)
        m = jnp.max(sc, axis=1, keepdims=True)
        pr = jnp.exp(sc - m)
        o_win = _dot(pr.astype(BF16), vv) / jnp.sum(pr, axis=1, keepdims=True)
        for hg in range(hg_n):
            h = g * hg_n + hg
            cs = slice(h * HEAD_DIM, (h + 1) * HEAD_DIM)
            o_ref[:, cs] = (gate[:, 3 * h:3 * h + 1] * oc_ref[:, cs] + gate[:, 3 * h + 1:3 * h + 2] * os_ref[:, cs]
                            + gate[:, 3 * h + 2:3 * h + 3] * o_win[hg * T_PAD:(hg + 1) * T_PAD])


def nsa_win_s_call(p, o_cmp, o_slc, cache_win, *, layer, bsz, n_real):
    win_len = cache_win.shape[2]
    assert win_len == WINDOW
    row = lambda width, off: pl.BlockSpec((T_PAD, width), lambda b: (b, off // width))
    return pl.pallas_call(
        functools.partial(_nsa_win_s_body, n_real=n_real, win_len=win_len),
        grid=(bsz,),
        in_specs=[row(GW, C_NQ), row(LANES, C_NG), row(GW, 0), row(GW, 0),
                  pl.BlockSpec((None, None, win_len, 2 * LANES), lambda b: (b, layer, 0, 0)),
                  row(2 * LANES, C_NKV + 4 * LANES)],
        out_specs=pl.BlockSpec((T_PAD, GW), lambda b: (b, 0)),
        out_shape=jax.ShapeDtypeStruct((bsz * T_PAD, GW), F32),
        compiler_params=_cparams(("parallel",)),
        name="nsa_win_s",
    )(p, p, o_cmp, o_slc, cache_win, p)


def _lane_pad(v):
    return jnp.pad(v, (0, LANES - v.shape[0]))[None, :]


def _layer_weights(w, l):
    ff = lambda m, axis: jnp.pad(m, [(0, FF_PAD - D_FF) if a == axis else (0, 0) for a in range(2)]).astype(BF16)
    win = w['w_in'][l]
    o_fox = GW + SSD_CONV_DIM + SSD_HEADS
    o_nsa = o_fox + 3 * GW + FOX_HEADS
    o_s5 = o_nsa + GW + 6 * NSA_KV_HEADS * HEAD_DIM + 3 * NSA_HEADS
    cols = lambda a, b: win[:, a:b]
    zpad = lambda n: jnp.zeros((D_MODEL, n), F32)
    w_in = jnp.concatenate([
        cols(GW, GW + SSD_CONV_DIM),
        cols(GW + SSD_CONV_DIM, o_fox), zpad(LANES - SSD_HEADS),
        cols(o_fox + 3 * GW, o_nsa), zpad(LANES - FOX_HEADS),
        cols(0, GW),
        cols(o_s5, o_s5 + GW),
        cols(o_fox, o_fox + 3 * GW),
        cols(o_nsa, o_nsa + GW),
        cols(o_nsa + GW, o_nsa + GW + 6 * LANES),
        cols(o_nsa + GW + 6 * LANES, o_s5), zpad(LANES - 3 * NSA_HEADS),
        zpad(P_W - C_NG - LANES),
    ], axis=1).astype(BF16)
    assert w_in.shape == (D_MODEL, P_W)
    pe4, w4 = nsa_cmp_params(w['nsa_cmp_pe'][l], w['nsa_cmp_w'][l])
    s5 = s5_params(w['s5_lambda_re'][l], w['s5_lambda_im'][l], w['s5_log_dt'][l],
                   w['s5_b_re'][l], w['s5_b_im'][l], w['s5_c_re'][l], w['s5_c_im'][l])
    return dict(
        ffn1=(w['ffn1_norm'][l][None, :], ff(w['ffn1_w1'][l], 1), ff(w['ffn1_w3'][l], 1), ff(w['ffn1_w2'][l], 0)),
        ffn2=(w['ffn2_norm'][l][None, :], ff(w['ffn2_w1'][l], 1), ff(w['ffn2_w3'][l], 1), ff(w['ffn2_w2'][l], 0)),
        mix_norm=w['mix_norm'][l][None, :], w_in=w_in,
        ssd=(w['ssd_conv_w'][l], w['ssd_conv_b'][l][None, :], _lane_pad(w['ssd_dt_bias'][l]),
             _lane_pad(w['ssd_a_log'][l]), _lane_pad(w['ssd_d'][l])),
        fox_bias=_lane_pad(w['fox_f_bias'][l]),
        pe4=pe4, w4=w4, s5=s5, s5_d=w['s5_d'][l][None, :], s5_glu=w['s5_w_glu'][l].astype(BF16),
        gains=jnp.stack([w['ssd_norm'][l], w['fox_out_norm'][l], w['nsa_out_norm'][l], w['s5_out_norm'][l]]),
        w_out=w['w_out'][l].astype(BF16),
    )


def _time_major(x, bsz, t_len):
    return jnp.swapaxes(x.reshape(bsz, t_len, -1), 0, 1).reshape(t_len * bsz, -1)


def _batch_major(x, bsz, t_len):
    return jnp.swapaxes(x.reshape(t_len, bsz, -1), 0, 1).reshape(bsz * t_len, -1)


def _prompt_layer(x, lw, final_gain, *, bsz, t_len, last):
    tm = 512
    x = ffn_call(x, *lw['ffn1'], final_gain, tm=tm, final_norm=False)
    p = inproj_call(x, lw['mix_norm'], lw['w_in'], tm=tm)
    conv0 = jnp.zeros((bsz, CONV_W - 1, SSD_CONV_DIM), F32)
    ssm0 = jnp.zeros((bsz, SSD_HEADS, HEAD_DIM, SSD_STATE), F32)
    y_ssd, ssm, conv = ssd_call(p, conv0, ssm0, *lw['ssd'], bsz=bsz, t_len=t_len, n_real=SSD_CHUNK)
    lf, cum, cumt = fox_prep_call(p, lw['fox_bias'], bsz=bsz, t_len=t_len)
    y_fox = fox_attn_call(p, cum, cumt, bsz=bsz, t_len=t_len)
    cmp = nsa_cmp_p_call(p, lw['pe4'], lw['w4'], bsz=bsz, t_len=t_len)
    y_nsa = nsa_attn_p_call(p, cmp, bsz=bsz, t_len=t_len)
    bre, bim, are, aim, cre, cim = lw['s5']
    x0 = jnp.zeros((SUBLANES, S5_N), F32)
    u = _time_major(p[:, C_U:C_U + GW], bsz, t_len)
    y_s5, xr, xi = s5_call(u, bre, bim, are, aim, x0, x0, cre, cim, lw['s5_d'], lw['s5_glu'],
                           nb=bsz, t_len=t_len, steps=256, t_last=t_len - 1)
    y_s5 = _batch_major(y_s5, bsz, t_len)
    x = outproj_call(x, y_ssd, y_fox, y_nsa, y_s5, lw['gains'], lw['w_out'], tm=tm)
    x = ffn_call(x, *lw['ffn2'], final_gain, tm=tm, final_norm=last)
    p3 = p.reshape(bsz, t_len, P_W)
    keep = min(WINDOW, t_len)
    off = ((t_len - 1) % (SUBLANES // bsz)) * bsz
    states = (
        p3[:, :, C_FK:C_FK + 2 * GW].reshape(bsz, t_len, 2, FOX_HEADS, HEAD_DIM),
        lf.reshape(bsz, t_len, LANES)[:, :, :FOX_HEADS],
        p3[:, :, C_NKV:C_NKV + 4 * LANES].reshape(bsz, t_len, 4, NSA_KV_HEADS, HEAD_DIM),
        p3[:, t_len - keep:, C_NKV + 4 * LANES:C_NKV + 6 * LANES].reshape(bsz, keep, 2, NSA_KV_HEADS, HEAD_DIM),
        ssm, conv,
        xr[off:off + bsz].reshape(bsz, S5_GROUPS, S5_STATE),
        xi[off:off + bsz].reshape(bsz, S5_GROUPS, S5_STATE),
    )
    return x, states


def _sample_layer(x, lw, final_gain, caches, page_table, *, layer, bsz, n_real, q_off, last):
    cache_fox_kv, cache_fox_lf, cache_nsa_kv, cache_nsa_win, st_ssd, st_conv, st_re, st_im = caches
    tm = bsz * T_PAD
    n_pool = cache_fox_kv.shape[0]
    n_pages = page_table.shape[1]
    assert q_off == n_pages * PAGE and q_off % SLC_BLOCK == 0 and bsz == SUBLANES
    x = ffn_call(x, *lw['ffn1'], final_gain, tm=tm, final_norm=False)
    p = inproj_call(x, lw['mix_norm'], lw['w_in'], tm=tm)
    p3 = p.reshape(bsz, T_PAD, P_W)

    p_ssd = jnp.pad(p3[:, :, :C_U], ((0, 0), (0, SSD_CHUNK - T_PAD), (0, 0))).reshape(bsz * SSD_CHUNK, C_U)
    y_ssd, ssm, conv = ssd_call(p_ssd, st_conv[:, layer], st_ssd[:, layer], *lw['ssd'],
                                bsz=bsz, t_len=SSD_CHUNK, n_real=n_real)
    y_ssd = y_ssd.reshape(bsz, SSD_CHUNK, GW)[:, :T_PAD].reshape(tm, GW)

    r_past = fox_prep_s_call(page_table, cache_fox_lf, layer=layer, bsz=bsz, n_pages=n_pages)
    y_fox, lf = fox_attn_s_call(page_table, p, lw['fox_bias'], r_past,
                                cache_fox_kv.reshape(n_pool, DEPTH, PAGE, 2 * GW),
                                layer=layer, bsz=bsz, n_pages=n_pages, n_real=n_real)

    cmp = nsa_cmp_s_call(page_table, cache_nsa_kv.reshape(n_pool, DEPTH, PAGE, 4 * LANES), lw['pe4'], lw['w4'],
                         layer=layer, bsz=bsz, n_pages=n_pages)
    o_cmp, idx = nsa_topk_s_call(p, cmp, bsz=bsz, q_off=q_off)
    idx = jnp.swapaxes(idx[:, :, :, :n_real], 2, 3).reshape(-1)
    o_slc = nsa_sel_s_call(page_table, idx, p, cache_nsa_kv.reshape(n_pool, DEPTH, 2, SLC_BLOCK, 4 * LANES),
                           layer=layer, bsz=bsz, n_tok=n_real, n_real=n_real, q_off=q_off)
    o_slc = jnp.transpose(o_slc[:, :, :, :NSA_GROUP], (0, 2, 1, 3, 4)).reshape(bsz, n_real, GW)
    o_slc = jnp.pad(o_slc, ((0, 0), (0, T_PAD - n_real), (0, 0))).reshape(tm, GW)
    y_nsa = nsa_win_s_call(p, o_cmp, o_slc, cache_nsa_win.reshape(bsz, DEPTH, WINDOW, 2 * LANES),
                           layer=layer, bsz=bsz, n_real=n_real)

    bre, bim, are, aim, cre, cim = lw['s5']
    u = _time_major(p[:, C_U:C_U + GW], bsz, T_PAD)
    y_s5, xr, xi = s5_call(u, bre, bim, are, aim, st_re[:, layer].reshape(bsz, S5_N), st_im[:, layer].reshape(bsz, S5_N),
                           cre, cim, lw['s5_d'], lw['s5_glu'], nb=bsz, t_len=T_PAD, steps=T_PAD, t_last=n_real - 1)
    y_s5 = _batch_major(y_s5, bsz, T_PAD)

    x = outproj_call(x, y_ssd, y_fox, y_nsa, y_s5, lw['gains'], lw['w_out'], tm=tm)
    x = ffn_call(x, *lw['ffn2'], final_gain, tm=tm, final_norm=last)
    new = p3[:, :n_real]
    win_rows = new[:, :, C_NKV + 4 * LANES:C_NKV + 6 * LANES].reshape(bsz, n_real, 2, NSA_KV_HEADS, HEAD_DIM)
    states = (
        new[:, :, C_FK:C_FK + 2 * GW].reshape(bsz, n_real, 2, FOX_HEADS, HEAD_DIM),
        lf.reshape(bsz, T_PAD, LANES)[:, :n_real, :FOX_HEADS],
        new[:, :, C_NKV:C_NKV + 4 * LANES].reshape(bsz, n_real, 4, NSA_KV_HEADS, HEAD_DIM),
        jnp.concatenate([cache_nsa_win[:, layer, n_real:], win_rows], axis=1),
        ssm, conv,
        xr.reshape(bsz, S5_GROUPS, S5_STATE), xi.reshape(bsz, S5_GROUPS, S5_STATE),
    )
    return x, states


def kernel(x_prompt, x_sample, cache_fox_kv, cache_fox_logf, cache_nsa_kv, cache_nsa_win_kv, state_ssd,
           state_ssd_conv, state_s5_re, state_s5_im, page_table, ffn1_norm, ffn1_w1, ffn1_w3, ffn1_w2, mix_norm,
           w_in, ssd_conv_w, ssd_conv_b, ssd_dt_bias, ssd_a_log, ssd_d, ssd_norm, fox_f_bias, fox_out_norm,
           nsa_cmp_pe, nsa_cmp_w, nsa_out_norm, s5_lambda_re, s5_lambda_im, s5_log_dt, s5_b_re, s5_b_im,
           s5_c_re, s5_c_im, s5_d, s5_w_glu, s5_out_norm, w_out, ffn2_norm, ffn2_w1, ffn2_w3, ffn2_w2, final_norm):
    w = dict(ffn1_norm=ffn1_norm, ffn1_w1=ffn1_w1, ffn1_w3=ffn1_w3, ffn1_w2=ffn1_w2, mix_norm=mix_norm, w_in=w_in,
             ssd_conv_w=ssd_conv_w, ssd_conv_b=ssd_conv_b, ssd_dt_bias=ssd_dt_bias, ssd_a_log=ssd_a_log, ssd_d=ssd_d,
             ssd_norm=ssd_norm, fox_f_bias=fox_f_bias, fox_out_norm=fox_out_norm, nsa_cmp_pe=nsa_cmp_pe,
             nsa_cmp_w=nsa_cmp_w, nsa_out_norm=nsa_out_norm, s5_lambda_re=s5_lambda_re, s5_lambda_im=s5_lambda_im,
             s5_log_dt=s5_log_dt, s5_b_re=s5_b_re, s5_b_im=s5_b_im, s5_c_re=s5_c_re, s5_c_im=s5_c_im, s5_d=s5_d,
             s5_w_glu=s5_w_glu, s5_out_norm=s5_out_norm, w_out=w_out, ffn2_norm=ffn2_norm, ffn2_w1=ffn2_w1,
             ffn2_w3=ffn2_w3, ffn2_w2=ffn2_w2)
    bsz_p, t_len, _ = x_prompt.shape
    bsz_s, n_real, _ = x_sample.shape
    depth = w_in.shape[0]
    q_off = page_table.shape[1] * PAGE
    fg = final_norm[None, :]
    caches = (cache_fox_kv, cache_fox_logf, cache_nsa_kv, cache_nsa_win_kv, state_ssd, state_ssd_conv,
              state_s5_re, state_s5_im)
    xp = x_prompt.reshape(bsz_p * t_len, D_MODEL)
    xs = jnp.pad(x_sample, ((0, 0), (0, T_PAD - n_real), (0, 0))).reshape(bsz_s * T_PAD, D_MODEL)
    st_p, st_s = [], []
    for l in range(depth):
        lw = _layer_weights(w, l)
        last = l == depth - 1
        xp, sp = _prompt_layer(xp, lw, fg, bsz=bsz_p, t_len=t_len, last=last)
        xs, ss = _sample_layer(xs, lw, fg, caches, page_table, layer=l, bsz=bsz_s, n_real=n_real, q_off=q_off,
                               last=last)
        st_p.append(sp)
        st_s.append(ss)
    y_p = xp.reshape(bsz_p, t_len, D_MODEL)
    y_s = xs.reshape(bsz_s, T_PAD, D_MODEL)[:, :n_real]
    out = [y_p, y_s]
    for i in range(8):
        out.append(jnp.stack([s[i] for s in st_p], axis=1))
        out.append(jnp.stack([s[i] for s in st_s], axis=1))
    return tuple(out)
```

```python
import functools
import math

import jax
import jax.numpy as jnp
from jax import lax
from jax.experimental import pallas as pl
from jax.experimental.pallas import tpu as pltpu

F32 = jnp.float32
BF16 = jnp.bfloat16
HIGHEST = lax.Precision.HIGHEST

D_MODEL = 2048
DEPTH = 2
HEAD_DIM = 64
GW = D_MODEL // 4
D_FF = ((8 * D_MODEL // 3 + 127) // 128) * 128
EPS = 1e-6
TINY = 1e-30
SSD_HEADS = GW // HEAD_DIM
SSD_GROUPS = 2
SSD_STATE = 64
CONV_W = 4
SSD_CONV_DIM = GW + 2 * SSD_GROUPS * SSD_STATE
SSD_CHUNK = 128
FOX_HEADS = GW // HEAD_DIM
NSA_HEADS = GW // HEAD_DIM
NSA_KV_HEADS = 2
NSA_GROUP = NSA_HEADS // NSA_KV_HEADS
CMP_BLOCK = 32
SLC_BLOCK = 64
TOP_N = 16
WINDOW = 512
FORCE_SCORE = 1e4
S5_CH = 16
S5_GROUPS = GW // S5_CH
S5_STATE = 64
S5_N = S5_GROUPS * S5_STATE
PAGE = 128

LANES = 128
SUBLANES = 8
VMEM_LIMIT = 56 * 1024 * 1024

C_XBC = 0
C_DT = 768
C_FF = 896
C_Z = 1024
C_U = 1536
C_FQ = 2048
C_FK = 2560
C_FV = 3072
C_NQ = 3584
C_NKV = 4096
C_NG = 4864
P_W = 5120
FF_PAD = 5632
FF_TILE = 512


def _cparams(sem):
    return pltpu.CompilerParams(dimension_semantics=sem, vmem_limit_bytes=VMEM_LIMIT)


def _rms(x, g):
    ms = jnp.mean(x * x, axis=-1, keepdims=True)
    return x * lax.rsqrt(ms + EPS) * g


def _sigmoid(x):
    return 1.0 / (1.0 + jnp.exp(-x))


def _silu(x):
    return x * _sigmoid(x)


def _softplus(x):
    return jnp.maximum(x, 0.0) + jnp.log(1.0 + jnp.exp(-jnp.abs(x)))


def _log_sigmoid(x):
    return jnp.minimum(x, 0.0) - jnp.log(1.0 + jnp.exp(-jnp.abs(x)))


def _dot(a, b):
    return jnp.dot(a, b, preferred_element_type=F32)


def _dot_nt(a, b):
    return lax.dot_general(a, b, (((1,), (1,)), ((), ())), preferred_element_type=F32)


def _dot_tn(a, b):
    return lax.dot_general(a, b, (((0,), (0,)), ((), ())), preferred_element_type=F32)


def _ffn_body(x_ref, g_ref, w1_ref, w3_ref, w2_ref, fg_ref, o_ref, h_ref, acc_ref, *, final_norm):
    k = pl.program_id(1)

    @pl.when(k == 0)
    def _():
        h_ref[...] = _rms(x_ref[...], g_ref[...]).astype(BF16)
        acc_ref[...] = jnp.zeros_like(acc_ref)

    h = h_ref[...]
    a = _dot(h, w1_ref[...])
    b = _dot(h, w3_ref[...])
    act = (_silu(a) * b).astype(BF16)
    acc_ref[...] += _dot(act, w2_ref[...])

    @pl.when(k == pl.num_programs(1) - 1)
    def _():
        y = x_ref[...] + 0.5 * acc_ref[...]
        if final_norm:
            y = _rms(y, fg_ref[...])
        o_ref[...] = y


def ffn_call(x, g, w1, w3, w2, fg, *, tm, final_norm):
    m = x.shape[0]
    assert m % tm == 0 and w1.shape == (D_MODEL, FF_PAD)
    return pl.pallas_call(
        functools.partial(_ffn_body, final_norm=final_norm),
        grid=(m // tm, FF_PAD // FF_TILE),
        in_specs=[
            pl.BlockSpec((tm, D_MODEL), lambda i, k: (i, 0)),
            pl.BlockSpec((1, D_MODEL), lambda i, k: (0, 0)),
            pl.BlockSpec((D_MODEL, FF_TILE), lambda i, k: (0, k)),
            pl.BlockSpec((D_MODEL, FF_TILE), lambda i, k: (0, k)),
            pl.BlockSpec((FF_TILE, D_MODEL), lambda i, k: (k, 0)),
            pl.BlockSpec((1, D_MODEL), lambda i, k: (0, 0)),
        ],
        out_specs=pl.BlockSpec((tm, D_MODEL), lambda i, k: (i, 0)),
        out_shape=jax.ShapeDtypeStruct((m, D_MODEL), F32),
        scratch_shapes=[pltpu.VMEM((tm, D_MODEL), BF16), pltpu.VMEM((tm, D_MODEL), F32)],
        compiler_params=_cparams(("parallel", "arbitrary")),
        name="ffn",
    )(x, g, w1, w3, w2, fg)


def _inproj_body(x_ref, g_ref, w_ref, o_ref, h_ref):
    @pl.when(pl.program_id(1) == 0)
    def _():
        h_ref[...] = _rms(x_ref[...], g_ref[...]).astype(BF16)

    o_ref[...] = _dot(h_ref[...], w_ref[...])


def inproj_call(x, g, w, *, tm, tn=1024):
    m = x.shape[0]
    assert m % tm == 0 and P_W % tn == 0
    return pl.pallas_call(
        _inproj_body,
        grid=(m // tm, P_W // tn),
        in_specs=[
            pl.BlockSpec((tm, D_MODEL), lambda i, j: (i, 0)),
            pl.BlockSpec((1, D_MODEL), lambda i, j: (0, 0)),
            pl.BlockSpec((D_MODEL, tn), lambda i, j: (0, j)),
        ],
        out_specs=pl.BlockSpec((tm, tn), lambda i, j: (i, j)),
        out_shape=jax.ShapeDtypeStruct((m, P_W), F32),
        scratch_shapes=[pltpu.VMEM((tm, D_MODEL), BF16)],
        compiler_params=_cparams(("parallel", "arbitrary")),
        name="inproj",
    )(x, g, w)


def _outproj_body(x_ref, a_ref, b_ref, c_ref, d_ref, gn_ref, w_ref, o_ref, h_ref):
    j = pl.program_id(1)
    tn = o_ref.shape[1]

    @pl.when(j == 0)
    def _():
        for i, r in enumerate((a_ref, b_ref, c_ref, d_ref)):
            h_ref[:, i * GW:(i + 1) * GW] = _rms(r[...], gn_ref[i:i + 1, :]).astype(BF16)

    col = pl.multiple_of(j * tn, tn)
    o_ref[...] = x_ref[:, pl.ds(col, tn)] + _dot(h_ref[...], w_ref[...])


def outproj_call(x, ya, yb, yc, yd, gains, w, *, tm, tn=1024):
    m = x.shape[0]
    assert m % tm == 0
    yspec = pl.BlockSpec((tm, GW), lambda i, j: (i, 0))
    return pl.pallas_call(
        _outproj_body,
        grid=(m // tm, D_MODEL // tn),
        in_specs=[
            pl.BlockSpec((tm, D_MODEL), lambda i, j: (i, 0)),
            yspec, yspec, yspec, yspec,
            pl.BlockSpec((4, GW), lambda i, j: (0, 0)),
            pl.BlockSpec((D_MODEL, tn), lambda i, j: (0, j)),
        ],
        out_specs=pl.BlockSpec((tm, tn), lambda i, j: (i, j)),
        out_shape=jax.ShapeDtypeStruct((m, D_MODEL), F32),
        scratch_shapes=[pltpu.VMEM((tm, D_MODEL), BF16)],
        compiler_params=_cparams(("parallel", "arbitrary")),
        name="outproj",
    )(x, ya, yb, yc, yd, gains, w)


def _ssd_body(xbc_ref, dt_ref, z_ref, conv0_ref, ssm0_ref, cw_ref, cb_ref, dtb_ref, alog_ref, dd_ref,
              y_ref, ssm_ref, conv_ref, xp_ref, act_ref, st_ref, *, n_real):
    c = pl.program_id(1)
    nc = pl.num_programs(1)
    q = SSD_CHUNK
    halo = SUBLANES

    @pl.when(c == 0)
    def _():
        xp_ref[halo - 3:halo, :] = conv0_ref[...]
        st_ref[...] = ssm0_ref[...]

    xr = xbc_ref[...]
    xp_ref[halo:halo + q, :] = xr
    conv = (cb_ref[...] + cw_ref[3:4, :] * xr
            + cw_ref[2:3, :] * xp_ref[halo - 1:halo - 1 + q, :]
            + cw_ref[1:2, :] * xp_ref[halo - 2:halo - 2 + q, :]
            + cw_ref[0:1, :] * xp_ref[halo - 3:halo - 3 + q, :])
    act_ref[...] = _silu(conv)

    row = lax.broadcasted_iota(jnp.int32, (q, LANES), 0)
    dt = jnp.where(row < n_real, _softplus(dt_ref[...] + dtb_ref[...]), 0.0)
    a = -jnp.exp(alog_ref[...])
    ti = lax.broadcasted_iota(jnp.int32, (q, q), 0)
    si = lax.broadcasted_iota(jnp.int32, (q, q), 1)
    causal = si <= ti
    acs = jnp.dot(causal.astype(F32), dt * a, preferred_element_type=F32, precision=HIGHEST)
    acs_t = acs.T
    e_acs = jnp.exp(acs)
    acs_last = acs[q - 1:q, :]
    w_end = jnp.exp(acs_last - acs) * dt
    e_last = jnp.exp(acs_last)

    for g in range(SSD_GROUPS):
        bm = act_ref[:, GW + g * SSD_STATE:GW + (g + 1) * SSD_STATE]
        cm = act_ref[:, GW + (SSD_GROUPS + g) * SSD_STATE:GW + (SSD_GROUPS + g + 1) * SSD_STATE]
        bm16 = bm.astype(BF16)
        cm16 = cm.astype(BF16)
        cb = _dot_nt(cm16, bm16)
        for hh in range(SSD_HEADS // SSD_GROUPS):
            h = g * (SSD_HEADS // SSD_GROUPS) + hh
            xs = act_ref[:, h * HEAD_DIM:(h + 1) * HEAD_DIM]
            seg = acs[:, h:h + 1] - acs_t[h:h + 1, :]
            decay = jnp.exp(jnp.where(causal, seg, -jnp.inf))
            y = _dot((cb * decay).astype(BF16), (xs * dt[:, h:h + 1]).astype(BF16))
            s_in = st_ref[h]
            y = y + _dot_nt(cm16, s_in.astype(BF16)) * e_acs[:, h:h + 1]
            y = y + dd_ref[:, h:h + 1] * xs
            cs = _dot_tn((xs * w_end[:, h:h + 1]).astype(BF16), bm16)
            st_ref[h] = e_last[:, h:h + 1] * s_in + cs
            zs = z_ref[:, h * HEAD_DIM:(h + 1) * HEAD_DIM]
            y_ref[:, h * HEAD_DIM:(h + 1) * HEAD_DIM] = y * _silu(zs)

    last_real = min(n_real, q)
    conv_ref[...] = xp_ref[halo + last_real - 3:halo + last_real, :]
    xp_ref[halo - 3:halo, :] = xp_ref[halo + q - 3:halo + q, :]

    @pl.when(c == nc - 1)
    def _():
        ssm_ref[...] = st_ref[...]


def ssd_call(p, conv0, ssm0, cw, cb, dtb, alog, dd, *, bsz, t_len, n_real):
    q = SSD_CHUNK
    nc = t_len // q
    assert t_len % q == 0 and (nc == 1 or n_real == q)
    row = lambda b, c: b * nc + c
    vec = lambda shape: pl.BlockSpec(shape, lambda b, c: (0, 0))
    return pl.pallas_call(
        functools.partial(_ssd_body, n_real=n_real),
        grid=(bsz, nc),
        in_specs=[
            pl.BlockSpec((q, SSD_CONV_DIM), lambda b, c: (row(b, c), C_XBC // SSD_CONV_DIM)),
            pl.BlockSpec((q, LANES), lambda b, c: (row(b, c), C_DT // LANES)),
            pl.BlockSpec((q, GW), lambda b, c: (row(b, c), C_Z // GW)),
            pl.BlockSpec((None, CONV_W - 1, SSD_CONV_DIM), lambda b, c: (b, 0, 0)),
            pl.BlockSpec((None, SSD_HEADS, HEAD_DIM, SSD_STATE), lambda b, c: (b, 0, 0, 0)),
            vec((CONV_W, SSD_CONV_DIM)), vec((1, SSD_CONV_DIM)), vec((1, LANES)), vec((1, LANES)), vec((1, LANES)),
        ],
        out_specs=[
            pl.BlockSpec((q, GW), lambda b, c: (row(b, c), 0)),
            pl.BlockSpec((None, SSD_HEADS, HEAD_DIM, SSD_STATE), lambda b, c: (b, 0, 0, 0)),
            pl.BlockSpec((None, CONV_W - 1, SSD_CONV_DIM), lambda b, c: (b, 0, 0)),
        ],
        out_shape=[
            jax.ShapeDtypeStruct((bsz * t_len, GW), F32),
            jax.ShapeDtypeStruct((bsz, SSD_HEADS, HEAD_DIM, SSD_STATE), F32),
            jax.ShapeDtypeStruct((bsz, CONV_W - 1, SSD_CONV_DIM), F32),
        ],
        scratch_shapes=[
            pltpu.VMEM((SUBLANES + q, SSD_CONV_DIM), F32),
            pltpu.VMEM((q, SSD_CONV_DIM), F32),
            pltpu.VMEM((SSD_HEADS, HEAD_DIM, SSD_STATE), F32),
        ],
        compiler_params=_cparams(("parallel", "arbitrary")),
        name="ssd",
    )(p, p, p, conv0, ssm0, cw, cb, dtb, alog, dd)


def _s5_body(u_ref, bre_ref, bim_ref, are_ref, aim_ref, x0r_ref, x0i_ref, cre_ref, cim_ref, d_ref, wg_ref,
             o_ref, xr_out, xi_out, xr_ref, xi_ref, sr_ref, si_ref, *, nb, tiles, t_last):
    c = pl.program_id(0)
    per = SUBLANES // nb

    @pl.when(c == 0)
    def _():
        sr_ref[...] = x0r_ref[...]
        si_ref[...] = x0i_ref[...]

    u = u_ref[...]
    u16 = u.astype(BF16)
    xr_ref[...] = _dot(u16, bre_ref[...])
    xi_ref[...] = _dot(u16, bim_ref[...])
    ar = are_ref[...]
    ai = aim_ref[...]
    first = lax.broadcasted_iota(jnp.int32, (SUBLANES, S5_N), 0) < nb

    def step(j, carry):
        sr, si = carry
        rows = pl.ds(pl.multiple_of(j * SUBLANES, SUBLANES), SUBLANES)
        br = xr_ref[rows, :]
        bi = xi_ref[rows, :]
        vr = ar * sr - ai * si + br
        vi = ar * si + ai * sr + bi
        if per == 2:
            pr = pltpu.roll(vr, nb, 0)
            pi = pltpu.roll(vi, nb, 0)
            wr = ar * pr - ai * pi + br
            wi = ar * pi + ai * pr + bi
            outr = jnp.where(first, vr, wr)
            outi = jnp.where(first, vi, wi)
            nxt = (pltpu.roll(wr, nb, 0), pltpu.roll(wi, nb, 0))
        else:
            outr, outi, nxt = vr, vi, (vr, vi)
        xr_ref[rows, :] = outr
        xi_ref[rows, :] = outi

        @pl.when(c * tiles + j == t_last // per)
        def _():
            xr_out[...] = outr
            xi_out[...] = outi

        return nxt

    sr, si = lax.fori_loop(0, tiles, step, (sr_ref[...], si_ref[...]))
    sr_ref[...] = sr
    si_ref[...] = si

    y = _dot(xr_ref[...].astype(BF16), cre_ref[...]) - _dot(xi_ref[...].astype(BF16), cim_ref[...])
    y = y + d_ref[...] * u
    g = _dot(jax.nn.gelu(y).astype(BF16), wg_ref[...])
    o_ref[...] = g[:, :GW] * _sigmoid(g[:, GW:])


def s5_call(u, bre, bim, are, aim, x0r, x0i, cre, cim, d, wg, *, nb, t_len, steps, t_last):
    assert t_len % steps == 0 and nb in (4, 8)
    rows = steps * nb
    assert rows % SUBLANES == 0
    const = lambda shape: pl.BlockSpec(shape, lambda c: (0, 0))
    return pl.pallas_call(
        functools.partial(_s5_body, nb=nb, tiles=rows // SUBLANES, t_last=t_last),
        grid=(t_len // steps,),
        in_specs=[
            pl.BlockSpec((rows, GW), lambda c: (c, 0)),
            const((GW, S5_N)), const((GW, S5_N)), const((1, S5_N)), const((1, S5_N)),
            const((SUBLANES, S5_N)), const((SUBLANES, S5_N)),
            const((S5_N, GW)), const((S5_N, GW)), const((1, GW)), const((GW, 2 * GW)),
        ],
        out_specs=[pl.BlockSpec((rows, GW), lambda c: (c, 0)), const((SUBLANES, S5_N)), const((SUBLANES, S5_N))],
        out_shape=[jax.ShapeDtypeStruct((t_len * nb, GW), F32),
                   jax.ShapeDtypeStruct((SUBLANES, S5_N), F32), jax.ShapeDtypeStruct((SUBLANES, S5_N), F32)],
        scratch_shapes=[pltpu.VMEM((rows, S5_N), F32), pltpu.VMEM((rows, S5_N), F32),
                        pltpu.VMEM((SUBLANES, S5_N), F32), pltpu.VMEM((SUBLANES, S5_N), F32)],
        compiler_params=_cparams(("arbitrary",)),
        name="s5",
    )(u, bre, bim, are, aim, x0r, x0i, cre, cim, d, wg)


def s5_params(lam_re, lam_im, log_dt, b_re, b_im, c_re, c_im):
    dt = jnp.exp(log_dt)[:, None]
    mag = jnp.exp(lam_re * dt)
    ab_re = mag * jnp.cos(lam_im * dt)
    ab_im = mag * jnp.sin(lam_im * dt)
    den = lam_re * lam_re + lam_im * lam_im
    zr = ((ab_re - 1.0) * lam_re + ab_im * lam_im) / den
    zi = (ab_im * lam_re - (ab_re - 1.0) * lam_im) / den
    bb_re = zr[..., None] * b_re - zi[..., None] * b_im
    bb_im = zr[..., None] * b_im + zi[..., None] * b_re
    eye = jnp.eye(S5_GROUPS, dtype=F32)

    def in_mat(bb):
        return jnp.einsum('gnc,gh->gchn', bb, eye).reshape(GW, S5_N).astype(BF16)

    def out_mat(cc):
        return jnp.einsum('gcn,gh->gnhc', cc, eye).reshape(S5_N, GW).astype(BF16)

    return (in_mat(bb_re), in_mat(bb_im), ab_re.reshape(1, S5_N), ab_im.reshape(1, S5_N),
            out_mat(c_re), out_mat(c_im))


LOG2E = 1.4426950408889634
NEG = -1e30


def _fox_prep_body(f_ref, b_ref, lf_ref, cumt_ref, carry_ref):
    c = pl.program_id(1)

    @pl.when(c == 0)
    def _():
        carry_ref[...] = jnp.zeros_like(carry_ref)

    lf = _log_sigmoid(f_ref[...] + b_ref[...])
    lf_ref[...] = lf
    tc = lf.shape[0]
    ti = lax.broadcasted_iota(jnp.int32, (tc, tc), 0)
    si = lax.broadcasted_iota(jnp.int32, (tc, tc), 1)
    cum = jnp.dot((si <= ti).astype(F32), lf, preferred_element_type=F32, precision=HIGHEST) + carry_ref[...]
    cumt_ref[...] = cum.T[:FOX_HEADS, :]
    carry_ref[...] = cum[tc - 1:tc, :]


def fox_prep_call(p, bias, *, bsz, t_len, tc=256):
    nc = t_len // tc
    assert t_len % tc == 0
    return pl.pallas_call(
        _fox_prep_body,
        grid=(bsz, nc),
        in_specs=[pl.BlockSpec((tc, LANES), lambda b, c: (b * nc + c, C_FF // LANES)),
                  pl.BlockSpec((1, LANES), lambda b, c: (0, 0))],
        out_specs=[pl.BlockSpec((tc, LANES), lambda b, c: (b * nc + c, 0)),
                   pl.BlockSpec((None, FOX_HEADS, tc), lambda b, c: (b, 0, c))],
        out_shape=[jax.ShapeDtypeStruct((bsz * t_len, LANES), F32),
                   jax.ShapeDtypeStruct((bsz, FOX_HEADS, t_len), F32)],
        scratch_shapes=[pltpu.VMEM((1, LANES), F32)],
        compiler_params=_cparams(("parallel", "arbitrary")),
        name="fox_prep",
    )(p, bias)


def _fox_attn_body(q_ref, k_ref, v_ref, cumt_ref, o_ref, *, tq, tk):
    hp = pl.program_id(1)
    qi = pl.program_id(2)
    q0 = qi * tq
    lane = lax.broadcasted_iota(jnp.int32, (tq, LANES), 1)
    q = q_ref[...] * (HEAD_DIM ** -0.5 * LOG2E)
    qs = jnp.concatenate([jnp.where(lane < HEAD_DIM, q, 0.0), jnp.where(lane >= HEAD_DIM, q, 0.0)],
                         axis=0).astype(BF16)
    t_pos = q0 + lax.broadcasted_iota(jnp.int32, (tq, tk), 0)
    s_off = lax.broadcasted_iota(jnp.int32, (tq, tk), 1)

    def chunk(kc, carry, masked):
        ks = pl.ds(pl.multiple_of(kc * tk, tk), tk)
        s = _dot_nt(qs, k_ref[ks, :].astype(BF16))
        vv = v_ref[ks, :].astype(BF16)
        out = []
        for hh in range(2):
            m, l, acc = carry[hh]
            sh = s[hh * tq:(hh + 1) * tq] - cumt_ref[pl.ds(2 * hp + hh, 1), ks] * LOG2E
            if masked:
                sh = jnp.where(kc * tk + s_off <= t_pos, sh, NEG)
            m_new = jnp.maximum(m, jnp.max(sh, axis=1, keepdims=True))
            alpha = jnp.exp2(m - m_new)
            pr = jnp.exp2(sh - m_new)
            l = alpha * l + jnp.sum(pr, axis=1, keepdims=True)
            acc = alpha * acc + _dot(pr.astype(BF16), vv)
            out.append((m_new, l, acc))
        return tuple(out)

    init = tuple((jnp.full((tq, 1), NEG, F32), jnp.zeros((tq, 1), F32), jnp.zeros((tq, LANES), F32))
                 for _ in range(2))
    n_full = q0 // tk
    carry = lax.fori_loop(0, n_full, lambda kc, c: chunk(kc, c, False), init)
    (_, la, acca), (_, lb, accb) = chunk(n_full, carry, True)
    o_ref[...] = jnp.where(lane < HEAD_DIM, acca / la, accb / lb)


def fox_attn_call(p, cumt, *, bsz, t_len, tq=128, tk=512):
    tk = min(tk, t_len)
    nq = t_len // tq
    assert t_len % tk == 0 and tk % tq == 0
    pairs = FOX_HEADS // 2
    return pl.pallas_call(
        functools.partial(_fox_attn_body, tq=tq, tk=tk),
        grid=(bsz, pairs, nq),
        in_specs=[
            pl.BlockSpec((tq, LANES), lambda b, hp, qi: (b * nq + qi, C_FQ // LANES + hp)),
            pl.BlockSpec((t_len, LANES), lambda b, hp, qi: (b, C_FK // LANES + hp)),
            pl.BlockSpec((t_len, LANES), lambda b, hp, qi: (b, C_FV // LANES + hp)),
            pl.BlockSpec((None, FOX_HEADS, t_len), lambda b, hp, qi: (b, 0, 0)),
        ],
        out_specs=pl.BlockSpec((tq, LANES), lambda b, hp, qi: (b * nq + qi, hp)),
        out_shape=jax.ShapeDtypeStruct((bsz * t_len, GW), F32),
        compiler_params=_cparams(("parallel", "parallel", "arbitrary")),
        name="fox_attn",
    )(p, p, p, cumt)


def _compress_rows(xk_ref, xv_ref, pe_ref, w_ref, nbc):
    half = nbc // 2
    acc = jnp.zeros((nbc, 2 * LANES), F32)
    for r in range(CMP_BLOCK):
        ev = pl.ds(r, half, stride=2 * CMP_BLOCK)
        od = pl.ds(CMP_BLOCK + r, half, stride=2 * CMP_BLOCK)
        rows = jnp.concatenate([jnp.concatenate([xk_ref[ev, :], xv_ref[ev, :]], axis=1),
                                jnp.concatenate([xk_ref[od, :], xv_ref[od, :]], axis=1)], axis=0)
        acc = acc + _dot((rows + pe_ref[r:r + 1, :]).astype(BF16), w_ref[r])
    return acc


def _nsa_cmp_p_body(xk_ref, xv_ref, pe_ref, w_ref, o_ref, *, nbc, hp):
    half = nbc // 2
    acc = _compress_rows(xk_ref, xv_ref, pe_ref, w_ref, nbc)
    o_ref[...] = jnp.zeros_like(o_ref)
    o_ref[0:half, :] = acc[0:half]
    o_ref[hp:hp + half, :] = acc[half:nbc]


def nsa_cmp_p_call(p, pe4, w4, *, bsz, t_len):
    nbc = t_len // CMP_BLOCK
    hp = max(nbc // 2, HEAD_DIM)
    return pl.pallas_call(
        functools.partial(_nsa_cmp_p_body, nbc=nbc, hp=hp),
        grid=(bsz,),
        in_specs=[pl.BlockSpec((t_len, LANES), lambda b: (b, C_NKV // LANES)),
                  pl.BlockSpec((t_len, LANES), lambda b: (b, C_NKV // LANES + 1)),
                  pl.BlockSpec((CMP_BLOCK, 2 * LANES), lambda b: (0, 0)),
                  pl.BlockSpec((CMP_BLOCK, 2 * LANES, 2 * LANES), lambda b: (0, 0, 0))],
        out_specs=pl.BlockSpec((None, 2 * hp, 2 * LANES), lambda b: (b, 0, 0)),
        out_shape=jax.ShapeDtypeStruct((bsz, 2 * hp, 2 * LANES), F32),
        compiler_params=_cparams(("parallel",)),
        name="nsa_cmp_p",
    )(p, p, pe4, w4)


def _cmp_attend(qs, cmp_ref, g, t_pos, blk, slot_ok):
    scale = HEAD_DIM ** -0.5
    kc = cmp_ref[:, g * HEAD_DIM:(g + 1) * HEAD_DIM].astype(BF16)
    vc = cmp_ref[:, 2 * HEAD_DIM + g * HEAD_DIM:2 * HEAD_DIM + (g + 1) * HEAD_DIM].astype(BF16)
    s = _dot_nt(qs, kc) * scale
    valid = slot_ok & ((blk + 1) * CMP_BLOCK - 1 <= t_pos)
    s = jnp.where(valid, s, -jnp.inf)
    m = jnp.max(s, axis=1, keepdims=True)
    m = jnp.where(m > -jnp.inf, m, 0.0)
    e = jnp.where(valid, jnp.exp(s - m), 0.0)
    pc = e / jnp.maximum(jnp.sum(e, axis=1, keepdims=True), TINY)
    return _dot(pc.astype(BF16), vc), pc


def _select_score(imp, t_pos, blk):
    cur = t_pos // SLC_BLOCK
    forced = (blk == 0) | (blk == cur) | (blk == cur - 1)
    avail = blk * SLC_BLOCK <= t_pos
    return jnp.where(avail, jnp.where(forced, FORCE_SCORE, imp), -1.0)


def _nsa_attn_p_body(q_ref, cmp_ref, ks_ref, vs_ref, kw_ref, vw_ref, gl_ref, o_ref, *, tq, tk, nbc, hp):
    qi = pl.program_id(1)
    hg_n = NSA_GROUP
    rows = hg_n * tq
    half = nbc // 2
    nbs = (nbc + 1) // 2
    q0 = qi * tq
    scale = HEAD_DIM ** -0.5
    t_col = q0 + lax.broadcasted_iota(jnp.int32, (tq, 1), 0)
    t_lane = q0 + lax.broadcasted_iota(jnp.int32, (1, tq), 1)
    t_lane_stack = jnp.concatenate([t_lane] * hg_n, axis=1)
    gate = _sigmoid(gl_ref[...])
    crow = lax.broadcasted_iota(jnp.int32, (2 * hp, 1), 0)
    slot = jnp.where(crow < hp, crow, crow - hp)
    cmp_blk = 2 * slot + jnp.where(crow < hp, 0, 1)
    slc_blk = lax.broadcasted_iota(jnp.int32, (hp, 1), 0)

    def add_bias(s, bias):
        n = s.shape[1]
        return (s.reshape(hg_n, tq, n) + bias[None]).reshape(rows, n)

    for g in range(NSA_KV_HEADS):
        q_f32 = jnp.concatenate(
            [q_ref[:, (g * hg_n + hg) * HEAD_DIM:(g * hg_n + hg + 1) * HEAD_DIM] for hg in range(hg_n)], axis=0)
        qs = q_f32.astype(BF16)
        qs2 = (q_f32 * (scale * LOG2E)).astype(BF16)

        kc = cmp_ref[:, g * HEAD_DIM:(g + 1) * HEAD_DIM].astype(BF16)
        vc = cmp_ref[:, 2 * HEAD_DIM + g * HEAD_DIM:2 * HEAD_DIM + (g + 1) * HEAD_DIM].astype(BF16)
        st = _dot_nt(kc, qs) * scale
        valid = (slot < half) & ((cmp_blk + 1) * CMP_BLOCK - 1 <= t_lane_stack)
        st = jnp.where(valid, st, -jnp.inf)
        m = jnp.max(st, axis=0, keepdims=True)
        m = jnp.where(m > -jnp.inf, m, 0.0)
        e = jnp.where(valid, jnp.exp(st - m), 0.0)
        pt = e / jnp.maximum(jnp.sum(e, axis=0, keepdims=True), TINY)
        o_cmp = _dot_tn(pt.astype(BF16), vc)

        imp = pt[:, 0:tq]
        for hg in range(1, hg_n):
            imp = imp + pt[:, hg * tq:(hg + 1) * tq]
        imp = imp[0:hp] + imp[hp:2 * hp]
        score = _select_score(imp, t_lane, slc_blk)
        rank = jnp.zeros((hp, tq), jnp.int32)
        for i in range(nbs):
            row = score[i:i + 1, :]
            rank = rank + ((row > score) | ((row == score) & (i < slc_blk))).astype(jnp.int32)
        sel_t = ((rank < TOP_N) & (score >= 0.0)).astype(BF16)

        def slc_chunk(kc_i, carry, sel_t=sel_t, qs2=qs2, g=g):
            m, l, acc = carry
            ks = pl.ds(pl.multiple_of(kc_i * tk, tk), tk)
            jb = lax.broadcasted_iota(jnp.int32, (hp, tk), 0)
            sp = kc_i * tk + lax.broadcasted_iota(jnp.int32, (hp, tk), 1)
            hit = _dot_tn(sel_t, (jb == sp // SLC_BLOCK).astype(BF16)) > 0.5
            s_pos = kc_i * tk + lax.broadcasted_iota(jnp.int32, (tq, tk), 1)
            bias = jnp.where(hit & (s_pos <= t_col), 0.0, NEG)
            s = add_bias(_dot_nt(qs2, ks_ref[ks, g * HEAD_DIM:(g + 1) * HEAD_DIM].astype(BF16)), bias)
            m_new = jnp.maximum(m, jnp.max(s, axis=1, keepdims=True))
            alpha = jnp.exp2(m - m_new)
            pr = jnp.exp2(s - m_new)
            l = alpha * l + jnp.sum(pr, axis=1, keepdims=True)
            acc = alpha * acc + _dot(pr.astype(BF16), vs_ref[ks, g * HEAD_DIM:(g + 1) * HEAD_DIM].astype(BF16))
            return m_new, l, acc

        init = (jnp.full((rows, 1), NEG, F32), jnp.zeros((rows, 1), F32), jnp.zeros((rows, HEAD_DIM), F32))
        _, l, acc = lax.fori_loop(0, (q0 + tq + tk - 1) // tk, slc_chunk, init)
        o_slc = acc / l

        span = WINDOW + tq
        start = pl.multiple_of(jnp.maximum(q0 - WINDOW, 0), tq)
        ws = pl.ds(start, span)
        diff = t_col - (start + lax.broadcasted_iota(jnp.int32, (tq, span), 1))
        bias = jnp.where((diff >= 0) & (diff < WINDOW), 0.0, NEG)
        s = add_bias(_dot_nt(qs2, kw_ref[ws, g * HEAD_DIM:(g + 1) * HEAD_DIM].astype(BF16)), bias)
        pr = jnp.exp2(s - jnp.max(s, axis=1, keepdims=True))
        o_win = (_dot(pr.astype(BF16), vw_ref[ws, g * HEAD_DIM:(g + 1) * HEAD_DIM].astype(BF16))
                 / jnp.sum(pr, axis=1, keepdims=True))

        for hg in range(hg_n):
            h = g * hg_n + hg
            rs = slice(hg * tq, (hg + 1) * tq)
            o = (gate[:, 3 * h:3 * h + 1] * o_cmp[rs] + gate[:, 3 * h + 1:3 * h + 2] * o_slc[rs]
                 + gate[:, 3 * h + 2:3 * h + 3] * o_win[rs])
            o_ref[:, h * HEAD_DIM:(h + 1) * HEAD_DIM] = o


def nsa_attn_p_call(p, cmp, *, bsz, t_len, tq=128, tk=512):
    nq = t_len // tq
    nbc = t_len // CMP_BLOCK
    hp = cmp.shape[1] // 2
    assert t_len % tk == 0 and WINDOW % tq == 0 and t_len >= WINDOW + tq
    kv = lambda off: pl.BlockSpec((t_len, LANES), lambda b, qi: (b, (C_NKV + off) // LANES))
    return pl.pallas_call(
        functools.partial(_nsa_attn_p_body, tq=tq, tk=tk, nbc=nbc, hp=hp),
        grid=(bsz, nq),
        in_specs=[
            pl.BlockSpec((tq, GW), lambda b, qi: (b * nq + qi, C_NQ // GW)),
            pl.BlockSpec((None, 2 * hp, 2 * LANES), lambda b, qi: (b, 0, 0)),
            kv(2 * LANES), kv(3 * LANES), kv(4 * LANES), kv(5 * LANES),
            pl.BlockSpec((tq, LANES), lambda b, qi: (b * nq + qi, C_NG // LANES)),
        ],
        out_specs=pl.BlockSpec((tq, GW), lambda b, qi: (b * nq + qi, 0)),
        out_shape=jax.ShapeDtypeStruct((bsz * t_len, GW), F32),
        compiler_params=_cparams(("parallel", "arbitrary")),
        name="nsa_attn_p",
    )(p, cmp, p, p, p, p, p)


def nsa_cmp_params(pe, cw):
    pe4 = jnp.concatenate([pe[0], pe[0], pe[1], pe[1]], axis=1)
    wk = cw[0].reshape(CMP_BLOCK, HEAD_DIM, HEAD_DIM)
    wv = cw[1].reshape(CMP_BLOCK, HEAD_DIM, HEAD_DIM)
    z = jnp.zeros_like(wk)
    rows = [jnp.concatenate([m if i == j else z for j in range(4)], axis=2) for i, m in enumerate((wk, wk, wv, wv))]
    return pe4, jnp.concatenate(rows, axis=1).astype(BF16)


def nsa_cmp_params_t(pe, cw):
    per_page = PAGE // CMP_BLOCK
    eye = jnp.eye(per_page, dtype=F32)

    def mat(w):
        wd = jnp.transpose(w.reshape(CMP_BLOCK, HEAD_DIM, HEAD_DIM), (1, 0, 2))
        return jnp.einsum('dre,mn->dmrne', wd, eye).reshape(HEAD_DIM, PAGE, per_page * HEAD_DIM).astype(BF16)

    bias = lambda x: jnp.tile(x.T, (1, per_page))
    return bias(pe[0]), bias(pe[1]), mat(cw[0]), mat(cw[1])


T_PAD = SUBLANES
FOX_LF_PAGES = 16
FOX_KV_PAGES = 16
NSA_CMP_PAGES = 16


def _page_spec(block, layer, pages_per_step, i, tail, first_step=0):
    def index_map(b, s, pt):
        j = jnp.maximum(s - first_step, 0) * pages_per_step + i
        return (pt[b, j], layer) + tail
    return pl.BlockSpec(block, index_map)


def _fox_prep_s_body(pt_ref, *refs, n_pages):
    pg = FOX_LF_PAGES
    page_refs, o_ref, a_ref = refs[:pg], refs[pg], refs[pg + 1]
    s = pl.program_id(1)
    for i in range(pg):
        a_ref[pl.ds(pl.multiple_of((s * pg + i) * FOX_HEADS, FOX_HEADS), FOX_HEADS), :] = page_refs[i][...]

    @pl.when(s == pl.num_programs(1) - 1)
    def _():
        n = n_pages * FOX_HEADS
        a = a_ref[...]
        ji = lax.broadcasted_iota(jnp.int32, (PAGE, PAGE), 0)
        si = lax.broadcasted_iota(jnp.int32, (PAGE, PAGE), 1)
        within = jnp.dot(a, (ji > si).astype(F32), preferred_element_type=F32, precision=HIGHEST)
        tot = jnp.broadcast_to(jnp.sum(a, axis=1, keepdims=True), (n, LANES))
        ri = lax.broadcasted_iota(jnp.int32, (n, n), 0)
        ci = lax.broadcasted_iota(jnp.int32, (n, n), 1)
        later = ((ci > ri) & ((ci - ri) % FOX_HEADS == 0)).astype(F32)
        o_ref[...] = within + jnp.dot(later, tot, preferred_element_type=F32, precision=HIGHEST)


def fox_prep_s_call(page_table, cache_lf, *, layer, bsz, n_pages):
    pg = FOX_LF_PAGES
    assert n_pages % pg == 0
    n = n_pages * FOX_HEADS
    grid_spec = pltpu.PrefetchScalarGridSpec(
        num_scalar_prefetch=1,
        grid=(bsz, n_pages // pg),
        in_specs=[_page_spec((None, None, FOX_HEADS, PAGE), layer, pg, i, (0, 0)) for i in range(pg)],
        out_specs=pl.BlockSpec((None, n, LANES), lambda b, s, pt: (b, 0, 0)),
        scratch_shapes=[pltpu.VMEM((n, LANES), F32)],
    )
    return pl.pallas_call(
        functools.partial(_fox_prep_s_body, n_pages=n_pages),
        grid_spec=grid_spec,
        out_shape=jax.ShapeDtypeStruct((bsz, n, LANES), F32),
        compiler_params=_cparams(("parallel", "arbitrary")),
        name="fox_prep_s",
    )(page_table, *([cache_lf] * pg))


def _fox_attn_s_body(pt_ref, q_ref, k_ref, v_ref, f_ref, b_ref, rp_ref, *refs, n_real):
    pg = FOX_KV_PAGES
    page_refs = refs[:pg]
    o_ref, lf_ref, qbd_ref, rqp_ref, m_ref, l_ref, acc_ref = refs[pg:]
    s = pl.program_id(1)
    scale = HEAD_DIM ** -0.5
    nh = FOX_HEADS
    rows = nh * T_PAD
    row_h = lax.broadcasted_iota(jnp.int32, (rows, 1), 0) // T_PAD
    row_t = lax.broadcasted_iota(jnp.int32, (rows, 1), 0) % T_PAD

    def rep_heads(x8):
        return jnp.concatenate([jnp.broadcast_to(x8[h:h + 1, :], (T_PAD, x8.shape[1])) for h in range(nh)], axis=0)

    def attend(scores, values, m, l, acc):
        tile_max = scores[0]
        for sc in scores[1:]:
            tile_max = jnp.maximum(tile_max, sc)
        m_new = jnp.maximum(m, jnp.max(tile_max, axis=1, keepdims=True))
        alpha = jnp.exp(m - m_new)
        acc = alpha * acc
        psum = None
        for sc, pv in zip(scores, values):
            pr = jnp.exp(sc - m_new)
            psum = pr if psum is None else psum + pr
            acc = acc + pv(pr.astype(BF16))
        return m_new, alpha * l + jnp.sum(psum, axis=1, keepdims=True), acc

    @pl.when(s == 0)
    def _():
        lf = _log_sigmoid(f_ref[...] + b_ref[...])
        lf_ref[...] = lf
        tok = lax.broadcasted_iota(jnp.int32, (T_PAD, LANES), 0)
        lfm = jnp.where(tok < n_real, lf, 0.0)
        r_new = jnp.zeros((T_PAD, LANES), F32)
        for j in range(1, n_real):
            r_new = r_new + jnp.where(tok < j, lfm[j:j + 1, :], 0.0)
        tot = jnp.sum(lfm, axis=0, keepdims=True)
        lane = lax.broadcasted_iota(jnp.int32, (rows, LANES), 1)
        pick = lane == row_h
        r_q = jnp.sum(jnp.where(pick, jnp.concatenate([r_new] * nh, axis=0), 0.0), axis=1, keepdims=True)
        t_q = jnp.sum(jnp.where(pick, jnp.broadcast_to(tot, (rows, LANES)), 0.0), axis=1, keepdims=True)
        rqp_ref[...] = r_q - t_q
        col = lax.broadcasted_iota(jnp.int32, (rows, GW), 1)
        qbd = jnp.where(col // HEAD_DIM == row_h, jnp.concatenate([q_ref[...]] * nh, axis=0), 0.0)
        qbd_ref[...] = qbd.astype(BF16)
        zrow = jnp.zeros((PAGE - T_PAD, GW), F32)
        kk = jnp.concatenate([k_ref[...], zrow], axis=0)
        vv = jnp.concatenate([v_ref[...], zrow], axis=0)
        r_pad = jnp.concatenate([r_new, jnp.zeros((PAGE - T_PAD, LANES), F32)], axis=0)
        bias = rep_heads(r_pad.T[:nh, :]) - r_q
        key = lax.broadcasted_iota(jnp.int32, (rows, PAGE), 1)
        mask = (key <= row_t) & (key < n_real)
        sc = jnp.where(mask, _dot_nt(qbd.astype(BF16), kk.astype(BF16)) * scale + bias, -jnp.inf)
        m0 = jnp.full((rows, 1), -jnp.inf, F32)
        m, l, acc = attend([sc], [lambda pr: _dot(pr, vv.astype(BF16))], m0,
                           jnp.zeros((rows, 1), F32), jnp.zeros((rows, GW), F32))
        m_ref[...] = m
        l_ref[...] = l
        acc_ref[...] = acc

    @pl.when(s > 0)
    def _():
        rqp = rqp_ref[...]
        qbd = qbd_ref[...]
        scores = [_dot(qbd, page_refs[i][0:GW, :].astype(BF16)) * scale
                  + (rep_heads(rp_ref[i * nh:(i + 1) * nh, :]) - rqp) for i in range(pg)]
        values = [lambda pr, i=i: _dot_nt(pr, page_refs[i][GW:2 * GW, :].astype(BF16)) for i in range(pg)]
        m, l, acc = attend(scores, values, m_ref[...], l_ref[...], acc_ref[...])
        m_ref[...] = m
        l_ref[...] = l
        acc_ref[...] = acc

    @pl.when(s == pl.num_programs(1) - 1)
    def _():
        col = lax.broadcasted_iota(jnp.int32, (rows, GW), 1)
        o = jnp.where(col // HEAD_DIM == row_h, acc_ref[...] / l_ref[...], 0.0)
        o_ref[...] = jnp.sum(o.reshape(nh, T_PAD, GW), axis=0)


def fox_attn_s_call(page_table, p, bias, r_past, cache_kv, *, layer, bsz, n_pages, n_real):
    pg = FOX_KV_PAGES
    assert n_pages % pg == 0
    rows = FOX_HEADS * T_PAD
    new = lambda width, off: pl.BlockSpec((T_PAD, width), lambda b, s, pt: (b, off // width))
    grid_spec = pltpu.PrefetchScalarGridSpec(
        num_scalar_prefetch=1,
        grid=(bsz, 1 + n_pages // pg),
        in_specs=[new(GW, C_FQ), new(GW, C_FK), new(GW, C_FV), new(LANES, C_FF),
                  pl.BlockSpec((1, LANES), lambda b, s, pt: (0, 0)),
                  pl.BlockSpec((None, pg * FOX_HEADS, LANES), lambda b, s, pt: (b, jnp.maximum(s - 1, 0), 0))]
                 + [_page_spec((None, None, 2 * GW, PAGE), layer, pg, i, (0, 0), first_step=1) for i in range(pg)],
        out_specs=[pl.BlockSpec((T_PAD, GW), lambda b, s, pt: (b, 0)),
                   pl.BlockSpec((T_PAD, LANES), lambda b, s, pt: (b, 0))],
        scratch_shapes=[pltpu.VMEM((rows, GW), BF16), pltpu.VMEM((rows, 1), F32), pltpu.VMEM((rows, 1), F32),
                        pltpu.VMEM((rows, 1), F32), pltpu.VMEM((rows, GW), F32)],
    )
    return pl.pallas_call(
        functools.partial(_fox_attn_s_body, n_real=n_real),
        grid_spec=grid_spec,
        out_shape=[jax.ShapeDtypeStruct((bsz * T_PAD, GW), F32), jax.ShapeDtypeStruct((bsz * T_PAD, LANES), F32)],
        compiler_params=_cparams(("parallel", "arbitrary")),
        name="fox_attn_s",
    )(page_table, p, p, p, p, bias, r_past, *([cache_kv] * pg))


def _nsa_cmp_s_body(pt_ref, *refs, n_pages):
    pg = NSA_CMP_PAGES
    k_pages, v_pages = refs[:pg], refs[pg:2 * pg]
    pek_ref, pev_ref, mk_ref, mv_ref, o_ref, xk_ref, xv_ref = refs[2 * pg:]
    s = pl.program_id(1)
    for i in range(pg):
        rows = pl.ds(pl.multiple_of((s * pg + i) * PAGE, PAGE), PAGE)
        xk_ref[rows, :] = k_pages[i][...]
        xv_ref[rows, :] = v_pages[i][...]

    @pl.when(s == pl.num_programs(1) - 1)
    def _():
        per_page = PAGE // CMP_BLOCK
        width = per_page * HEAD_DIM
        for x_ref, pe_ref, m_ref, off in ((xk_ref, pek_ref, mk_ref, 0), (xv_ref, pev_ref, mv_ref, 2 * HEAD_DIM)):
            acc = jnp.zeros((NSA_KV_HEADS * n_pages, width), F32)
            for d in range(HEAD_DIM):
                rows = jnp.concatenate([x_ref[pl.ds(g * HEAD_DIM + d, n_pages, stride=PAGE), :]
                                        for g in range(NSA_KV_HEADS)], axis=0)
                acc = acc + _dot((rows + pe_ref[d:d + 1, :]).astype(BF16), m_ref[d])
            for g in range(NSA_KV_HEADS):
                for n in range(per_page):
                    o_ref[n * n_pages:(n + 1) * n_pages, off + g * HEAD_DIM:off + (g + 1) * HEAD_DIM] = (
                        acc[g * n_pages:(g + 1) * n_pages, n * HEAD_DIM:(n + 1) * HEAD_DIM])


def nsa_cmp_s_call(page_table, cache_kv, pek, pev, mk, mv, *, layer, bsz, n_pages):
    pg = NSA_CMP_PAGES
    assert n_pages % pg == 0
    per_page = PAGE // CMP_BLOCK
    nbc = n_pages * per_page
    const = lambda shape: pl.BlockSpec(shape, lambda b, s, pt: (0,) * len(shape))
    grid_spec = pltpu.PrefetchScalarGridSpec(
        num_scalar_prefetch=1,
        grid=(bsz, n_pages // pg),
        in_specs=[_page_spec((None, None, PAGE, PAGE), layer, pg, i, (0, 0)) for i in range(pg)]
                 + [_page_spec((None, None, PAGE, PAGE), layer, pg, i, (1, 0)) for i in range(pg)]
                 + [const((HEAD_DIM, PAGE)), const((HEAD_DIM, PAGE)),
                    const((HEAD_DIM, PAGE, per_page * HEAD_DIM)), const((HEAD_DIM, PAGE, per_page * HEAD_DIM))],
        out_specs=pl.BlockSpec((None, nbc, 2 * LANES), lambda b, s, pt: (b, 0, 0)),
        scratch_shapes=[pltpu.VMEM((n_pages * PAGE, PAGE), F32), pltpu.VMEM((n_pages * PAGE, PAGE), F32)],
    )
    return pl.pallas_call(
        functools.partial(_nsa_cmp_s_body, n_pages=n_pages),
        grid_spec=grid_spec,
        out_shape=jax.ShapeDtypeStruct((bsz, nbc, 2 * LANES), F32),
        compiler_params=_cparams(("parallel", "arbitrary")),
        name="nsa_cmp_s",
    )(page_table, *([cache_kv] * (2 * pg)), pek, pev, mk, mv)


def _stack_heads(q_ref, g):
    hg_n = NSA_GROUP
    return jnp.concatenate(
        [q_ref[:, (g * hg_n + hg) * HEAD_DIM:(g * hg_n + hg + 1) * HEAD_DIM] for hg in range(hg_n)], axis=0)


def _nsa_topk_s_body(q_ref, cmp_ref, o_ref, idx_ref, *, q_off, nbc):
    hg_n = NSA_GROUP
    per_page = PAGE // CMP_BLOCK
    n_pages = nbc // per_page
    w = 2 * n_pages + LANES
    t_col = q_off + lax.broadcasted_iota(jnp.int32, (T_PAD, 1), 0)
    t_stack = jnp.concatenate([t_col] * hg_n, axis=0)
    c = lax.broadcasted_iota(jnp.int32, (1, nbc), 1)
    cmp_blk = per_page * (c % n_pages) + c // n_pages

    def slc_blk(i):
        return jnp.where(i < n_pages, 2 * i, jnp.where(i < 2 * n_pages, 2 * (i - n_pages) + 1, i))

    blk_i = slc_blk(lax.broadcasted_iota(jnp.int32, (w, w), 0))
    blk_j = slc_blk(lax.broadcasted_iota(jnp.int32, (w, w), 1))
    blk_row = slc_blk(lax.broadcasted_iota(jnp.int32, (1, w), 1))
    rr = lax.broadcasted_iota(jnp.int32, (TOP_N, w), 0)
    lane = lax.broadcasted_iota(jnp.int32, (TOP_N, LANES), 1)
    for g in range(NSA_KV_HEADS):
        qs = _stack_heads(q_ref, g).astype(BF16)
        o_cmp, pc = _cmp_attend(qs, cmp_ref, g, t_stack, cmp_blk, cmp_blk >= 0)
        for hg in range(hg_n):
            h = g * hg_n + hg
            o_ref[:, h * HEAD_DIM:(h + 1) * HEAD_DIM] = o_cmp[hg * T_PAD:(hg + 1) * T_PAD]
        imp = pc[0:T_PAD]
        for hg in range(1, hg_n):
            imp = imp + pc[hg * T_PAD:(hg + 1) * T_PAD]
        imp = jnp.concatenate([imp[:, 0:n_pages] + imp[:, n_pages:2 * n_pages],
                               imp[:, 2 * n_pages:3 * n_pages] + imp[:, 3 * n_pages:4 * n_pages],
                               jnp.zeros((T_PAD, LANES), F32)], axis=1)
        score = _select_score(imp, t_col, blk_row)
        score_t = jnp.concatenate([score, jnp.zeros((LANES - T_PAD, w), F32)], axis=0).T
        out = jnp.zeros((TOP_N, LANES), F32)
        for t in range(T_PAD):
            col = score_t[:, t:t + 1]
            row = score[t:t + 1, :]
            ahead = (col > row) | ((col == row) & (blk_i < blk_j))
            rank = jnp.sum(ahead.astype(F32), axis=0, keepdims=True)
            hit = (rank == rr.astype(F32)) & (row >= 0.0)
            found = jnp.sum(hit.astype(F32), axis=1, keepdims=True)
            which = jnp.sum(jnp.where(hit, blk_row.astype(F32), 0.0), axis=1, keepdims=True)
            out = jnp.where(lane == t, jnp.where(found > 0.5, which, -1.0), out)
        idx_ref[g] = out.astype(jnp.int32)


def nsa_topk_s_call(p, cmp, *, bsz, q_off):
    nbc = cmp.shape[1]
    return pl.pallas_call(
        functools.partial(_nsa_topk_s_body, q_off=q_off, nbc=nbc),
        grid=(bsz,),
        in_specs=[pl.BlockSpec((T_PAD, GW), lambda b: (b, C_NQ // GW)),
                  pl.BlockSpec((None, nbc, 2 * LANES), lambda b: (b, 0, 0))],
        out_specs=[pl.BlockSpec((T_PAD, GW), lambda b: (b, 0)),
                   pl.BlockSpec((None, NSA_KV_HEADS, TOP_N, LANES), lambda b: (b, 0, 0, 0))],
        out_shape=[jax.ShapeDtypeStruct((bsz * T_PAD, GW), F32),
                   jax.ShapeDtypeStruct((bsz, NSA_KV_HEADS, TOP_N, LANES), jnp.int32)],
        compiler_params=_cparams(("parallel",)),
        name="nsa_topk_s",
    )(p, cmp)


def _nsa_sel_s_body(pt_ref, idx_ref, q_ref, kn_ref, vn_ref, *refs, n_real, n_tok, q_off):
    k_blocks, v_blocks, o_ref = refs[:TOP_N], refs[TOP_N:2 * TOP_N], refs[2 * TOP_N]
    b, g, t = pl.program_id(0), pl.program_id(1), pl.program_id(2)
    scale = HEAD_DIM ** -0.5
    hg_n = NSA_GROUP
    base = ((b * NSA_KV_HEADS + g) * n_tok + t) * TOP_N
    new_blk = q_off // SLC_BLOCK
    qrow = q_ref[pl.ds(t, 1), :]
    qs = jnp.concatenate([qrow[:, hg * HEAD_DIM:(hg + 1) * HEAD_DIM] for hg in range(hg_n)]
                         + [jnp.zeros((SUBLANES - hg_n, HEAD_DIM), F32)], axis=0).astype(BF16)

    def pick(blk):
        return jnp.where(g == 0, blk[:, 0:HEAD_DIM], blk[:, HEAD_DIM:2 * HEAD_DIM])

    lane_half = lax.broadcasted_iota(jnp.int32, (SUBLANES, PAGE), 1) // SLC_BLOCK
    logits, values = [], []
    has_new = jnp.bool_(False)
    for r in range(TOP_N):
        j = idx_ref[base + r]
        from_cache = (j >= 0) & (j < new_blk)
        has_new = has_new | (j == new_blk)
        sc = _dot(qs, k_blocks[r][...].astype(BF16)) * scale
        logits.append(jnp.where(from_cache & (lane_half == j % 2), sc, -jnp.inf))
        values.append(lambda pr, r=r: _dot_nt(pr, v_blocks[r][...].astype(BF16)))
    sc = _dot_nt(qs, pick(kn_ref[...]).astype(BF16)) * scale
    key = lax.broadcasted_iota(jnp.int32, (SUBLANES, T_PAD), 1)
    logits.append(jnp.where(has_new & (key <= t) & (key < n_real), sc, -jnp.inf))
    values.append(lambda pr: _dot(pr, pick(vn_ref[...]).astype(BF16)))
    m = logits[0].max(axis=1, keepdims=True)
    for x in logits[1:]:
        m = jnp.maximum(m, x.max(axis=1, keepdims=True))
    l = jnp.zeros((SUBLANES, 1), F32)
    acc = jnp.zeros((SUBLANES, HEAD_DIM), F32)
    for x, pv in zip(logits, values):
        pr = jnp.exp(x - m)
        l = l + jnp.sum(pr, axis=1, keepdims=True)
        acc = acc + pv(pr.astype(BF16))
    o_ref[...] = acc / l


def nsa_sel_s_call(page_table, idx, p, cache_kv, *, layer, bsz, n_tok, n_real, q_off):
    n_cached = q_off // SLC_BLOCK
    blocks_per_page = PAGE // SLC_BLOCK

    def blk_spec(r, kind):
        def index_map(b, g, t, pt, ix):
            j = jnp.clip(ix[((b * NSA_KV_HEADS + g) * n_tok + t) * TOP_N + r], 0, n_cached - 1)
            return (pt[b, j // blocks_per_page], layer, kind * NSA_KV_HEADS + g, 0)
        return pl.BlockSpec((None, None, HEAD_DIM, PAGE), index_map)

    grid_spec = pltpu.PrefetchScalarGridSpec(
        num_scalar_prefetch=2,
        grid=(bsz, NSA_KV_HEADS, n_tok),
        in_specs=[pl.BlockSpec((T_PAD, 2 * LANES), lambda b, g, t, pt, ix: (b, C_NQ // (2 * LANES) + g)),
                  pl.BlockSpec((T_PAD, LANES), lambda b, g, t, pt, ix: (b, C_NKV // LANES + 2)),
                  pl.BlockSpec((T_PAD, LANES), lambda b, g, t, pt, ix: (b, C_NKV // LANES + 3))]
                 + [blk_spec(r, 2) for r in range(TOP_N)] + [blk_spec(r, 3) for r in range(TOP_N)],
        out_specs=pl.BlockSpec((None, None, None, SUBLANES, HEAD_DIM), lambda b, g, t, pt, ix: (b, g, t, 0, 0)),
    )
    return pl.pallas_call(
        functools.partial(_nsa_sel_s_body, n_real=n_real, n_tok=n_tok, q_off=q_off),
        grid_spec=grid_spec,
        out_shape=jax.ShapeDtypeStruct((bsz, NSA_KV_HEADS, n_tok, SUBLANES, HEAD_DIM), F32),
        compiler_params=_cparams(("parallel", "arbitrary", "arbitrary")),
        name="nsa_sel_s",
    )(page_table, idx, p, p, p, *([cache_kv] * (2 * TOP_N)))


def _nsa_win_s_body(q_ref, gl_ref, oc_ref, os_ref, win_ref, new_ref, o_ref, *, n_real, win_len):
    hg_n = NSA_GROUP
    rows = hg_n * T_PAD
    scale = HEAD_DIM ** -0.5
    gate = _sigmoid(gl_ref[...])
    t_row = lax.broadcasted_iota(jnp.int32, (rows, 1), 0) % T_PAD
    key = lax.broadcasted_iota(jnp.int32, (rows, win_len), 1)
    mask_c = key + WINDOW > t_row + win_len
    new_i = lax.broadcasted_iota(jnp.int32, (rows, PAGE), 1)
    mask_n = (new_i <= t_row) & (new_i < n_real)
    zpad = jnp.zeros((PAGE - T_PAD, HEAD_DIM), F32)
    for g in range(NSA_KV_HEADS):
        qs = _stack_heads(q_ref, g).astype(BF16)
        kt = win_ref[g * HEAD_DIM:(g + 1) * HEAD_DIM, :].astype(BF16)
        vt = win_ref[LANES + g * HEAD_DIM:LANES + (g + 1) * HEAD_DIM, :].astype(BF16)
        kn = jnp.concatenate([new_ref[:, g * HEAD_DIM:(g + 1) * HEAD_DIM], zpad], axis=0).astype(BF16)
        vn = jnp.concatenate([new_ref[:, LANES + g * HEAD_DIM:LANES + (g + 1) * HEAD_DIM], zpad], axis=0).astype(BF16)
        sc_c = jnp.where(mask_c, _dot(qs, kt) * scale, -jnp.inf)
        sc_n = jnp.where(mask_n, _dot_nt(qs, kn) * scale, -jnp.inf)
        m = jnp.maximum(jnp.max(sc_c, axis=1, keepdims=True), jnp.max(sc_n, axis=1, keepdims=True))
        pr_c = jnp.exp(sc_c - m)
        pr_n = jnp.exp(sc_n - m)
        l = jnp.sum(pr_c, axis=1, keepdims=True) + jnp.sum(pr_n, axis=1, keepdims=True)
        o_win = (_dot_nt(pr_c.astype(BF16), vt) + _dot(pr_n.astype(BF16), vn)) / l
        for hg in range(hg_n):
            h = g * hg_n + hg
            cs = slice(h * HEAD_DIM, (h + 1) * HEAD_DIM)
            o_ref[:, cs] = (gate[:, 3 * h:3 * h + 1] * oc_ref[:, cs] + gate[:, 3 * h + 1:3 * h + 2] * os_ref[:, cs]
                            + gate[:, 3 * h + 2:3 * h + 3] * o_win[hg * T_PAD:(hg + 1) * T_PAD])


def nsa_win_s_call(p, o_cmp, o_slc, cache_win, *, layer, bsz, n_real):
    win_len = cache_win.shape[3]
    assert win_len == WINDOW
    row = lambda width, off: pl.BlockSpec((T_PAD, width), lambda b: (b, off // width))
    return pl.pallas_call(
        functools.partial(_nsa_win_s_body, n_real=n_real, win_len=win_len),
        grid=(bsz,),
        in_specs=[row(GW, C_NQ), row(LANES, C_NG), row(GW, 0), row(GW, 0),
                  pl.BlockSpec((None, None, 2 * LANES, win_len), lambda b: (b, layer, 0, 0)),
                  row(2 * LANES, C_NKV + 4 * LANES)],
        out_specs=pl.BlockSpec((T_PAD, GW), lambda b: (b, 0)),
        out_shape=jax.ShapeDtypeStruct((bsz * T_PAD, GW), F32),
        compiler_params=_cparams(("parallel",)),
        name="nsa_win_s",
    )(p, p, o_cmp, o_slc, cache_win, p)


def _lane_pad(v):
    return jnp.pad(v, (0, LANES - v.shape[0]))[None, :]


def _layer_weights(w, l):
    ff = lambda m, axis: jnp.pad(m, [(0, FF_PAD - D_FF) if a == axis else (0, 0) for a in range(2)]).astype(BF16)
    win = w['w_in'][l]
    o_fox = GW + SSD_CONV_DIM + SSD_HEADS
    o_nsa = o_fox + 3 * GW + FOX_HEADS
    o_s5 = o_nsa + GW + 6 * NSA_KV_HEADS * HEAD_DIM + 3 * NSA_HEADS
    cols = lambda a, b: win[:, a:b]
    zpad = lambda n: jnp.zeros((D_MODEL, n), F32)
    w_in = jnp.concatenate([
        cols(GW, GW + SSD_CONV_DIM),
        cols(GW + SSD_CONV_DIM, o_fox), zpad(LANES - SSD_HEADS),
        cols(o_fox + 3 * GW, o_nsa), zpad(LANES - FOX_HEADS),
        cols(0, GW),
        cols(o_s5, o_s5 + GW),
        cols(o_fox, o_fox + 3 * GW),
        cols(o_nsa, o_nsa + GW),
        cols(o_nsa + GW, o_nsa + GW + 6 * LANES),
        cols(o_nsa + GW + 6 * LANES, o_s5), zpad(LANES - 3 * NSA_HEADS),
        zpad(P_W - C_NG - LANES),
    ], axis=1).astype(BF16)
    assert w_in.shape == (D_MODEL, P_W)
    pe4, w4 = nsa_cmp_params(w['nsa_cmp_pe'][l], w['nsa_cmp_w'][l])
    s5 = s5_params(w['s5_lambda_re'][l], w['s5_lambda_im'][l], w['s5_log_dt'][l],
                   w['s5_b_re'][l], w['s5_b_im'][l], w['s5_c_re'][l], w['s5_c_im'][l])
    return dict(
        ffn1=(w['ffn1_norm'][l][None, :], ff(w['ffn1_w1'][l], 1), ff(w['ffn1_w3'][l], 1), ff(w['ffn1_w2'][l], 0)),
        ffn2=(w['ffn2_norm'][l][None, :], ff(w['ffn2_w1'][l], 1), ff(w['ffn2_w3'][l], 1), ff(w['ffn2_w2'][l], 0)),
        mix_norm=w['mix_norm'][l][None, :], w_in=w_in,
        ssd=(w['ssd_conv_w'][l], w['ssd_conv_b'][l][None, :], _lane_pad(w['ssd_dt_bias'][l]),
             _lane_pad(w['ssd_a_log'][l]), _lane_pad(w['ssd_d'][l])),
        fox_bias=_lane_pad(w['fox_f_bias'][l]),
        pe4=pe4, w4=w4, cmp_t=nsa_cmp_params_t(w['nsa_cmp_pe'][l], w['nsa_cmp_w'][l]), s5=s5, s5_d=w['s5_d'][l][None, :], s5_glu=w['s5_w_glu'][l].astype(BF16),
        gains=jnp.stack([w['ssd_norm'][l], w['fox_out_norm'][l], w['nsa_out_norm'][l], w['s5_out_norm'][l]]),
        w_out=w['w_out'][l].astype(BF16),
    )


def _time_major(x, bsz, t_len):
    return jnp.swapaxes(x.reshape(bsz, t_len, -1), 0, 1).reshape(t_len * bsz, -1)


def _batch_major(x, bsz, t_len):
    return jnp.swapaxes(x.reshape(t_len, bsz, -1), 0, 1).reshape(bsz * t_len, -1)


def _prompt_layer(x, lw, final_gain, *, bsz, t_len, last):
    tm = 512
    x = ffn_call(x, *lw['ffn1'], final_gain, tm=tm, final_norm=False)
    p = inproj_call(x, lw['mix_norm'], lw['w_in'], tm=tm)
    conv0 = jnp.zeros((bsz, CONV_W - 1, SSD_CONV_DIM), F32)
    ssm0 = jnp.zeros((bsz, SSD_HEADS, HEAD_DIM, SSD_STATE), F32)
    y_ssd, ssm, conv = ssd_call(p, conv0, ssm0, *lw['ssd'], bsz=bsz, t_len=t_len, n_real=SSD_CHUNK)
    lf, cumt = fox_prep_call(p, lw['fox_bias'], bsz=bsz, t_len=t_len)
    y_fox = fox_attn_call(p, cumt, bsz=bsz, t_len=t_len)
    cmp = nsa_cmp_p_call(p, lw['pe4'], lw['w4'], bsz=bsz, t_len=t_len)
    y_nsa = nsa_attn_p_call(p, cmp, bsz=bsz, t_len=t_len)
    bre, bim, are, aim, cre, cim = lw['s5']
    x0 = jnp.zeros((SUBLANES, S5_N), F32)
    u = _time_major(p[:, C_U:C_U + GW], bsz, t_len)
    y_s5, xr, xi = s5_call(u, bre, bim, are, aim, x0, x0, cre, cim, lw['s5_d'], lw['s5_glu'],
                           nb=bsz, t_len=t_len, steps=256, t_last=t_len - 1)
    y_s5 = _batch_major(y_s5, bsz, t_len)
    x = outproj_call(x, y_ssd, y_fox, y_nsa, y_s5, lw['gains'], lw['w_out'], tm=tm)
    x = ffn_call(x, *lw['ffn2'], final_gain, tm=tm, final_norm=last)
    p3 = p.reshape(bsz, t_len, P_W)
    keep = min(WINDOW, t_len)
    off = ((t_len - 1) % (SUBLANES // bsz)) * bsz
    states = (
        p3[:, :, C_FK:C_FK + 2 * GW].reshape(bsz, t_len, 2, FOX_HEADS, HEAD_DIM),
        lf.reshape(bsz, t_len, LANES)[:, :, :FOX_HEADS],
        p3[:, :, C_NKV:C_NKV + 4 * LANES].reshape(bsz, t_len, 4, NSA_KV_HEADS, HEAD_DIM),
        p3[:, t_len - keep:, C_NKV + 4 * LANES:C_NKV + 6 * LANES].reshape(bsz, keep, 2, NSA_KV_HEADS, HEAD_DIM),
        ssm, conv,
        xr[off:off + bsz].reshape(bsz, S5_GROUPS, S5_STATE),
        xi[off:off + bsz].reshape(bsz, S5_GROUPS, S5_STATE),
    )
    return x, states


def _sample_layer(x, lw, final_gain, caches, page_table, *, layer, bsz, n_real, q_off, last):
    fox_kv_t, fox_lf_t, nsa_kv_t, nsa_win_t, cache_nsa_win, st_ssd, st_conv, st_re, st_im = caches
    tm = bsz * T_PAD
    n_pages = page_table.shape[1]
    assert q_off == n_pages * PAGE and q_off % SLC_BLOCK == 0 and bsz == SUBLANES
    x = ffn_call(x, *lw['ffn1'], final_gain, tm=tm, final_norm=False)
    p = inproj_call(x, lw['mix_norm'], lw['w_in'], tm=tm)
    p3 = p.reshape(bsz, T_PAD, P_W)

    p_ssd = jnp.pad(p3[:, :, :C_U], ((0, 0), (0, SSD_CHUNK - T_PAD), (0, 0))).reshape(bsz * SSD_CHUNK, C_U)
    y_ssd, ssm, conv = ssd_call(p_ssd, st_conv[:, layer], st_ssd[:, layer], *lw['ssd'],
                                bsz=bsz, t_len=SSD_CHUNK, n_real=n_real)
    y_ssd = y_ssd.reshape(bsz, SSD_CHUNK, GW)[:, :T_PAD].reshape(tm, GW)

    r_past = fox_prep_s_call(page_table, fox_lf_t, layer=layer, bsz=bsz, n_pages=n_pages)
    y_fox, lf = fox_attn_s_call(page_table, p, lw['fox_bias'], r_past, fox_kv_t,
                                layer=layer, bsz=bsz, n_pages=n_pages, n_real=n_real)

    cmp = nsa_cmp_s_call(page_table, nsa_kv_t, *lw['cmp_t'], layer=layer, bsz=bsz, n_pages=n_pages)
    o_cmp, idx = nsa_topk_s_call(p, cmp, bsz=bsz, q_off=q_off)
    idx = jnp.swapaxes(idx[:, :, :, :n_real], 2, 3).reshape(-1)
    o_slc = nsa_sel_s_call(page_table, idx, p, nsa_kv_t,
                           layer=layer, bsz=bsz, n_tok=n_real, n_real=n_real, q_off=q_off)
    o_slc = jnp.transpose(o_slc[:, :, :, :NSA_GROUP], (0, 2, 1, 3, 4)).reshape(bsz, n_real, GW)
    o_slc = jnp.pad(o_slc, ((0, 0), (0, T_PAD - n_real), (0, 0))).reshape(tm, GW)
    y_nsa = nsa_win_s_call(p, o_cmp, o_slc, nsa_win_t, layer=layer, bsz=bsz, n_real=n_real)

    bre, bim, are, aim, cre, cim = lw['s5']
    u = _time_major(p[:, C_U:C_U + GW], bsz, T_PAD)
    y_s5, xr, xi = s5_call(u, bre, bim, are, aim, st_re[:, layer].reshape(bsz, S5_N), st_im[:, layer].reshape(bsz, S5_N),
                           cre, cim, lw['s5_d'], lw['s5_glu'], nb=bsz, t_len=T_PAD, steps=T_PAD, t_last=n_real - 1)
    y_s5 = _batch_major(y_s5, bsz, T_PAD)

    x = outproj_call(x, y_ssd, y_fox, y_nsa, y_s5, lw['gains'], lw['w_out'], tm=tm)
    x = ffn_call(x, *lw['ffn2'], final_gain, tm=tm, final_norm=last)
    new = p3[:, :n_real]
    win_rows = new[:, :, C_NKV + 4 * LANES:C_NKV + 6 * LANES].reshape(bsz, n_real, 2, NSA_KV_HEADS, HEAD_DIM)
    states = (
        new[:, :, C_FK:C_FK + 2 * GW].reshape(bsz, n_real, 2, FOX_HEADS, HEAD_DIM),
        lf.reshape(bsz, T_PAD, LANES)[:, :n_real, :FOX_HEADS],
        new[:, :, C_NKV:C_NKV + 4 * LANES].reshape(bsz, n_real, 4, NSA_KV_HEADS, HEAD_DIM),
        jnp.concatenate([cache_nsa_win[:, layer, n_real:], win_rows], axis=1),
        ssm, conv,
        xr.reshape(bsz, S5_GROUPS, S5_STATE), xi.reshape(bsz, S5_GROUPS, S5_STATE),
    )
    return x, states


def kernel(x_prompt, x_sample, cache_fox_kv, cache_fox_logf, cache_nsa_kv, cache_nsa_win_kv, state_ssd,
           state_ssd_conv, state_s5_re, state_s5_im, page_table, ffn1_norm, ffn1_w1, ffn1_w3, ffn1_w2, mix_norm,
           w_in, ssd_conv_w, ssd_conv_b, ssd_dt_bias, ssd_a_log, ssd_d, ssd_norm, fox_f_bias, fox_out_norm,
           nsa_cmp_pe, nsa_cmp_w, nsa_out_norm, s5_lambda_re, s5_lambda_im, s5_log_dt, s5_b_re, s5_b_im,
           s5_c_re, s5_c_im, s5_d, s5_w_glu, s5_out_norm, w_out, ffn2_norm, ffn2_w1, ffn2_w3, ffn2_w2, final_norm):
    w = dict(ffn1_norm=ffn1_norm, ffn1_w1=ffn1_w1, ffn1_w3=ffn1_w3, ffn1_w2=ffn1_w2, mix_norm=mix_norm, w_in=w_in,
             ssd_conv_w=ssd_conv_w, ssd_conv_b=ssd_conv_b, ssd_dt_bias=ssd_dt_bias, ssd_a_log=ssd_a_log, ssd_d=ssd_d,
             ssd_norm=ssd_norm, fox_f_bias=fox_f_bias, fox_out_norm=fox_out_norm, nsa_cmp_pe=nsa_cmp_pe,
             nsa_cmp_w=nsa_cmp_w, nsa_out_norm=nsa_out_norm, s5_lambda_re=s5_lambda_re, s5_lambda_im=s5_lambda_im,
             s5_log_dt=s5_log_dt, s5_b_re=s5_b_re, s5_b_im=s5_b_im, s5_c_re=s5_c_re, s5_c_im=s5_c_im, s5_d=s5_d,
             s5_w_glu=s5_w_glu, s5_out_norm=s5_out_norm, w_out=w_out, ffn2_norm=ffn2_norm, ffn2_w1=ffn2_w1,
             ffn2_w3=ffn2_w3, ffn2_w2=ffn2_w2)
    bsz_p, t_len, _ = x_prompt.shape
    bsz_s, n_real, _ = x_sample.shape
    depth = w_in.shape[0]
    q_off = page_table.shape[1] * PAGE
    fg = final_norm[None, :]
    n_pool = cache_fox_kv.shape[0]
    keys_minor = (0, 1, 3, 4, 5, 2)
    caches = (jnp.transpose(cache_fox_kv, keys_minor).reshape(n_pool, depth, 2 * GW, PAGE),
              jnp.transpose(cache_fox_logf, (0, 1, 3, 2)),
              jnp.transpose(cache_nsa_kv, keys_minor).reshape(n_pool, depth, 4 * LANES, PAGE),
              jnp.transpose(cache_nsa_win_kv, keys_minor).reshape(bsz_s, depth, 2 * LANES, WINDOW),
              cache_nsa_win_kv, state_ssd, state_ssd_conv, state_s5_re, state_s5_im)
    xp = x_prompt.reshape(bsz_p * t_len, D_MODEL)
    xs = jnp.pad(x_sample, ((0, 0), (0, T_PAD - n_real), (0, 0))).reshape(bsz_s * T_PAD, D_MODEL)
    st_p, st_s = [], []
    for l in range(depth):
        lw = _layer_weights(w, l)
        last = l == depth - 1
        xp, sp = _prompt_layer(xp, lw, fg, bsz=bsz_p, t_len=t_len, last=last)
        xs, ss = _sample_layer(xs, lw, fg, caches, page_table, layer=l, bsz=bsz_s, n_real=n_real, q_off=q_off,
                               last=last)
        st_p.append(sp)
        st_s.append(ss)
    y_p = xp.reshape(bsz_p, t_len, D_MODEL)
    y_s = xs.reshape(bsz_s, T_PAD, D_MODEL)[:, :n_real]
    out = [y_p, y_s]
    for i in range(8):
        out.append(jnp.stack([s[i] for s in st_p], axis=1))
        out.append(jnp.stack([s[i] for s in st_s], axis=1))
    return tuple(out)
```

```python
import functools
import math

import jax
import jax.numpy as jnp
from jax import lax
from jax.experimental import pallas as pl
from jax.experimental.pallas import tpu as pltpu

F32 = jnp.float32
BF16 = jnp.bfloat16
HIGHEST = lax.Precision.HIGHEST

D_MODEL = 2048
DEPTH = 2
HEAD_DIM = 64
GW = D_MODEL // 4
D_FF = ((8 * D_MODEL // 3 + 127) // 128) * 128
EPS = 1e-6
TINY = 1e-30
SSD_HEADS = GW // HEAD_DIM
SSD_GROUPS = 2
SSD_STATE = 64
CONV_W = 4
SSD_CONV_DIM = GW + 2 * SSD_GROUPS * SSD_STATE
SSD_CHUNK = 128
FOX_HEADS = GW // HEAD_DIM
NSA_HEADS = GW // HEAD_DIM
NSA_KV_HEADS = 2
NSA_GROUP = NSA_HEADS // NSA_KV_HEADS
CMP_BLOCK = 32
SLC_BLOCK = 64
TOP_N = 16
WINDOW = 512
FORCE_SCORE = 1e4
S5_CH = 16
S5_GROUPS = GW // S5_CH
S5_STATE = 64
S5_N = S5_GROUPS * S5_STATE
PAGE = 128

LANES = 128
SUBLANES = 8
VMEM_LIMIT = 56 * 1024 * 1024

C_XBC = 0
C_DT = 768
C_FF = 896
C_Z = 1024
C_U = 1536
C_FQ = 2048
C_FK = 2560
C_FV = 3072
C_NQ = 3584
C_NKV = 4096
C_NG = 4864
P_W = 5120
FF_TILE = 512


def _cparams(sem):
    return pltpu.CompilerParams(dimension_semantics=sem, vmem_limit_bytes=VMEM_LIMIT)


def _rms(x, g):
    ms = jnp.mean(x * x, axis=-1, keepdims=True)
    return x * lax.rsqrt(ms + EPS) * g


def _sigmoid(x):
    return 1.0 / (1.0 + jnp.exp(-x))


def _silu(x):
    return x * _sigmoid(x)


def _softplus(x):
    return jnp.maximum(x, 0.0) + jnp.log(1.0 + jnp.exp(-jnp.abs(x)))


def _log_sigmoid(x):
    return jnp.minimum(x, 0.0) - jnp.log(1.0 + jnp.exp(-jnp.abs(x)))


def _dot(a, b):
    return jnp.dot(a, b, preferred_element_type=F32)


def _dot_nt(a, b):
    return lax.dot_general(a, b, (((1,), (1,)), ((), ())), preferred_element_type=F32)


def _dot_tn(a, b):
    return lax.dot_general(a, b, (((0,), (0,)), ((), ())), preferred_element_type=F32)


def _ffn_body(x_ref, g_ref, w1_ref, w3_ref, w2_ref, fg_ref, o_ref, h_ref, act_ref, *, final_norm, nk, tn):
    s = pl.program_id(1)
    tail = D_FF - (nk - 1) * FF_TILE

    @pl.when(s == 0)
    def _():
        h_ref[...] = _rms(x_ref[...], g_ref[...]).astype(BF16)

    def hidden_tile():
        h = h_ref[...]
        return (_silu(_dot(h, w1_ref[...])) * _dot(h, w3_ref[...])).astype(BF16)

    @pl.when(s < nk - 1)
    def _():
        act_ref[:, pl.ds(pl.multiple_of(s * FF_TILE, FF_TILE), FF_TILE)] = hidden_tile()

    @pl.when(s == nk - 1)
    def _():
        act_ref[:, (nk - 1) * FF_TILE:D_FF] = hidden_tile()[:, :tail]

    @pl.when(s >= nk)
    def _():
        cols = pl.ds(pl.multiple_of((s - nk) * tn, tn), tn)
        o_ref[:, cols] = x_ref[:, cols] + 0.5 * _dot(act_ref[...], w2_ref[...])

    if final_norm:
        @pl.when(s == pl.num_programs(1) - 1)
        def _():
            o_ref[...] = _rms(o_ref[...], fg_ref[...])


def ffn_call(x, g, w1, w3, w2, fg, *, tm, layer, final_norm, tn=512):
    m = x.shape[0]
    nk = pl.cdiv(D_FF, FF_TILE)
    assert m % tm == 0 and w1.shape[1:] == (D_MODEL, D_FF) and w2.shape[1:] == (D_FF, D_MODEL) and D_MODEL % tn == 0
    hid = lambda i, s: (layer, 0, jnp.minimum(s, nk - 1))
    return pl.pallas_call(
        functools.partial(_ffn_body, final_norm=final_norm, nk=nk, tn=tn),
        grid=(m // tm, nk + D_MODEL // tn),
        in_specs=[
            pl.BlockSpec((tm, D_MODEL), lambda i, s: (i, 0)),
            pl.BlockSpec((1, D_MODEL), lambda i, s: (0, 0)),
            pl.BlockSpec((None, D_MODEL, FF_TILE), hid),
            pl.BlockSpec((None, D_MODEL, FF_TILE), hid),
            pl.BlockSpec((None, D_FF, tn), lambda i, s: (layer, 0, jnp.maximum(s - nk, 0))),
            pl.BlockSpec((1, D_MODEL), lambda i, s: (0, 0)),
        ],
        out_specs=pl.BlockSpec((tm, D_MODEL), lambda i, s: (i, 0)),
        out_shape=jax.ShapeDtypeStruct((m, D_MODEL), F32),
        scratch_shapes=[pltpu.VMEM((tm, D_MODEL), BF16), pltpu.VMEM((tm, D_FF), BF16)],
        compiler_params=_cparams(("parallel", "arbitrary")),
        name="ffn",
    )(x, g, w1, w3, w2, fg)


IN_TILE = P_W // 2


def _inproj_body(x_ref, g_ref, w_ref, o_ref, fkv_ref, nkv_ref, win_ref, *rest, time_major):
    h_ref = rest[-1]
    j = pl.program_id(1)

    @pl.when(j == 0)
    def _():
        h_ref[...] = _rms(x_ref[...], g_ref[...]).astype(BF16)
        y = _dot(h_ref[...], w_ref[...])
        o_ref[...] = y
        if time_major:
            rest[0][...] = y[:, C_U:C_U + GW]

    @pl.when(j == 1)
    def _():
        y = _dot(h_ref[...], w_ref[...])
        o_ref[...] = y
        fkv_ref[...] = y[:, C_FK - IN_TILE:C_FK - IN_TILE + 2 * GW]
        nkv_ref[...] = y[:, C_NKV - IN_TILE:C_NKV - IN_TILE + 4 * LANES]
        win_ref[...] = y[:, C_NKV - IN_TILE + 4 * LANES:C_NKV - IN_TILE + 6 * LANES]


def inproj_call(x, g, w, *, tm, time_major=None):
    m = x.shape[0]
    assert m % tm == 0 and C_U + GW <= IN_TILE <= C_FK
    row = lambda width: pl.BlockSpec((tm, width), lambda i, j: (i, 0))
    out_specs = [pl.BlockSpec((tm, IN_TILE), lambda i, j: (i, j)), row(2 * GW), row(4 * LANES), row(2 * LANES)]
    out_shape = [jax.ShapeDtypeStruct((m, P_W), F32), jax.ShapeDtypeStruct((m, 2 * GW), F32),
                 jax.ShapeDtypeStruct((m, 4 * LANES), F32), jax.ShapeDtypeStruct((m, 2 * LANES), F32)]
    if time_major is not None:
        bsz, t_len = time_major
        nt = t_len // tm
        assert t_len % tm == 0 and m == bsz * t_len
        out_specs.append(pl.BlockSpec((tm, GW), lambda i, j: (i % nt, i // nt)))
        out_shape.append(jax.ShapeDtypeStruct((t_len, bsz * GW), F32))
    return pl.pallas_call(
        functools.partial(_inproj_body, time_major=time_major is not None),
        grid=(m // tm, P_W // IN_TILE),
        in_specs=[
            pl.BlockSpec((tm, D_MODEL), lambda i, j: (i, 0)),
            pl.BlockSpec((1, D_MODEL), lambda i, j: (0, 0)),
            pl.BlockSpec((D_MODEL, IN_TILE), lambda i, j: (0, j)),
        ],
        out_specs=out_specs,
        out_shape=out_shape,
        scratch_shapes=[pltpu.VMEM((tm, D_MODEL), BF16)],
        compiler_params=_cparams(("parallel", "arbitrary")),
        name="inproj",
    )(x, g, w)


def _outproj_body(x_ref, a_ref, b_ref, c_ref, d_ref, gn_ref, w_ref, o_ref):
    y = x_ref[...]
    for i, r in enumerate((a_ref, b_ref, c_ref, d_ref)):
        y = y + _dot(_rms(r[...], gn_ref[i:i + 1, :]).astype(BF16), w_ref[i * GW:(i + 1) * GW, :])
    o_ref[...] = y


def outproj_call(x, ya, yb, yc, yd, gains, w, *, tm, layer, time_major=None):
    m = x.shape[0]
    assert m % tm == 0
    yspec = pl.BlockSpec((tm, GW), lambda i: (i, 0))
    dspec = yspec
    if time_major is not None:
        bsz, t_len = time_major
        nt = t_len // tm
        assert t_len % tm == 0 and yd.shape == (t_len, bsz * GW)
        dspec = pl.BlockSpec((tm, GW), lambda i: (i % nt, i // nt))
    return pl.pallas_call(
        _outproj_body,
        grid=(m // tm,),
        in_specs=[
            pl.BlockSpec((tm, D_MODEL), lambda i: (i, 0)),
            yspec, yspec, yspec, dspec,
            pl.BlockSpec((4, GW), lambda i: (0, 0)),
            pl.BlockSpec((None, D_MODEL, D_MODEL), lambda i: (layer, 0, 0)),
        ],
        out_specs=pl.BlockSpec((tm, D_MODEL), lambda i: (i, 0)),
        out_shape=jax.ShapeDtypeStruct((m, D_MODEL), F32),
        compiler_params=_cparams(("parallel",)),
        name="outproj",
    )(x, ya, yb, yc, yd, gains, w)


def _ssd_body(xbc_ref, dt_ref, z_ref, conv0_ref, ssm0_ref, cw_ref, cb_ref, dtb_ref, alog_ref, dd_ref,
              y_ref, ssm_ref, conv_ref, xp_ref, act_ref, st_ref, *, n_real):
    c = pl.program_id(1)
    nc = pl.num_programs(1)
    q = SSD_CHUNK
    halo = SUBLANES

    @pl.when(c == 0)
    def _():
        xp_ref[halo - 3:halo, :] = conv0_ref[...]
        st_ref[...] = ssm0_ref[...]

    xr = xbc_ref[...]
    xp_ref[halo:halo + q, :] = xr
    conv = (cb_ref[...] + cw_ref[3:4, :] * xr
            + cw_ref[2:3, :] * xp_ref[halo - 1:halo - 1 + q, :]
            + cw_ref[1:2, :] * xp_ref[halo - 2:halo - 2 + q, :]
            + cw_ref[0:1, :] * xp_ref[halo - 3:halo - 3 + q, :])
    act_ref[...] = _silu(conv)

    row = lax.broadcasted_iota(jnp.int32, (q, LANES), 0)
    dt = jnp.where(row < n_real, _softplus(dt_ref[...] + dtb_ref[...]), 0.0)
    a = -jnp.exp(alog_ref[...])
    ti = lax.broadcasted_iota(jnp.int32, (q, q), 0)
    si = lax.broadcasted_iota(jnp.int32, (q, q), 1)
    causal = si <= ti
    acs = jnp.dot(causal.astype(F32), dt * a, preferred_element_type=F32, precision=HIGHEST)
    acs_t = acs.T
    e_acs = jnp.exp(acs)
    acs_last = acs[q - 1:q, :]
    w_end = jnp.exp(acs_last - acs) * dt
    e_last = jnp.exp(acs_last)

    for g in range(SSD_GROUPS):
        bm = act_ref[:, GW + g * SSD_STATE:GW + (g + 1) * SSD_STATE]
        cm = act_ref[:, GW + (SSD_GROUPS + g) * SSD_STATE:GW + (SSD_GROUPS + g + 1) * SSD_STATE]
        bm16 = bm.astype(BF16)
        cm16 = cm.astype(BF16)
        cb = _dot_nt(cm16, bm16)
        for hh in range(SSD_HEADS // SSD_GROUPS):
            h = g * (SSD_HEADS // SSD_GROUPS) + hh
            xs = act_ref[:, h * HEAD_DIM:(h + 1) * HEAD_DIM]
            seg = acs[:, h:h + 1] - acs_t[h:h + 1, :]
            decay = jnp.exp(jnp.where(causal, seg, -jnp.inf))
            y = _dot((cb * decay).astype(BF16), (xs * dt[:, h:h + 1]).astype(BF16))
            s_in = st_ref[h]
            y = y + _dot_nt(cm16, s_in.astype(BF16)) * e_acs[:, h:h + 1]
            y = y + dd_ref[:, h:h + 1] * xs
            cs = _dot_tn((xs * w_end[:, h:h + 1]).astype(BF16), bm16)
            st_ref[h] = e_last[:, h:h + 1] * s_in + cs
            zs = z_ref[:, h * HEAD_DIM:(h + 1) * HEAD_DIM]
            y_ref[:, h * HEAD_DIM:(h + 1) * HEAD_DIM] = y * _silu(zs)

    last_real = min(n_real, q)
    conv_ref[...] = xp_ref[halo + last_real - 3:halo + last_real, :]
    xp_ref[halo - 3:halo, :] = xp_ref[halo + q - 3:halo + q, :]

    @pl.when(c == nc - 1)
    def _():
        ssm_ref[...] = st_ref[...]


def ssd_call(p, conv0, ssm0, cw, cb, dtb, alog, dd, *, bsz, t_len, n_real):
    q = SSD_CHUNK
    nc = t_len // q
    assert t_len % q == 0 and (nc == 1 or n_real == q)
    row = lambda b, c: b * nc + c
    vec = lambda shape: pl.BlockSpec(shape, lambda b, c: (0, 0))
    return pl.pallas_call(
        functools.partial(_ssd_body, n_real=n_real),
        grid=(bsz, nc),
        in_specs=[
            pl.BlockSpec((q, SSD_CONV_DIM), lambda b, c: (row(b, c), C_XBC // SSD_CONV_DIM)),
            pl.BlockSpec((q, LANES), lambda b, c: (row(b, c), C_DT // LANES)),
            pl.BlockSpec((q, GW), lambda b, c: (row(b, c), C_Z // GW)),
            pl.BlockSpec((None, CONV_W - 1, SSD_CONV_DIM), lambda b, c: (b, 0, 0)),
            pl.BlockSpec((None, SSD_HEADS, HEAD_DIM, SSD_STATE), lambda b, c: (b, 0, 0, 0)),
            vec((CONV_W, SSD_CONV_DIM)), vec((1, SSD_CONV_DIM)), vec((1, LANES)), vec((1, LANES)), vec((1, LANES)),
        ],
        out_specs=[
            pl.BlockSpec((q, GW), lambda b, c: (row(b, c), 0)),
            pl.BlockSpec((None, SSD_HEADS, HEAD_DIM, SSD_STATE), lambda b, c: (b, 0, 0, 0)),
            pl.BlockSpec((None, CONV_W - 1, SSD_CONV_DIM), lambda b, c: (b, 0, 0)),
        ],
        out_shape=[
            jax.ShapeDtypeStruct((bsz * t_len, GW), F32),
            jax.ShapeDtypeStruct((bsz, SSD_HEADS, HEAD_DIM, SSD_STATE), F32),
            jax.ShapeDtypeStruct((bsz, CONV_W - 1, SSD_CONV_DIM), F32),
        ],
        scratch_shapes=[
            pltpu.VMEM((SUBLANES + q, SSD_CONV_DIM), F32),
            pltpu.VMEM((q, SSD_CONV_DIM), F32),
            pltpu.VMEM((SSD_HEADS, HEAD_DIM, SSD_STATE), F32),
        ],
        compiler_params=_cparams(("parallel", "arbitrary")),
        name="ssd",
    )(p, p, p, conv0, ssm0, cw, cb, dtb, alog, dd)


def _s5_body(u_ref, bre_ref, bim_ref, are_ref, aim_ref, x0r_ref, x0i_ref, cre_ref, cim_ref, d_ref, wg_ref,
             o_ref, xr_out, xi_out, xr_ref, xi_ref, sr_ref, si_ref, *, nb, tiles, t_last):
    c = pl.program_id(0)
    per = SUBLANES // nb

    @pl.when(c == 0)
    def _():
        sr_ref[...] = x0r_ref[...]
        si_ref[...] = x0i_ref[...]

    u = u_ref[...]
    u16 = u.astype(BF16)
    xr_ref[...] = _dot(u16, bre_ref[...])
    xi_ref[...] = _dot(u16, bim_ref[...])
    ar = are_ref[...]
    ai = aim_ref[...]
    first = lax.broadcasted_iota(jnp.int32, (SUBLANES, S5_N), 0) < nb

    def step(j, carry):
        sr, si = carry
        rows = pl.ds(pl.multiple_of(j * SUBLANES, SUBLANES), SUBLANES)
        br = xr_ref[rows, :]
        bi = xi_ref[rows, :]
        vr = ar * sr - ai * si + br
        vi = ar * si + ai * sr + bi
        if per == 2:
            pr = pltpu.roll(vr, nb, 0)
            pi = pltpu.roll(vi, nb, 0)
            wr = ar * pr - ai * pi + br
            wi = ar * pi + ai * pr + bi
            outr = jnp.where(first, vr, wr)
            outi = jnp.where(first, vi, wi)
            nxt = (pltpu.roll(wr, nb, 0), pltpu.roll(wi, nb, 0))
        else:
            outr, outi, nxt = vr, vi, (vr, vi)
        xr_ref[rows, :] = outr
        xi_ref[rows, :] = outi

        @pl.when(c * tiles + j == t_last // per)
        def _():
            xr_out[...] = outr
            xi_out[...] = outi

        return nxt

    sr, si = lax.fori_loop(0, tiles, step, (sr_ref[...], si_ref[...]))
    sr_ref[...] = sr
    si_ref[...] = si

    y = _dot(xr_ref[...].astype(BF16), cre_ref[...]) - _dot(xi_ref[...].astype(BF16), cim_ref[...])
    y = y + d_ref[...] * u
    g = _dot(jax.nn.gelu(y).astype(BF16), wg_ref[...])
    o_ref[...] = g[:, :GW] * _sigmoid(g[:, GW:])


def s5_call(u, bre, bim, are, aim, x0r, x0i, cre, cim, d, wg, *, nb, t_len, steps, t_last):
    assert t_len % steps == 0 and nb in (4, 8)
    rows = steps * nb
    assert rows % SUBLANES == 0
    const = lambda shape: pl.BlockSpec(shape, lambda c: (0, 0))
    return pl.pallas_call(
        functools.partial(_s5_body, nb=nb, tiles=rows // SUBLANES, t_last=t_last),
        grid=(t_len // steps,),
        in_specs=[
            pl.BlockSpec((rows, GW), lambda c: (c, 0)),
            const((GW, S5_N)), const((GW, S5_N)), const((1, S5_N)), const((1, S5_N)),
            const((SUBLANES, S5_N)), const((SUBLANES, S5_N)),
            const((S5_N, GW)), const((S5_N, GW)), const((1, GW)), const((GW, 2 * GW)),
        ],
        out_specs=[pl.BlockSpec((rows, GW), lambda c: (c, 0)), const((SUBLANES, S5_N)), const((SUBLANES, S5_N))],
        out_shape=[jax.ShapeDtypeStruct((t_len * nb, GW), F32),
                   jax.ShapeDtypeStruct((SUBLANES, S5_N), F32), jax.ShapeDtypeStruct((SUBLANES, S5_N), F32)],
        scratch_shapes=[pltpu.VMEM((rows, S5_N), F32), pltpu.VMEM((rows, S5_N), F32),
                        pltpu.VMEM((SUBLANES, S5_N), F32), pltpu.VMEM((SUBLANES, S5_N), F32)],
        compiler_params=_cparams(("arbitrary",)),
        name="s5",
    )(u, bre, bim, are, aim, x0r, x0i, cre, cim, d, wg)


def s5_params(lam_re, lam_im, log_dt, b_re, b_im, c_re, c_im):
    dt = jnp.exp(log_dt)[:, None]
    mag = jnp.exp(lam_re * dt)
    ab_re = mag * jnp.cos(lam_im * dt)
    ab_im = mag * jnp.sin(lam_im * dt)
    den = lam_re * lam_re + lam_im * lam_im
    zr = ((ab_re - 1.0) * lam_re + ab_im * lam_im) / den
    zi = (ab_im * lam_re - (ab_re - 1.0) * lam_im) / den
    bb_re = zr[..., None] * b_re - zi[..., None] * b_im
    bb_im = zr[..., None] * b_im + zi[..., None] * b_re
    eye = jnp.eye(S5_GROUPS, dtype=F32)

    def in_mat(bb):
        return jnp.einsum('gnc,gh->gchn', bb, eye).reshape(GW, S5_N).astype(BF16)

    def out_mat(cc):
        return jnp.einsum('gcn,gh->gnhc', cc, eye).reshape(S5_N, GW).astype(BF16)

    return (in_mat(bb_re), in_mat(bb_im), ab_re.reshape(1, S5_N), ab_im.reshape(1, S5_N),
            out_mat(c_re), out_mat(c_im))


LOG2E = 1.4426950408889634
NEG = -1e30


def _fox_prep_body(f_ref, b_ref, lf_ref, cumt_ref, carry_ref):
    c = pl.program_id(1)

    @pl.when(c == 0)
    def _():
        carry_ref[...] = jnp.zeros_like(carry_ref)

    lf = _log_sigmoid(f_ref[...] + b_ref[...])
    lf_ref[...] = lf
    tc = lf.shape[0]
    ti = lax.broadcasted_iota(jnp.int32, (tc, tc), 0)
    si = lax.broadcasted_iota(jnp.int32, (tc, tc), 1)
    cum = jnp.dot((si <= ti).astype(F32), lf, preferred_element_type=F32, precision=HIGHEST) + carry_ref[...]
    cumt_ref[...] = cum.T[:FOX_HEADS, :]
    carry_ref[...] = cum[tc - 1:tc, :]


def fox_prep_call(p, bias, *, bsz, t_len, tc=256):
    nc = t_len // tc
    assert t_len % tc == 0
    return pl.pallas_call(
        _fox_prep_body,
        grid=(bsz, nc),
        in_specs=[pl.BlockSpec((tc, LANES), lambda b, c: (b * nc + c, C_FF // LANES)),
                  pl.BlockSpec((1, LANES), lambda b, c: (0, 0))],
        out_specs=[pl.BlockSpec((tc, LANES), lambda b, c: (b * nc + c, 0)),
                   pl.BlockSpec((None, FOX_HEADS, tc), lambda b, c: (b, 0, c))],
        out_shape=[jax.ShapeDtypeStruct((bsz * t_len, LANES), F32),
                   jax.ShapeDtypeStruct((bsz, FOX_HEADS, t_len), F32)],
        scratch_shapes=[pltpu.VMEM((1, LANES), F32)],
        compiler_params=_cparams(("parallel", "arbitrary")),
        name="fox_prep",
    )(p, bias)


FOX_PAIRS_PER_STEP = 2


def _fox_attn_body(q_ref, k_ref, v_ref, cumt_ref, o_ref, *, tq, tk):
    hq = pl.program_id(1)
    qi = pl.program_id(2)
    q0 = qi * tq
    npair = FOX_PAIRS_PER_STEP
    lane = lax.broadcasted_iota(jnp.int32, (tq, LANES), 1)
    t_pos = q0 + lax.broadcasted_iota(jnp.int32, (tq, tk), 0)
    s_off = lax.broadcasted_iota(jnp.int32, (tq, tk), 1)
    qs = []
    for pp in range(npair):
        q = q_ref[:, pp * LANES:(pp + 1) * LANES] * (HEAD_DIM ** -0.5 * LOG2E)
        qs.append(jnp.concatenate([jnp.where(lane < HEAD_DIM, q, 0.0), jnp.where(lane >= HEAD_DIM, q, 0.0)],
                                  axis=0).astype(BF16))

    def chunk(kc, carry, masked):
        ks = pl.ds(pl.multiple_of(kc * tk, tk), tk)
        out = []
        for pp in range(npair):
            s = _dot_nt(qs[pp], k_ref[ks, pp * LANES:(pp + 1) * LANES].astype(BF16))
            vv = v_ref[ks, pp * LANES:(pp + 1) * LANES].astype(BF16)
            for hh in range(2):
                m, l, acc = carry[2 * pp + hh]
                head = 2 * (hq * npair + pp) + hh
                sh = s[hh * tq:(hh + 1) * tq] - cumt_ref[pl.ds(head, 1), ks] * LOG2E
                if masked:
                    sh = jnp.where(kc * tk + s_off <= t_pos, sh, NEG)
                m_new = jnp.maximum(m, jnp.max(sh, axis=1, keepdims=True))
                alpha = jnp.exp2(m - m_new)
                pr = jnp.exp2(sh - m_new)
                l = alpha * l + jnp.sum(pr, axis=1, keepdims=True)
                acc = alpha * acc + _dot(pr.astype(BF16), vv)
                out.append((m_new, l, acc))
        return tuple(out)

    init = tuple((jnp.full((tq, 1), NEG, F32), jnp.zeros((tq, 1), F32), jnp.zeros((tq, LANES), F32))
                 for _ in range(2 * npair))
    n_full = q0 // tk
    carry = lax.fori_loop(0, n_full, lambda kc, c: chunk(kc, c, False), init)
    res = chunk(n_full, carry, True)
    for pp in range(npair):
        (_, la, acca), (_, lb, accb) = res[2 * pp], res[2 * pp + 1]
        o_ref[:, pp * LANES:(pp + 1) * LANES] = jnp.where(lane < HEAD_DIM, acca / la, accb / lb)


def fox_attn_call(p, cumt, *, bsz, t_len, tq=128, tk=512):
    tk = min(tk, t_len)
    nq = t_len // tq
    assert t_len % tk == 0 and tk % tq == 0
    width = FOX_PAIRS_PER_STEP * LANES
    assert GW % width == 0
    return pl.pallas_call(
        functools.partial(_fox_attn_body, tq=tq, tk=tk),
        grid=(bsz, GW // width, nq),
        in_specs=[
            pl.BlockSpec((tq, width), lambda b, hq, qi: (b * nq + qi, C_FQ // width + hq)),
            pl.BlockSpec((t_len, width), lambda b, hq, qi: (b, C_FK // width + hq)),
            pl.BlockSpec((t_len, width), lambda b, hq, qi: (b, C_FV // width + hq)),
            pl.BlockSpec((None, FOX_HEADS, t_len), lambda b, hq, qi: (b, 0, 0)),
        ],
        out_specs=pl.BlockSpec((tq, width), lambda b, hq, qi: (b * nq + qi, hq)),
        out_shape=jax.ShapeDtypeStruct((bsz * t_len, GW), F32),
        compiler_params=_cparams(("parallel", "parallel", "arbitrary")),
        name="fox_attn",
    )(p, p, p, cumt)


def _compress_rows(xk_ref, xv_ref, pe_ref, w_ref, nbc):
    half = nbc // 2
    acc = jnp.zeros((nbc, 2 * LANES), F32)
    for r in range(CMP_BLOCK):
        ev = pl.ds(r, half, stride=2 * CMP_BLOCK)
        od = pl.ds(CMP_BLOCK + r, half, stride=2 * CMP_BLOCK)
        rows = jnp.concatenate([jnp.concatenate([xk_ref[ev, :], xv_ref[ev, :]], axis=1),
                                jnp.concatenate([xk_ref[od, :], xv_ref[od, :]], axis=1)], axis=0)
        acc = acc + _dot((rows + pe_ref[r:r + 1, :]).astype(BF16), w_ref[r])
    return acc


def _nsa_cmp_p_body(xk_ref, xv_ref, pe_ref, w_ref, o_ref, *, nbc, hp):
    half = nbc // 2
    acc = _compress_rows(xk_ref, xv_ref, pe_ref, w_ref, nbc)
    o_ref[...] = jnp.zeros_like(o_ref)
    o_ref[0:half, :] = acc[0:half]
    o_ref[hp:hp + half, :] = acc[half:nbc]


def nsa_cmp_p_call(p, pe4, w4, *, bsz, t_len):
    nbc = t_len // CMP_BLOCK
    hp = max(nbc // 2, HEAD_DIM)
    return pl.pallas_call(
        functools.partial(_nsa_cmp_p_body, nbc=nbc, hp=hp),
        grid=(bsz,),
        in_specs=[pl.BlockSpec((t_len, LANES), lambda b: (b, C_NKV // LANES)),
                  pl.BlockSpec((t_len, LANES), lambda b: (b, C_NKV // LANES + 1)),
                  pl.BlockSpec((CMP_BLOCK, 2 * LANES), lambda b: (0, 0)),
                  pl.BlockSpec((CMP_BLOCK, 2 * LANES, 2 * LANES), lambda b: (0, 0, 0))],
        out_specs=pl.BlockSpec((None, 2 * hp, 2 * LANES), lambda b: (b, 0, 0)),
        out_shape=jax.ShapeDtypeStruct((bsz, 2 * hp, 2 * LANES), F32),
        compiler_params=_cparams(("parallel",)),
        name="nsa_cmp_p",
    )(p, p, pe4, w4)


def _cmp_attend(qs, cmp_ref, g, t_pos, blk, slot_ok):
    scale = HEAD_DIM ** -0.5
    kc = cmp_ref[:, g * HEAD_DIM:(g + 1) * HEAD_DIM].astype(BF16)
    vc = cmp_ref[:, 2 * HEAD_DIM + g * HEAD_DIM:2 * HEAD_DIM + (g + 1) * HEAD_DIM].astype(BF16)
    s = _dot_nt(qs, kc) * scale
    valid = slot_ok & ((blk + 1) * CMP_BLOCK - 1 <= t_pos)
    s = jnp.where(valid, s, -jnp.inf)
    m = jnp.max(s, axis=1, keepdims=True)
    m = jnp.where(m > -jnp.inf, m, 0.0)
    e = jnp.where(valid, jnp.exp(s - m), 0.0)
    pc = e / jnp.maximum(jnp.sum(e, axis=1, keepdims=True), TINY)
    return _dot(pc.astype(BF16), vc), pc


def _select_score(imp, t_pos, blk):
    cur = t_pos // SLC_BLOCK
    forced = (blk == 0) | (blk == cur) | (blk == cur - 1)
    avail = blk * SLC_BLOCK <= t_pos
    return jnp.where(avail, jnp.where(forced, FORCE_SCORE, imp), -1.0)


def _nsa_attn_p_body(q_ref, cmp_ref, ks_ref, vs_ref, kw_ref, vw_ref, gl_ref, o_ref, *, tq, tk, nbc, hp):
    qi = pl.program_id(1)
    hg_n = NSA_GROUP
    rows = hg_n * tq
    half = nbc // 2
    nbs = (nbc + 1) // 2
    q0 = qi * tq
    scale = HEAD_DIM ** -0.5
    t_col = q0 + lax.broadcasted_iota(jnp.int32, (tq, 1), 0)
    t_lane = q0 + lax.broadcasted_iota(jnp.int32, (1, tq), 1)
    t_lane_stack = jnp.concatenate([t_lane] * hg_n, axis=1)
    gate = _sigmoid(gl_ref[...])
    crow = lax.broadcasted_iota(jnp.int32, (2 * hp, 1), 0)
    slot = jnp.where(crow < hp, crow, crow - hp)
    cmp_blk = 2 * slot + jnp.where(crow < hp, 0, 1)
    slc_blk = lax.broadcasted_iota(jnp.int32, (hp, 1), 0)

    def add_bias(s, bias):
        n = s.shape[1]
        return (s.reshape(hg_n, tq, n) + bias[None]).reshape(rows, n)

    per_g = []
    for g in range(NSA_KV_HEADS):
        q_f32 = jnp.concatenate(
            [q_ref[:, (g * hg_n + hg) * HEAD_DIM:(g * hg_n + hg + 1) * HEAD_DIM] for hg in range(hg_n)], axis=0)
        qs = q_f32.astype(BF16)
        qs2 = (q_f32 * (scale * LOG2E)).astype(BF16)

        kc = cmp_ref[:, g * HEAD_DIM:(g + 1) * HEAD_DIM].astype(BF16)
        vc = cmp_ref[:, 2 * HEAD_DIM + g * HEAD_DIM:2 * HEAD_DIM + (g + 1) * HEAD_DIM].astype(BF16)
        st = _dot_nt(kc, qs) * scale
        valid = (slot < half) & ((cmp_blk + 1) * CMP_BLOCK - 1 <= t_lane_stack)
        st = jnp.where(valid, st, -jnp.inf)
        m = jnp.max(st, axis=0, keepdims=True)
        m = jnp.where(m > -jnp.inf, m, 0.0)
        e = jnp.where(valid, jnp.exp(st - m), 0.0)
        pt = e / jnp.maximum(jnp.sum(e, axis=0, keepdims=True), TINY)
        o_cmp = _dot_tn(pt.astype(BF16), vc)

        imp = pt[:, 0:tq]
        for hg in range(1, hg_n):
            imp = imp + pt[:, hg * tq:(hg + 1) * tq]
        imp = imp[0:hp] + imp[hp:2 * hp]
        score = _select_score(imp, t_lane, slc_blk)
        rank = jnp.zeros((hp, tq), jnp.int32)
        for i in range(nbs):
            row = score[i:i + 1, :]
            rank = rank + ((row > score) | ((row == score) & (i < slc_blk))).astype(jnp.int32)
        sel_t = ((rank < TOP_N) & (score >= 0.0)).astype(BF16)

        per_g.append((qs2, o_cmp, sel_t))

    def slc_chunk(kc_i, carry):
        ks = pl.ds(pl.multiple_of(kc_i * tk, tk), tk)
        jb = lax.broadcasted_iota(jnp.int32, (hp, tk), 0)
        sp = kc_i * tk + lax.broadcasted_iota(jnp.int32, (hp, tk), 1)
        expand = (jb == sp // SLC_BLOCK).astype(BF16)
        causal = kc_i * tk + lax.broadcasted_iota(jnp.int32, (tq, tk), 1) <= t_col
        out = []
        for g in range(NSA_KV_HEADS):
            qs2, _, sel_t = per_g[g]
            m, l, acc = carry[g]
            hit = _dot_tn(sel_t, expand) > 0.5
            bias = jnp.where(hit & causal, 0.0, NEG)
            s = add_bias(_dot_nt(qs2, ks_ref[ks, g * HEAD_DIM:(g + 1) * HEAD_DIM].astype(BF16)), bias)
            m_new = jnp.maximum(m, jnp.max(s, axis=1, keepdims=True))
            alpha = jnp.exp2(m - m_new)
            pr = jnp.exp2(s - m_new)
            l = alpha * l + jnp.sum(pr, axis=1, keepdims=True)
            acc = alpha * acc + _dot(pr.astype(BF16), vs_ref[ks, g * HEAD_DIM:(g + 1) * HEAD_DIM].astype(BF16))
            out.append((m_new, l, acc))
        return tuple(out)

    init = tuple((jnp.full((rows, 1), NEG, F32), jnp.zeros((rows, 1), F32), jnp.zeros((rows, HEAD_DIM), F32))
                 for _ in range(NSA_KV_HEADS))
    slc = lax.fori_loop(0, (q0 + tq + tk - 1) // tk, slc_chunk, init)

    span = WINDOW + tq
    start = pl.multiple_of(jnp.maximum(q0 - WINDOW, 0), tq)
    ws = pl.ds(start, span)
    diff = t_col - (start + lax.broadcasted_iota(jnp.int32, (tq, span), 1))
    win_bias = jnp.where((diff >= 0) & (diff < WINDOW), 0.0, NEG)
    for g in range(NSA_KV_HEADS):
        qs2, o_cmp, _ = per_g[g]
        _, l, acc = slc[g]
        o_slc = acc / l
        s = add_bias(_dot_nt(qs2, kw_ref[ws, g * HEAD_DIM:(g + 1) * HEAD_DIM].astype(BF16)), win_bias)
        pr = jnp.exp2(s - jnp.max(s, axis=1, keepdims=True))
        o_win = (_dot(pr.astype(BF16), vw_ref[ws, g * HEAD_DIM:(g + 1) * HEAD_DIM].astype(BF16))
                 / jnp.sum(pr, axis=1, keepdims=True))
        for hg in range(hg_n):
            h = g * hg_n + hg
            rs = slice(hg * tq, (hg + 1) * tq)
            o = (gate[:, 3 * h:3 * h + 1] * o_cmp[rs] + gate[:, 3 * h + 1:3 * h + 2] * o_slc[rs]
                 + gate[:, 3 * h + 2:3 * h + 3] * o_win[rs])
            o_ref[:, h * HEAD_DIM:(h + 1) * HEAD_DIM] = o


def nsa_attn_p_call(p, cmp, *, bsz, t_len, tq=128, tk=512):
    nq = t_len // tq
    nbc = t_len // CMP_BLOCK
    hp = cmp.shape[1] // 2
    assert t_len % tk == 0 and WINDOW % tq == 0 and t_len >= WINDOW + tq
    kv = lambda off: pl.BlockSpec((t_len, LANES), lambda b, qi: (b, (C_NKV + off) // LANES))
    return pl.pallas_call(
        functools.partial(_nsa_attn_p_body, tq=tq, tk=tk, nbc=nbc, hp=hp),
        grid=(bsz, nq),
        in_specs=[
            pl.BlockSpec((tq, GW), lambda b, qi: (b * nq + qi, C_NQ // GW)),
            pl.BlockSpec((None, 2 * hp, 2 * LANES), lambda b, qi: (b, 0, 0)),
            kv(2 * LANES), kv(3 * LANES), kv(4 * LANES), kv(5 * LANES),
            pl.BlockSpec((tq, LANES), lambda b, qi: (b * nq + qi, C_NG // LANES)),
        ],
        out_specs=pl.BlockSpec((tq, GW), lambda b, qi: (b * nq + qi, 0)),
        out_shape=jax.ShapeDtypeStruct((bsz * t_len, GW), F32),
        compiler_params=_cparams(("parallel", "arbitrary")),
        name="nsa_attn_p",
    )(p, cmp, p, p, p, p, p)


def nsa_cmp_params(pe, cw):
    pe4 = jnp.concatenate([pe[0], pe[0], pe[1], pe[1]], axis=1)
    wk = cw[0].reshape(CMP_BLOCK, HEAD_DIM, HEAD_DIM)
    wv = cw[1].reshape(CMP_BLOCK, HEAD_DIM, HEAD_DIM)
    z = jnp.zeros_like(wk)
    rows = [jnp.concatenate([m if i == j else z for j in range(4)], axis=2) for i, m in enumerate((wk, wk, wv, wv))]
    return pe4, jnp.concatenate(rows, axis=1).astype(BF16)


def nsa_cmp_params_t(pe, cw):
    per_page = PAGE // CMP_BLOCK
    eye = jnp.eye(per_page, dtype=F32)

    def mat(w):
        wd = jnp.transpose(w.reshape(CMP_BLOCK, HEAD_DIM, HEAD_DIM), (1, 0, 2))
        return jnp.einsum('dre,mn->dmrne', wd, eye).reshape(HEAD_DIM, PAGE, per_page * HEAD_DIM).astype(BF16)

    bias = lambda x: jnp.tile(x.T, (1, per_page))
    return bias(pe[0]), bias(pe[1]), mat(cw[0]), mat(cw[1])


T_PAD = SUBLANES
FOX_LF_PAGES = 16
FOX_KV_PAGES = 16
NSA_CMP_PAGES = 16


def _page_spec(block, layer, pages_per_step, i, tail, first_step=0):
    def index_map(b, s, pt):
        j = jnp.maximum(s - first_step, 0) * pages_per_step + i
        return (pt[b, j], layer) + tail
    return pl.BlockSpec(block, index_map)


def _fox_prep_s_body(pt_ref, *refs, n_pages):
    pg = FOX_LF_PAGES
    page_refs, o_ref, a_ref = refs[:pg], refs[pg], refs[pg + 1]
    s = pl.program_id(1)
    for i in range(pg):
        a_ref[pl.ds(pl.multiple_of((s * pg + i) * FOX_HEADS, FOX_HEADS), FOX_HEADS), :] = page_refs[i][...]

    @pl.when(s == pl.num_programs(1) - 1)
    def _():
        n = n_pages * FOX_HEADS
        a = a_ref[...]
        ji = lax.broadcasted_iota(jnp.int32, (PAGE, PAGE), 0)
        si = lax.broadcasted_iota(jnp.int32, (PAGE, PAGE), 1)
        within = jnp.dot(a, (ji > si).astype(F32), preferred_element_type=F32, precision=HIGHEST)
        tot = jnp.broadcast_to(jnp.sum(a, axis=1, keepdims=True), (n, LANES))
        ri = lax.broadcasted_iota(jnp.int32, (n, n), 0)
        ci = lax.broadcasted_iota(jnp.int32, (n, n), 1)
        later = ((ci > ri) & ((ci - ri) % FOX_HEADS == 0)).astype(F32)
        o_ref[...] = within + jnp.dot(later, tot, preferred_element_type=F32, precision=HIGHEST)


def fox_prep_s_call(page_table, cache_lf, *, layer, bsz, n_pages):
    pg = FOX_LF_PAGES
    assert n_pages % pg == 0
    n = n_pages * FOX_HEADS
    grid_spec = pltpu.PrefetchScalarGridSpec(
        num_scalar_prefetch=1,
        grid=(bsz, n_pages // pg),
        in_specs=[_page_spec((None, None, FOX_HEADS, PAGE), layer, pg, i, (0, 0)) for i in range(pg)],
        out_specs=pl.BlockSpec((None, n, LANES), lambda b, s, pt: (b, 0, 0)),
        scratch_shapes=[pltpu.VMEM((n, LANES), F32)],
    )
    return pl.pallas_call(
        functools.partial(_fox_prep_s_body, n_pages=n_pages),
        grid_spec=grid_spec,
        out_shape=jax.ShapeDtypeStruct((bsz, n, LANES), F32),
        compiler_params=_cparams(("parallel", "arbitrary")),
        name="fox_prep_s",
    )(page_table, *([cache_lf] * pg))


def _fox_attn_s_body(pt_ref, q_ref, k_ref, v_ref, f_ref, b_ref, rp_ref, *refs, n_real):
    pg = FOX_KV_PAGES
    page_refs = refs[:pg]
    o_ref, lf_ref, qbd_ref, rqp_ref, m_ref, l_ref, acc_ref = refs[pg:]
    s = pl.program_id(1)
    scale = HEAD_DIM ** -0.5
    nh = FOX_HEADS
    rows = nh * T_PAD
    row_h = lax.broadcasted_iota(jnp.int32, (rows, 1), 0) // T_PAD
    row_t = lax.broadcasted_iota(jnp.int32, (rows, 1), 0) % T_PAD

    def rep_heads(x8):
        return jnp.concatenate([jnp.broadcast_to(x8[h:h + 1, :], (T_PAD, x8.shape[1])) for h in range(nh)], axis=0)

    def attend(scores, values, m, l, acc):
        tile_max = scores[0]
        for sc in scores[1:]:
            tile_max = jnp.maximum(tile_max, sc)
        m_new = jnp.maximum(m, jnp.max(tile_max, axis=1, keepdims=True))
        alpha = jnp.exp(m - m_new)
        acc = alpha * acc
        psum = None
        for sc, pv in zip(scores, values):
            pr = jnp.exp(sc - m_new)
            psum = pr if psum is None else psum + pr
            acc = acc + pv(pr.astype(BF16))
        return m_new, alpha * l + jnp.sum(psum, axis=1, keepdims=True), acc

    @pl.when(s == 0)
    def _():
        lf = _log_sigmoid(f_ref[...] + b_ref[...])
        lf_ref[...] = lf
        tok = lax.broadcasted_iota(jnp.int32, (T_PAD, LANES), 0)
        lfm = jnp.where(tok < n_real, lf, 0.0)
        r_new = jnp.zeros((T_PAD, LANES), F32)
        for j in range(1, n_real):
            r_new = r_new + jnp.where(tok < j, lfm[j:j + 1, :], 0.0)
        tot = jnp.sum(lfm, axis=0, keepdims=True)
        lane = lax.broadcasted_iota(jnp.int32, (rows, LANES), 1)
        pick = lane == row_h
        r_q = jnp.sum(jnp.where(pick, jnp.concatenate([r_new] * nh, axis=0), 0.0), axis=1, keepdims=True)
        t_q = jnp.sum(jnp.where(pick, jnp.broadcast_to(tot, (rows, LANES)), 0.0), axis=1, keepdims=True)
        rqp_ref[...] = r_q - t_q
        col = lax.broadcasted_iota(jnp.int32, (rows, GW), 1)
        qbd = jnp.where(col // HEAD_DIM == row_h, jnp.concatenate([q_ref[...]] * nh, axis=0), 0.0)
        qbd_ref[...] = qbd.astype(BF16)
        zrow = jnp.zeros((PAGE - T_PAD, GW), F32)
        kk = jnp.concatenate([k_ref[...], zrow], axis=0)
        vv = jnp.concatenate([v_ref[...], zrow], axis=0)
        r_pad = jnp.concatenate([r_new, jnp.zeros((PAGE - T_PAD, LANES), F32)], axis=0)
        bias = rep_heads(r_pad.T[:nh, :]) - r_q
        key = lax.broadcasted_iota(jnp.int32, (rows, PAGE), 1)
        mask = (key <= row_t) & (key < n_real)
        sc = jnp.where(mask, _dot_nt(qbd.astype(BF16), kk.astype(BF16)) * scale + bias, -jnp.inf)
        m0 = jnp.full((rows, 1), -jnp.inf, F32)
        m, l, acc = attend([sc], [lambda pr: _dot(pr, vv.astype(BF16))], m0,
                           jnp.zeros((rows, 1), F32), jnp.zeros((rows, GW), F32))
        m_ref[...] = m
        l_ref[...] = l
        acc_ref[...] = acc

    @pl.when(s > 0)
    def _():
        rqp = rqp_ref[...]
        qbd = qbd_ref[...]
        scores = [_dot(qbd, page_refs[i][0:GW, :].astype(BF16)) * scale
                  + (rep_heads(rp_ref[i * nh:(i + 1) * nh, :]) - rqp) for i in range(pg)]
        values = [lambda pr, i=i: _dot_nt(pr, page_refs[i][GW:2 * GW, :].astype(BF16)) for i in range(pg)]
        m, l, acc = attend(scores, values, m_ref[...], l_ref[...], acc_ref[...])
        m_ref[...] = m
        l_ref[...] = l
        acc_ref[...] = acc

    @pl.when(s == pl.num_programs(1) - 1)
    def _():
        col = lax.broadcasted_iota(jnp.int32, (rows, GW), 1)
        o = jnp.where(col // HEAD_DIM == row_h, acc_ref[...] / l_ref[...], 0.0)
        o_ref[...] = jnp.sum(o.reshape(nh, T_PAD, GW), axis=0)


def fox_attn_s_call(page_table, p, bias, r_past, cache_kv, *, layer, bsz, n_pages, n_real):
    pg = FOX_KV_PAGES
    assert n_pages % pg == 0
    rows = FOX_HEADS * T_PAD
    new = lambda width, off: pl.BlockSpec((T_PAD, width), lambda b, s, pt: (b, off // width))
    grid_spec = pltpu.PrefetchScalarGridSpec(
        num_scalar_prefetch=1,
        grid=(bsz, 1 + n_pages // pg),
        in_specs=[new(GW, C_FQ), new(GW, C_FK), new(GW, C_FV), new(LANES, C_FF),
                  pl.BlockSpec((1, LANES), lambda b, s, pt: (0, 0)),
                  pl.BlockSpec((None, pg * FOX_HEADS, LANES), lambda b, s, pt: (b, jnp.maximum(s - 1, 0), 0))]
                 + [_page_spec((None, None, 2 * GW, PAGE), layer, pg, i, (0, 0), first_step=1) for i in range(pg)],
        out_specs=[pl.BlockSpec((T_PAD, GW), lambda b, s, pt: (b, 0)),
                   pl.BlockSpec((T_PAD, LANES), lambda b, s, pt: (b, 0))],
        scratch_shapes=[pltpu.VMEM((rows, GW), BF16), pltpu.VMEM((rows, 1), F32), pltpu.VMEM((rows, 1), F32),
                        pltpu.VMEM((rows, 1), F32), pltpu.VMEM((rows, GW), F32)],
    )
    return pl.pallas_call(
        functools.partial(_fox_attn_s_body, n_real=n_real),
        grid_spec=grid_spec,
        out_shape=[jax.ShapeDtypeStruct((bsz * T_PAD, GW), F32), jax.ShapeDtypeStruct((bsz * T_PAD, LANES), F32)],
        compiler_params=_cparams(("parallel", "arbitrary")),
        name="fox_attn_s",
    )(page_table, p, p, p, p, bias, r_past, *([cache_kv] * pg))


def _nsa_cmp_s_body(pt_ref, *refs, n_pages):
    pg = NSA_CMP_PAGES
    k_pages, v_pages = refs[:pg], refs[pg:2 * pg]
    pek_ref, pev_ref, mk_ref, mv_ref, o_ref, xk_ref, xv_ref = refs[2 * pg:]
    s = pl.program_id(1)
    for i in range(pg):
        rows = pl.ds(pl.multiple_of((s * pg + i) * PAGE, PAGE), PAGE)
        xk_ref[rows, :] = k_pages[i][...]
        xv_ref[rows, :] = v_pages[i][...]

    @pl.when(s == pl.num_programs(1) - 1)
    def _():
        per_page = PAGE // CMP_BLOCK
        width = per_page * HEAD_DIM
        for x_ref, pe_ref, m_ref, off in ((xk_ref, pek_ref, mk_ref, 0), (xv_ref, pev_ref, mv_ref, 2 * HEAD_DIM)):
            acc = jnp.zeros((NSA_KV_HEADS * n_pages, width), F32)
            for d in range(HEAD_DIM):
                rows = jnp.concatenate([x_ref[pl.ds(g * HEAD_DIM + d, n_pages, stride=PAGE), :]
                                        for g in range(NSA_KV_HEADS)], axis=0)
                acc = acc + _dot((rows + pe_ref[d:d + 1, :]).astype(BF16), m_ref[d])
            for g in range(NSA_KV_HEADS):
                for n in range(per_page):
                    o_ref[n * n_pages:(n + 1) * n_pages, off + g * HEAD_DIM:off + (g + 1) * HEAD_DIM] = (
                        acc[g * n_pages:(g + 1) * n_pages, n * HEAD_DIM:(n + 1) * HEAD_DIM])


def nsa_cmp_s_call(page_table, cache_kv, pek, pev, mk, mv, *, layer, bsz, n_pages):
    pg = NSA_CMP_PAGES
    assert n_pages % pg == 0
    per_page = PAGE // CMP_BLOCK
    nbc = n_pages * per_page
    const = lambda shape: pl.BlockSpec(shape, lambda b, s, pt: (0,) * len(shape))
    grid_spec = pltpu.PrefetchScalarGridSpec(
        num_scalar_prefetch=1,
        grid=(bsz, n_pages // pg),
        in_specs=[_page_spec((None, None, PAGE, PAGE), layer, pg, i, (0, 0)) for i in range(pg)]
                 + [_page_spec((None, None, PAGE, PAGE), layer, pg, i, (1, 0)) for i in range(pg)]
                 + [const((HEAD_DIM, PAGE)), const((HEAD_DIM, PAGE)),
                    const((HEAD_DIM, PAGE, per_page * HEAD_DIM)), const((HEAD_DIM, PAGE, per_page * HEAD_DIM))],
        out_specs=pl.BlockSpec((None, nbc, 2 * LANES), lambda b, s, pt: (b, 0, 0)),
        scratch_shapes=[pltpu.VMEM((n_pages * PAGE, PAGE), F32), pltpu.VMEM((n_pages * PAGE, PAGE), F32)],
    )
    return pl.pallas_call(
        functools.partial(_nsa_cmp_s_body, n_pages=n_pages),
        grid_spec=grid_spec,
        out_shape=jax.ShapeDtypeStruct((bsz, nbc, 2 * LANES), F32),
        compiler_params=_cparams(("parallel", "arbitrary")),
        name="nsa_cmp_s",
    )(page_table, *([cache_kv] * (2 * pg)), pek, pev, mk, mv)


def _stack_heads(q_ref, g):
    hg_n = NSA_GROUP
    return jnp.concatenate(
        [q_ref[:, (g * hg_n + hg) * HEAD_DIM:(g * hg_n + hg + 1) * HEAD_DIM] for hg in range(hg_n)], axis=0)


def _nsa_topk_s_body(q_ref, cmp_ref, o_ref, idx_ref, *, q_off, nbc):
    hg_n = NSA_GROUP
    per_page = PAGE // CMP_BLOCK
    n_pages = nbc // per_page
    w = 2 * n_pages + LANES
    t_col = q_off + lax.broadcasted_iota(jnp.int32, (T_PAD, 1), 0)
    t_stack = jnp.concatenate([t_col] * hg_n, axis=0)
    c = lax.broadcasted_iota(jnp.int32, (1, nbc), 1)
    cmp_blk = per_page * (c % n_pages) + c // n_pages

    def slc_blk(i):
        return jnp.where(i < n_pages, 2 * i, jnp.where(i < 2 * n_pages, 2 * (i - n_pages) + 1, i))

    blk_i = slc_blk(lax.broadcasted_iota(jnp.int32, (w, w), 0))
    blk_j = slc_blk(lax.broadcasted_iota(jnp.int32, (w, w), 1))
    blk_row = slc_blk(lax.broadcasted_iota(jnp.int32, (1, w), 1))
    rr = lax.broadcasted_iota(jnp.int32, (TOP_N, w), 0)
    lane = lax.broadcasted_iota(jnp.int32, (TOP_N, LANES), 1)
    for g in range(NSA_KV_HEADS):
        qs = _stack_heads(q_ref, g).astype(BF16)
        o_cmp, pc = _cmp_attend(qs, cmp_ref, g, t_stack, cmp_blk, cmp_blk >= 0)
        for hg in range(hg_n):
            h = g * hg_n + hg
            o_ref[:, h * HEAD_DIM:(h + 1) * HEAD_DIM] = o_cmp[hg * T_PAD:(hg + 1) * T_PAD]
        imp = pc[0:T_PAD]
        for hg in range(1, hg_n):
            imp = imp + pc[hg * T_PAD:(hg + 1) * T_PAD]
        imp = jnp.concatenate([imp[:, 0:n_pages] + imp[:, n_pages:2 * n_pages],
                               imp[:, 2 * n_pages:3 * n_pages] + imp[:, 3 * n_pages:4 * n_pages],
                               jnp.zeros((T_PAD, LANES), F32)], axis=1)
        score = _select_score(imp, t_col, blk_row)
        score_t = jnp.concatenate([score, jnp.zeros((LANES - T_PAD, w), F32)], axis=0).T
        out = jnp.zeros((TOP_N, LANES), F32)
        for t in range(T_PAD):
            col = score_t[:, t:t + 1]
            row = score[t:t + 1, :]
            ahead = (col > row) | ((col == row) & (blk_i < blk_j))
            rank = jnp.sum(ahead.astype(F32), axis=0, keepdims=True)
            hit = (rank == rr.astype(F32)) & (row >= 0.0)
            found = jnp.sum(hit.astype(F32), axis=1, keepdims=True)
            which = jnp.sum(jnp.where(hit, blk_row.astype(F32), 0.0), axis=1, keepdims=True)
            out = jnp.where(lane == t, jnp.where(found > 0.5, which, -1.0), out)
        idx_ref[g] = out.astype(jnp.int32)


def nsa_topk_s_call(p, cmp, *, bsz, q_off):
    nbc = cmp.shape[1]
    return pl.pallas_call(
        functools.partial(_nsa_topk_s_body, q_off=q_off, nbc=nbc),
        grid=(bsz,),
        in_specs=[pl.BlockSpec((T_PAD, GW), lambda b: (b, C_NQ // GW)),
                  pl.BlockSpec((None, nbc, 2 * LANES), lambda b: (b, 0, 0))],
        out_specs=[pl.BlockSpec((T_PAD, GW), lambda b: (b, 0)),
                   pl.BlockSpec((None, NSA_KV_HEADS, TOP_N, LANES), lambda b: (b, 0, 0, 0))],
        out_shape=[jax.ShapeDtypeStruct((bsz * T_PAD, GW), F32),
                   jax.ShapeDtypeStruct((bsz, NSA_KV_HEADS, TOP_N, LANES), jnp.int32)],
        compiler_params=_cparams(("parallel",)),
        name="nsa_topk_s",
    )(p, cmp)


def _nsa_sel_s_body(pt_ref, idx_ref, q_ref, kn_ref, vn_ref, *refs, n_real, n_tok, q_off):
    k_blocks, v_blocks, o_ref = refs[:TOP_N], refs[TOP_N:2 * TOP_N], refs[2 * TOP_N]
    b, g, t = pl.program_id(0), pl.program_id(1), pl.program_id(2)
    scale = HEAD_DIM ** -0.5
    hg_n = NSA_GROUP
    base = ((b * NSA_KV_HEADS + g) * n_tok + t) * TOP_N
    new_blk = q_off // SLC_BLOCK
    qrow = q_ref[pl.ds(t, 1), :]
    qs = jnp.concatenate([qrow[:, hg * HEAD_DIM:(hg + 1) * HEAD_DIM] for hg in range(hg_n)]
                         + [jnp.zeros((SUBLANES - hg_n, HEAD_DIM), F32)], axis=0).astype(BF16)

    def pick(blk):
        return jnp.where(g == 0, blk[:, 0:HEAD_DIM], blk[:, HEAD_DIM:2 * HEAD_DIM])

    lane_half = lax.broadcasted_iota(jnp.int32, (SUBLANES, PAGE), 1) // SLC_BLOCK
    logits, values = [], []
    has_new = jnp.bool_(False)
    for r in range(TOP_N):
        j = idx_ref[base + r]
        from_cache = (j >= 0) & (j < new_blk)
        has_new = has_new | (j == new_blk)
        sc = _dot(qs, k_blocks[r][...].astype(BF16)) * scale
        logits.append(jnp.where(from_cache & (lane_half == j % 2), sc, -jnp.inf))
        values.append(lambda pr, r=r: _dot_nt(pr, v_blocks[r][...].astype(BF16)))
    sc = _dot_nt(qs, pick(kn_ref[...]).astype(BF16)) * scale
    key = lax.broadcasted_iota(jnp.int32, (SUBLANES, T_PAD), 1)
    logits.append(jnp.where(has_new & (key <= t) & (key < n_real), sc, -jnp.inf))
    values.append(lambda pr: _dot(pr, pick(vn_ref[...]).astype(BF16)))
    m = logits[0].max(axis=1, keepdims=True)
    for x in logits[1:]:
        m = jnp.maximum(m, x.max(axis=1, keepdims=True))
    l = jnp.zeros((SUBLANES, 1), F32)
    acc = jnp.zeros((SUBLANES, HEAD_DIM), F32)
    for x, pv in zip(logits, values):
        pr = jnp.exp(x - m)
        l = l + jnp.sum(pr, axis=1, keepdims=True)
        acc = acc + pv(pr.astype(BF16))
    o_ref[...] = acc / l


def nsa_sel_s_call(page_table, idx, p, cache_kv, *, layer, bsz, n_tok, n_real, q_off):
    n_cached = q_off // SLC_BLOCK
    blocks_per_page = PAGE // SLC_BLOCK

    def blk_spec(r, kind):
        def index_map(b, g, t, pt, ix):
            j = jnp.clip(ix[((b * NSA_KV_HEADS + g) * n_tok + t) * TOP_N + r], 0, n_cached - 1)
            return (pt[b, j // blocks_per_page], layer, kind * NSA_KV_HEADS + g, 0)
        return pl.BlockSpec((None, None, HEAD_DIM, PAGE), index_map)

    grid_spec = pltpu.PrefetchScalarGridSpec(
        num_scalar_prefetch=2,
        grid=(bsz, NSA_KV_HEADS, n_tok),
        in_specs=[pl.BlockSpec((T_PAD, 2 * LANES), lambda b, g, t, pt, ix: (b, C_NQ // (2 * LANES) + g)),
                  pl.BlockSpec((T_PAD, LANES), lambda b, g, t, pt, ix: (b, C_NKV // LANES + 2)),
                  pl.BlockSpec((T_PAD, LANES), lambda b, g, t, pt, ix: (b, C_NKV // LANES + 3))]
                 + [blk_spec(r, 2) for r in range(TOP_N)] + [blk_spec(r, 3) for r in range(TOP_N)],
        out_specs=pl.BlockSpec((None, None, None, SUBLANES, HEAD_DIM), lambda b, g, t, pt, ix: (b, g, t, 0, 0)),
    )
    return pl.pallas_call(
        functools.partial(_nsa_sel_s_body, n_real=n_real, n_tok=n_tok, q_off=q_off),
        grid_spec=grid_spec,
        out_shape=jax.ShapeDtypeStruct((bsz, NSA_KV_HEADS, n_tok, SUBLANES, HEAD_DIM), F32),
        compiler_params=_cparams(("parallel", "arbitrary", "arbitrary")),
        name="nsa_sel_s",
    )(page_table, idx, p, p, p, *([cache_kv] * (2 * TOP_N)))


def _nsa_win_s_body(q_ref, gl_ref, oc_ref, os_ref, win_ref, new_ref, o_ref, *, n_real, win_len):
    hg_n = NSA_GROUP
    rows = hg_n * T_PAD
    scale = HEAD_DIM ** -0.5
    gate = _sigmoid(gl_ref[...])
    t_row = lax.broadcasted_iota(jnp.int32, (rows, 1), 0) % T_PAD
    key = lax.broadcasted_iota(jnp.int32, (rows, win_len), 1)
    mask_c = key + WINDOW > t_row + win_len
    new_i = lax.broadcasted_iota(jnp.int32, (rows, PAGE), 1)
    mask_n = (new_i <= t_row) & (new_i < n_real)
    zpad = jnp.zeros((PAGE - T_PAD, HEAD_DIM), F32)
    for g in range(NSA_KV_HEADS):
        qs = _stack_heads(q_ref, g).astype(BF16)
        kt = win_ref[g * HEAD_DIM:(g + 1) * HEAD_DIM, :].astype(BF16)
        vt = win_ref[LANES + g * HEAD_DIM:LANES + (g + 1) * HEAD_DIM, :].astype(BF16)
        kn = jnp.concatenate([new_ref[:, g * HEAD_DIM:(g + 1) * HEAD_DIM], zpad], axis=0).astype(BF16)
        vn = jnp.concatenate([new_ref[:, LANES + g * HEAD_DIM:LANES + (g + 1) * HEAD_DIM], zpad], axis=0).astype(BF16)
        sc_c = jnp.where(mask_c, _dot(qs, kt) * scale, -jnp.inf)
        sc_n = jnp.where(mask_n, _dot_nt(qs, kn) * scale, -jnp.inf)
        m = jnp.maximum(jnp.max(sc_c, axis=1, keepdims=True), jnp.max(sc_n, axis=1, keepdims=True))
        pr_c = jnp.exp(sc_c - m)
        pr_n = jnp.exp(sc_n - m)
        l = jnp.sum(pr_c, axis=1, keepdims=True) + jnp.sum(pr_n, axis=1, keepdims=True)
        o_win = (_dot_nt(pr_c.astype(BF16), vt) + _dot(pr_n.astype(BF16), vn)) / l
        for hg in range(hg_n):
            h = g * hg_n + hg
            cs = slice(h * HEAD_DIM, (h + 1) * HEAD_DIM)
            o_ref[:, cs] = (gate[:, 3 * h:3 * h + 1] * oc_ref[:, cs] + gate[:, 3 * h + 1:3 * h + 2] * os_ref[:, cs]
                            + gate[:, 3 * h + 2:3 * h + 3] * o_win[hg * T_PAD:(hg + 1) * T_PAD])


def nsa_win_s_call(p, o_cmp, o_slc, cache_win, *, layer, bsz, n_real):
    win_len = cache_win.shape[3]
    assert win_len == WINDOW
    row = lambda width, off: pl.BlockSpec((T_PAD, width), lambda b: (b, off // width))
    return pl.pallas_call(
        functools.partial(_nsa_win_s_body, n_real=n_real, win_len=win_len),
        grid=(bsz,),
        in_specs=[row(GW, C_NQ), row(LANES, C_NG), row(GW, 0), row(GW, 0),
                  pl.BlockSpec((None, None, 2 * LANES, win_len), lambda b: (b, layer, 0, 0)),
                  row(2 * LANES, C_NKV + 4 * LANES)],
        out_specs=pl.BlockSpec((T_PAD, GW), lambda b: (b, 0)),
        out_shape=jax.ShapeDtypeStruct((bsz * T_PAD, GW), F32),
        compiler_params=_cparams(("parallel",)),
        name="nsa_win_s",
    )(p, p, o_cmp, o_slc, cache_win, p)


def _lane_pad(v):
    return jnp.pad(v, (0, LANES - v.shape[0]))[None, :]


def _layer_weights(w, l):
    win = w['w_in'][l]
    o_fox = GW + SSD_CONV_DIM + SSD_HEADS
    o_nsa = o_fox + 3 * GW + FOX_HEADS
    o_s5 = o_nsa + GW + 6 * NSA_KV_HEADS * HEAD_DIM + 3 * NSA_HEADS
    cols = lambda a, b: win[:, a:b]
    zpad = lambda n: jnp.zeros((D_MODEL, n), F32)
    w_in = jnp.concatenate([
        cols(GW, GW + SSD_CONV_DIM),
        cols(GW + SSD_CONV_DIM, o_fox), zpad(LANES - SSD_HEADS),
        cols(o_fox + 3 * GW, o_nsa), zpad(LANES - FOX_HEADS),
        cols(0, GW),
        cols(o_s5, o_s5 + GW),
        cols(o_fox, o_fox + 3 * GW),
        cols(o_nsa, o_nsa + GW),
        cols(o_nsa + GW, o_nsa + GW + 6 * LANES),
        cols(o_nsa + GW + 6 * LANES, o_s5), zpad(LANES - 3 * NSA_HEADS),
        zpad(P_W - C_NG - LANES),
    ], axis=1).astype(BF16)
    assert w_in.shape == (D_MODEL, P_W)
    pe4, w4 = nsa_cmp_params(w['nsa_cmp_pe'][l], w['nsa_cmp_w'][l])
    s5 = s5_params(w['s5_lambda_re'][l], w['s5_lambda_im'][l], w['s5_log_dt'][l],
                   w['s5_b_re'][l], w['s5_b_im'][l], w['s5_c_re'][l], w['s5_c_im'][l])
    return dict(
        ffn1=(w['ffn1_norm'][l][None, :], w['ffn1_w1_bf16'], w['ffn1_w3_bf16'], w['ffn1_w2_bf16']),
        ffn2=(w['ffn2_norm'][l][None, :], w['ffn2_w1_bf16'], w['ffn2_w3_bf16'], w['ffn2_w2_bf16']),
        mix_norm=w['mix_norm'][l][None, :], w_in=w_in,
        ssd=(w['ssd_conv_w'][l], w['ssd_conv_b'][l][None, :], _lane_pad(w['ssd_dt_bias'][l]),
             _lane_pad(w['ssd_a_log'][l]), _lane_pad(w['ssd_d'][l])),
        fox_bias=_lane_pad(w['fox_f_bias'][l]),
        pe4=pe4, w4=w4, cmp_t=nsa_cmp_params_t(w['nsa_cmp_pe'][l], w['nsa_cmp_w'][l]), s5=s5, s5_d=w['s5_d'][l][None, :], s5_glu=w['s5_w_glu'][l].astype(BF16),
        gains=jnp.stack([w['ssd_norm'][l], w['fox_out_norm'][l], w['nsa_out_norm'][l], w['s5_out_norm'][l]]),
        w_out=w['w_out_bf16'],
    )


def _time_major(x, bsz, t_len):
    return jnp.swapaxes(x.reshape(bsz, t_len, -1), 0, 1).reshape(t_len * bsz, -1)


def _batch_major(x, bsz, t_len):
    return jnp.swapaxes(x.reshape(t_len, bsz, -1), 0, 1).reshape(bsz * t_len, -1)


def _prompt_layer(x, lw, final_gain, *, layer, bsz, t_len, last):
    tm = 512
    x = ffn_call(x, *lw['ffn1'], final_gain, tm=tm, layer=layer, final_norm=False)
    p, fox_kv, nsa_kv, nsa_win, u = inproj_call(x, lw['mix_norm'], lw['w_in'], tm=tm, time_major=(bsz, t_len))
    conv0 = jnp.zeros((bsz, CONV_W - 1, SSD_CONV_DIM), F32)
    ssm0 = jnp.zeros((bsz, SSD_HEADS, HEAD_DIM, SSD_STATE), F32)
    y_ssd, ssm, conv = ssd_call(p, conv0, ssm0, *lw['ssd'], bsz=bsz, t_len=t_len, n_real=SSD_CHUNK)
    lf, cumt = fox_prep_call(p, lw['fox_bias'], bsz=bsz, t_len=t_len)
    y_fox = fox_attn_call(p, cumt, bsz=bsz, t_len=t_len)
    cmp = nsa_cmp_p_call(p, lw['pe4'], lw['w4'], bsz=bsz, t_len=t_len)
    y_nsa = nsa_attn_p_call(p, cmp, bsz=bsz, t_len=t_len)
    bre, bim, are, aim, cre, cim = lw['s5']
    x0 = jnp.zeros((SUBLANES, S5_N), F32)
    y_s5, xr, xi = s5_call(u.reshape(t_len * bsz, GW), bre, bim, are, aim, x0, x0, cre, cim, lw['s5_d'], lw['s5_glu'],
                           nb=bsz, t_len=t_len, steps=256, t_last=t_len - 1)
    x = outproj_call(x, y_ssd, y_fox, y_nsa, y_s5.reshape(t_len, bsz * GW), lw['gains'], lw['w_out'],
                     tm=tm, layer=layer, time_major=(bsz, t_len))
    x = ffn_call(x, *lw['ffn2'], final_gain, tm=tm, layer=layer, final_norm=last)
    keep = min(WINDOW, t_len)
    off = ((t_len - 1) % (SUBLANES // bsz)) * bsz
    states = (
        fox_kv.reshape(bsz, t_len, 2, FOX_HEADS, HEAD_DIM),
        lf.reshape(bsz, t_len, LANES)[:, :, :FOX_HEADS],
        nsa_kv.reshape(bsz, t_len, 4, NSA_KV_HEADS, HEAD_DIM),
        nsa_win.reshape(bsz, t_len, 2, NSA_KV_HEADS, HEAD_DIM)[:, t_len - keep:],
        ssm, conv,
        xr[off:off + bsz].reshape(bsz, S5_GROUPS, S5_STATE),
        xi[off:off + bsz].reshape(bsz, S5_GROUPS, S5_STATE),
    )
    return x, states


def _sample_layer(x, lw, final_gain, caches, page_table, *, layer, bsz, n_real, q_off, last):
    fox_kv_t, fox_lf_t, nsa_kv_t, nsa_win_t, cache_nsa_win, st_ssd, st_conv, st_re, st_im = caches
    tm = bsz * T_PAD
    n_pages = page_table.shape[1]
    assert q_off == n_pages * PAGE and q_off % SLC_BLOCK == 0 and bsz == SUBLANES
    x = ffn_call(x, *lw['ffn1'], final_gain, tm=tm, layer=layer, final_norm=False)
    p, fox_kv, nsa_kv, nsa_win = inproj_call(x, lw['mix_norm'], lw['w_in'], tm=tm)
    p3 = p.reshape(bsz, T_PAD, P_W)

    p_ssd = jnp.pad(p3[:, :, :C_U], ((0, 0), (0, SSD_CHUNK - T_PAD), (0, 0))).reshape(bsz * SSD_CHUNK, C_U)
    y_ssd, ssm, conv = ssd_call(p_ssd, st_conv[:, layer], st_ssd[:, layer], *lw['ssd'],
                                bsz=bsz, t_len=SSD_CHUNK, n_real=n_real)
    y_ssd = y_ssd.reshape(bsz, SSD_CHUNK, GW)[:, :T_PAD].reshape(tm, GW)

    r_past = fox_prep_s_call(page_table, fox_lf_t, layer=layer, bsz=bsz, n_pages=n_pages)
    y_fox, lf = fox_attn_s_call(page_table, p, lw['fox_bias'], r_past, fox_kv_t,
                                layer=layer, bsz=bsz, n_pages=n_pages, n_real=n_real)

    cmp = nsa_cmp_s_call(page_table, nsa_kv_t, *lw['cmp_t'], layer=layer, bsz=bsz, n_pages=n_pages)
    o_cmp, idx = nsa_topk_s_call(p, cmp, bsz=bsz, q_off=q_off)
    idx = jnp.swapaxes(idx[:, :, :, :n_real], 2, 3).reshape(-1)
    o_slc = nsa_sel_s_call(page_table, idx, p, nsa_kv_t,
                           layer=layer, bsz=bsz, n_tok=n_real, n_real=n_real, q_off=q_off)
    o_slc = jnp.transpose(o_slc[:, :, :, :NSA_GROUP], (0, 2, 1, 3, 4)).reshape(bsz, n_real, GW)
    o_slc = jnp.pad(o_slc, ((0, 0), (0, T_PAD - n_real), (0, 0))).reshape(tm, GW)
    y_nsa = nsa_win_s_call(p, o_cmp, o_slc, nsa_win_t, layer=layer, bsz=bsz, n_real=n_real)

    bre, bim, are, aim, cre, cim = lw['s5']
    u = _time_major(p[:, C_U:C_U + GW], bsz, T_PAD)
    y_s5, xr, xi = s5_call(u, bre, bim, are, aim, st_re[:, layer].reshape(bsz, S5_N), st_im[:, layer].reshape(bsz, S5_N),
                           cre, cim, lw['s5_d'], lw['s5_glu'], nb=bsz, t_len=T_PAD, steps=T_PAD, t_last=n_real - 1)
    y_s5 = _batch_major(y_s5, bsz, T_PAD)

    x = outproj_call(x, y_ssd, y_fox, y_nsa, y_s5, lw['gains'], lw['w_out'], tm=tm, layer=layer)
    x = ffn_call(x, *lw['ffn2'], final_gain, tm=tm, layer=layer, final_norm=last)
    real = lambda a: a.reshape(bsz, T_PAD, -1)[:, :n_real]
    win_rows = real(nsa_win).reshape(bsz, n_real, 2, NSA_KV_HEADS, HEAD_DIM)
    states = (
        real(fox_kv).reshape(bsz, n_real, 2, FOX_HEADS, HEAD_DIM),
        lf.reshape(bsz, T_PAD, LANES)[:, :n_real, :FOX_HEADS],
        real(nsa_kv).reshape(bsz, n_real, 4, NSA_KV_HEADS, HEAD_DIM),
        jnp.concatenate([cache_nsa_win[:, layer, n_real:], win_rows], axis=1),
        ssm, conv,
        xr.reshape(bsz, S5_GROUPS, S5_STATE), xi.reshape(bsz, S5_GROUPS, S5_STATE),
    )
    return x, states


def kernel(x_prompt, x_sample, cache_fox_kv, cache_fox_logf, cache_nsa_kv, cache_nsa_win_kv, state_ssd,
           state_ssd_conv, state_s5_re, state_s5_im, page_table, ffn1_norm, ffn1_w1, ffn1_w3, ffn1_w2, mix_norm,
           w_in, ssd_conv_w, ssd_conv_b, ssd_dt_bias, ssd_a_log, ssd_d, ssd_norm, fox_f_bias, fox_out_norm,
           nsa_cmp_pe, nsa_cmp_w, nsa_out_norm, s5_lambda_re, s5_lambda_im, s5_log_dt, s5_b_re, s5_b_im,
           s5_c_re, s5_c_im, s5_d, s5_w_glu, s5_out_norm, w_out, ffn2_norm, ffn2_w1, ffn2_w3, ffn2_w2, final_norm):
    w = dict(ffn1_norm=ffn1_norm, ffn1_w1=ffn1_w1, ffn1_w3=ffn1_w3, ffn1_w2=ffn1_w2, mix_norm=mix_norm, w_in=w_in,
             ssd_conv_w=ssd_conv_w, ssd_conv_b=ssd_conv_b, ssd_dt_bias=ssd_dt_bias, ssd_a_log=ssd_a_log, ssd_d=ssd_d,
             ssd_norm=ssd_norm, fox_f_bias=fox_f_bias, fox_out_norm=fox_out_norm, nsa_cmp_pe=nsa_cmp_pe,
             nsa_cmp_w=nsa_cmp_w, nsa_out_norm=nsa_out_norm, s5_lambda_re=s5_lambda_re, s5_lambda_im=s5_lambda_im,
             s5_log_dt=s5_log_dt, s5_b_re=s5_b_re, s5_b_im=s5_b_im, s5_c_re=s5_c_re, s5_c_im=s5_c_im, s5_d=s5_d,
             s5_w_glu=s5_w_glu, s5_out_norm=s5_out_norm, w_out=w_out, ffn2_norm=ffn2_norm, ffn2_w1=ffn2_w1,
             ffn2_w3=ffn2_w3, ffn2_w2=ffn2_w2)
    bsz_p, t_len, _ = x_prompt.shape
    bsz_s, n_real, _ = x_sample.shape
    depth = w_in.shape[0]
    q_off = page_table.shape[1] * PAGE
    fg = final_norm[None, :]
    n_pool = cache_fox_kv.shape[0]
    keys_minor = (0, 1, 3, 4, 5, 2)
    caches = (jnp.transpose(cache_fox_kv, keys_minor).reshape(n_pool, depth, 2 * GW, PAGE),
              jnp.transpose(cache_fox_logf, (0, 1, 3, 2)),
              jnp.transpose(cache_nsa_kv, keys_minor).reshape(n_pool, depth, 4 * LANES, PAGE),
              jnp.transpose(cache_nsa_win_kv, keys_minor).reshape(bsz_s, depth, 2 * LANES, WINDOW),
              cache_nsa_win_kv, state_ssd, state_ssd_conv, state_s5_re, state_s5_im)
    xp = x_prompt.reshape(bsz_p * t_len, D_MODEL)
    xs = jnp.pad(x_sample, ((0, 0), (0, T_PAD - n_real), (0, 0))).reshape(bsz_s * T_PAD, D_MODEL)
    st_p, st_s = [], []
    for name in ('ffn1_w1', 'ffn1_w3', 'ffn1_w2', 'ffn2_w1', 'ffn2_w3', 'ffn2_w2', 'w_out'):
        w[name + '_bf16'] = w[name].astype(BF16)
    for l in range(depth):
        lw = _layer_weights(w, l)
        last = l == depth - 1
        xp, sp = _prompt_layer(xp, lw, fg, layer=l, bsz=bsz_p, t_len=t_len, last=last)
        xs, ss = _sample_layer(xs, lw, fg, caches, page_table, layer=l, bsz=bsz_s, n_real=n_real, q_off=q_off,
                               last=last)
        st_p.append(sp)
        st_s.append(ss)
    y_p = xp.reshape(bsz_p, t_len, D_MODEL)
    y_s = xs.reshape(bsz_s, T_PAD, D_MODEL)[:, :n_real]
    out = [y_p, y_s]
    for i in range(8):
        out.append(jnp.stack([s[i] for s in st_p], axis=1))
        out.append(jnp.stack([s[i] for s in st_s], axis=1))
    return tuple(out)
```

```python
import functools
import math

import jax
import jax.numpy as jnp
from jax import lax
from jax.experimental import pallas as pl
from jax.experimental.pallas import tpu as pltpu

F32 = jnp.float32
BF16 = jnp.bfloat16
HIGHEST = lax.Precision.HIGHEST

D_MODEL = 2048
DEPTH = 2
HEAD_DIM = 64
GW = D_MODEL // 4
D_FF = ((8 * D_MODEL // 3 + 127) // 128) * 128
EPS = 1e-6
TINY = 1e-30
SSD_HEADS = GW // HEAD_DIM
SSD_GROUPS = 2
SSD_STATE = 64
CONV_W = 4
SSD_CONV_DIM = GW + 2 * SSD_GROUPS * SSD_STATE
SSD_CHUNK = 128
FOX_HEADS = GW // HEAD_DIM
NSA_HEADS = GW // HEAD_DIM
NSA_KV_HEADS = 2
NSA_GROUP = NSA_HEADS // NSA_KV_HEADS
CMP_BLOCK = 32
SLC_BLOCK = 64
TOP_N = 16
WINDOW = 512
FORCE_SCORE = 1e4
S5_CH = 16
S5_GROUPS = GW // S5_CH
S5_STATE = 64
S5_N = S5_GROUPS * S5_STATE
PAGE = 128

LANES = 128
SUBLANES = 8
VMEM_LIMIT = 56 * 1024 * 1024

C_XBC = 0
C_DT = 768
C_FF = 896
C_Z = 1024
C_U = 1536
C_FQ = 2048
C_FK = 2560
C_FV = 3072
C_NQ = 3584
C_NKV = 4096
C_NG = 4864
P_W = 5120
FF_TILE = 256
FFN_ROWS = 1024


def _cparams(sem):
    return pltpu.CompilerParams(dimension_semantics=sem, vmem_limit_bytes=VMEM_LIMIT)


def _rms(x, g):
    ms = jnp.mean(x * x, axis=-1, keepdims=True)
    return x * lax.rsqrt(ms + EPS) * g


def _sigmoid(x):
    return 1.0 / (1.0 + jnp.exp(-x))


def _silu(x):
    return x * _sigmoid(x)


def _softplus(x):
    return jnp.maximum(x, 0.0) + jnp.log(1.0 + jnp.exp(-jnp.abs(x)))


def _log_sigmoid(x):
    return jnp.minimum(x, 0.0) - jnp.log(1.0 + jnp.exp(-jnp.abs(x)))


def _dot(a, b):
    return jnp.dot(a, b, preferred_element_type=F32)


def _dot_nt(a, b):
    return lax.dot_general(a, b, (((1,), (1,)), ((), ())), preferred_element_type=F32)


def _dot_tn(a, b):
    return lax.dot_general(a, b, (((0,), (0,)), ((), ())), preferred_element_type=F32)


def _ffn_body(x_ref, g_ref, w1_ref, w3_ref, w2_ref, fg_ref, o_ref, h_ref, acc_ref, *, final_norm, nk, tf):
    k = pl.program_id(1)
    tail = D_FF - (nk - 1) * tf

    @pl.when(k == 0)
    def _():
        h_ref[...] = _rms(x_ref[...], g_ref[...]).astype(BF16)
        acc_ref[...] = jnp.zeros_like(acc_ref)

    def partial_sum(width):
        h = h_ref[...]
        a = _dot(h, w1_ref[:, :width].astype(BF16))
        b = _dot(h, w3_ref[:, :width].astype(BF16))
        return _dot((_silu(a) * b).astype(BF16), w2_ref[:width, :].astype(BF16))

    @pl.when(k < nk - 1)
    def _():
        acc_ref[...] += partial_sum(tf)

    @pl.when(k == nk - 1)
    def _():
        y = x_ref[...] + 0.5 * (acc_ref[...] + partial_sum(tail))
        if final_norm:
            y = _rms(y, fg_ref[...])
        o_ref[...] = y


def ffn_call(x, g, w1, w3, w2, fg, *, tm, layer, final_norm, tf=FF_TILE):
    m = x.shape[0]
    nk = pl.cdiv(D_FF, tf)
    assert m % tm == 0 and w1.shape[1:] == (D_MODEL, D_FF) and w2.shape[1:] == (D_FF, D_MODEL)
    once = pl.Buffered(1)
    return pl.pallas_call(
        functools.partial(_ffn_body, final_norm=final_norm, nk=nk, tf=tf),
        grid=(m // tm, nk),
        in_specs=[
            pl.BlockSpec((tm, D_MODEL), lambda i, k: (i, 0), pipeline_mode=once),
            pl.BlockSpec((1, D_MODEL), lambda i, k: (0, 0)),
            pl.BlockSpec((None, D_MODEL, tf), lambda i, k: (layer, 0, k)),
            pl.BlockSpec((None, D_MODEL, tf), lambda i, k: (layer, 0, k)),
            pl.BlockSpec((None, tf, D_MODEL), lambda i, k: (layer, k, 0)),
            pl.BlockSpec((1, D_MODEL), lambda i, k: (0, 0)),
        ],
        out_specs=pl.BlockSpec((tm, D_MODEL), lambda i, k: (i, 0), pipeline_mode=once),
        out_shape=jax.ShapeDtypeStruct((m, D_MODEL), F32),
        scratch_shapes=[pltpu.VMEM((tm, D_MODEL), BF16), pltpu.VMEM((tm, D_MODEL), F32)],
        compiler_params=_cparams(("parallel", "arbitrary")),
        name="ffn",
    )(x, g, w1, w3, w2, fg)


IN_TILE = P_W // 2


def _inproj_body(x_ref, g_ref, w_ref, o_ref, fkv_ref, nkv_ref, win_ref, h_ref):
    j = pl.program_id(1)

    @pl.when(j == 0)
    def _():
        h_ref[...] = _rms(x_ref[...], g_ref[...]).astype(BF16)
        o_ref[...] = _dot(h_ref[...], w_ref[...])

    @pl.when(j == 1)
    def _():
        y = _dot(h_ref[...], w_ref[...])
        o_ref[...] = y
        fkv_ref[...] = y[:, C_FK - IN_TILE:C_FK - IN_TILE + 2 * GW]
        nkv_ref[...] = y[:, C_NKV - IN_TILE:C_NKV - IN_TILE + 4 * LANES]
        win_ref[...] = y[:, C_NKV - IN_TILE + 4 * LANES:C_NKV - IN_TILE + 6 * LANES]


def inproj_call(x, g, w, *, tm):
    m = x.shape[0]
    assert m % tm == 0 and IN_TILE <= C_FK
    row = lambda width: pl.BlockSpec((tm, width), lambda i, j: (i, 0))
    out_specs = [pl.BlockSpec((tm, IN_TILE), lambda i, j: (i, j)), row(2 * GW), row(4 * LANES), row(2 * LANES)]
    out_shape = [jax.ShapeDtypeStruct((m, P_W), F32), jax.ShapeDtypeStruct((m, 2 * GW), F32),
                 jax.ShapeDtypeStruct((m, 4 * LANES), F32), jax.ShapeDtypeStruct((m, 2 * LANES), F32)]
    return pl.pallas_call(
        _inproj_body,
        grid=(m // tm, P_W // IN_TILE),
        in_specs=[
            pl.BlockSpec((tm, D_MODEL), lambda i, j: (i, 0)),
            pl.BlockSpec((1, D_MODEL), lambda i, j: (0, 0)),
            pl.BlockSpec((D_MODEL, IN_TILE), lambda i, j: (0, j)),
        ],
        out_specs=out_specs,
        out_shape=out_shape,
        scratch_shapes=[pltpu.VMEM((tm, D_MODEL), BF16)],
        compiler_params=_cparams(("parallel", "arbitrary")),
        name="inproj",
    )(x, g, w)


def _outproj_body(x_ref, a_ref, b_ref, c_ref, d_ref, gn_ref, w_ref, o_ref):
    y = x_ref[...]
    for i, r in enumerate((a_ref, b_ref, c_ref, d_ref)):
        y = y + _dot(_rms(r[...], gn_ref[i:i + 1, :]).astype(BF16), w_ref[i * GW:(i + 1) * GW, :].astype(BF16))
    o_ref[...] = y


def outproj_call(x, ya, yb, yc, yd, gains, w, *, tm, layer):
    m = x.shape[0]
    assert m % tm == 0
    yspec = pl.BlockSpec((tm, GW), lambda i: (i, 0))
    return pl.pallas_call(
        _outproj_body,
        grid=(m // tm,),
        in_specs=[
            pl.BlockSpec((tm, D_MODEL), lambda i: (i, 0)),
            yspec, yspec, yspec, yspec,
            pl.BlockSpec((4, GW), lambda i: (0, 0)),
            pl.BlockSpec((None, D_MODEL, D_MODEL), lambda i: (layer, 0, 0), pipeline_mode=pl.Buffered(1)),
        ],
        out_specs=pl.BlockSpec((tm, D_MODEL), lambda i: (i, 0)),
        out_shape=jax.ShapeDtypeStruct((m, D_MODEL), F32),
        compiler_params=_cparams(("parallel",)),
        name="outproj",
    )(x, ya, yb, yc, yd, gains, w)


def _ssd_body(xbc_ref, dt_ref, z_ref, conv0_ref, ssm0_ref, cw_ref, cb_ref, dtb_ref, alog_ref, dd_ref,
              y_ref, ssm_ref, conv_ref, xp_ref, act_ref, st_ref, *, n_real):
    c = pl.program_id(1)
    nc = pl.num_programs(1)
    q = SSD_CHUNK
    halo = SUBLANES

    @pl.when(c == 0)
    def _():
        xp_ref[halo - 3:halo, :] = conv0_ref[...]
        st_ref[...] = ssm0_ref[...]

    xr = xbc_ref[...]
    xp_ref[halo:halo + q, :] = xr
    conv = (cb_ref[...] + cw_ref[3:4, :] * xr
            + cw_ref[2:3, :] * xp_ref[halo - 1:halo - 1 + q, :]
            + cw_ref[1:2, :] * xp_ref[halo - 2:halo - 2 + q, :]
            + cw_ref[0:1, :] * xp_ref[halo - 3:halo - 3 + q, :])
    act_ref[...] = _silu(conv)

    row = lax.broadcasted_iota(jnp.int32, (q, LANES), 0)
    dt = jnp.where(row < n_real, _softplus(dt_ref[...] + dtb_ref[...]), 0.0)
    a = -jnp.exp(alog_ref[...])
    ti = lax.broadcasted_iota(jnp.int32, (q, q), 0)
    si = lax.broadcasted_iota(jnp.int32, (q, q), 1)
    causal = si <= ti
    acs = jnp.dot(causal.astype(F32), dt * a, preferred_element_type=F32, precision=HIGHEST)
    acs_t = acs.T
    e_acs = jnp.exp(acs)
    acs_last = acs[q - 1:q, :]
    w_end = jnp.exp(acs_last - acs) * dt
    e_last = jnp.exp(acs_last)

    for g in range(SSD_GROUPS):
        bm = act_ref[:, GW + g * SSD_STATE:GW + (g + 1) * SSD_STATE]
        cm = act_ref[:, GW + (SSD_GROUPS + g) * SSD_STATE:GW + (SSD_GROUPS + g + 1) * SSD_STATE]
        bm16 = bm.astype(BF16)
        cm16 = cm.astype(BF16)
        cb = _dot_nt(cm16, bm16)
        for hh in range(SSD_HEADS // SSD_GROUPS):
            h = g * (SSD_HEADS // SSD_GROUPS) + hh
            xs = act_ref[:, h * HEAD_DIM:(h + 1) * HEAD_DIM]
            seg = acs[:, h:h + 1] - acs_t[h:h + 1, :]
            decay = jnp.exp(jnp.where(causal, seg, -jnp.inf))
            y = _dot((cb * decay).astype(BF16), (xs * dt[:, h:h + 1]).astype(BF16))
            s_in = st_ref[h]
            y = y + _dot_nt(cm16, s_in.astype(BF16)) * e_acs[:, h:h + 1]
            y = y + dd_ref[:, h:h + 1] * xs
            cs = _dot_tn((xs * w_end[:, h:h + 1]).astype(BF16), bm16)
            st_ref[h] = e_last[:, h:h + 1] * s_in + cs
            zs = z_ref[:, h * HEAD_DIM:(h + 1) * HEAD_DIM]
            y_ref[:, h * HEAD_DIM:(h + 1) * HEAD_DIM] = y * _silu(zs)

    last_real = min(n_real, q)
    conv_ref[...] = xp_ref[halo + last_real - 3:halo + last_real, :]
    xp_ref[halo - 3:halo, :] = xp_ref[halo + q - 3:halo + q, :]

    @pl.when(c == nc - 1)
    def _():
        ssm_ref[...] = st_ref[...]


def ssd_call(p, conv0, ssm0, cw, cb, dtb, alog, dd, *, bsz, t_len, n_real):
    q = SSD_CHUNK
    nc = t_len // q
    assert t_len % q == 0 and (nc == 1 or n_real == q)
    row = lambda b, c: b * nc + c
    vec = lambda shape: pl.BlockSpec(shape, lambda b, c: (0, 0))
    return pl.pallas_call(
        functools.partial(_ssd_body, n_real=n_real),
        grid=(bsz, nc),
        in_specs=[
            pl.BlockSpec((q, SSD_CONV_DIM), lambda b, c: (row(b, c), C_XBC // SSD_CONV_DIM)),
            pl.BlockSpec((q, LANES), lambda b, c: (row(b, c), C_DT // LANES)),
            pl.BlockSpec((q, GW), lambda b, c: (row(b, c), C_Z // GW)),
            pl.BlockSpec((None, CONV_W - 1, SSD_CONV_DIM), lambda b, c: (b, 0, 0)),
            pl.BlockSpec((None, SSD_HEADS, HEAD_DIM, SSD_STATE), lambda b, c: (b, 0, 0, 0)),
            vec((CONV_W, SSD_CONV_DIM)), vec((1, SSD_CONV_DIM)), vec((1, LANES)), vec((1, LANES)), vec((1, LANES)),
        ],
        out_specs=[
            pl.BlockSpec((q, GW), lambda b, c: (row(b, c), 0)),
            pl.BlockSpec((None, SSD_HEADS, HEAD_DIM, SSD_STATE), lambda b, c: (b, 0, 0, 0)),
            pl.BlockSpec((None, CONV_W - 1, SSD_CONV_DIM), lambda b, c: (b, 0, 0)),
        ],
        out_shape=[
            jax.ShapeDtypeStruct((bsz * t_len, GW), F32),
            jax.ShapeDtypeStruct((bsz, SSD_HEADS, HEAD_DIM, SSD_STATE), F32),
            jax.ShapeDtypeStruct((bsz, CONV_W - 1, SSD_CONV_DIM), F32),
        ],
        scratch_shapes=[
            pltpu.VMEM((SUBLANES + q, SSD_CONV_DIM), F32),
            pltpu.VMEM((q, SSD_CONV_DIM), F32),
            pltpu.VMEM((SSD_HEADS, HEAD_DIM, SSD_STATE), F32),
        ],
        compiler_params=_cparams(("parallel", "arbitrary")),
        name="ssd",
    )(p, p, p, conv0, ssm0, cw, cb, dtb, alog, dd)


def _s5_body(u_ref, bre_ref, bim_ref, are_ref, aim_ref, x0r_ref, x0i_ref, cre_ref, cim_ref, d_ref, wg_ref,
             o_ref, xr_out, xi_out, xr_ref, xi_ref, sr_ref, si_ref, *, nb, tiles, t_last):
    c = pl.program_id(0)
    per = SUBLANES // nb

    @pl.when(c == 0)
    def _():
        sr_ref[...] = x0r_ref[...]
        si_ref[...] = x0i_ref[...]

    u = u_ref[...]
    u16 = u.astype(BF16)
    xr_ref[...] = _dot(u16, bre_ref[...])
    xi_ref[...] = _dot(u16, bim_ref[...])
    ar = are_ref[...]
    ai = aim_ref[...]
    first = lax.broadcasted_iota(jnp.int32, (SUBLANES, S5_N), 0) < nb

    def step(j, carry):
        sr, si = carry
        rows = pl.ds(pl.multiple_of(j * SUBLANES, SUBLANES), SUBLANES)
        br = xr_ref[rows, :]
        bi = xi_ref[rows, :]
        vr = ar * sr - ai * si + br
        vi = ar * si + ai * sr + bi
        if per == 2:
            pr = pltpu.roll(vr, nb, 0)
            pi = pltpu.roll(vi, nb, 0)
            wr = ar * pr - ai * pi + br
            wi = ar * pi + ai * pr + bi
            outr = jnp.where(first, vr, wr)
            outi = jnp.where(first, vi, wi)
            nxt = (pltpu.roll(wr, nb, 0), pltpu.roll(wi, nb, 0))
        else:
            outr, outi, nxt = vr, vi, (vr, vi)
        xr_ref[rows, :] = outr
        xi_ref[rows, :] = outi

        @pl.when(c * tiles + j == t_last // per)
        def _():
            xr_out[...] = outr
            xi_out[...] = outi

        return nxt

    sr, si = lax.fori_loop(0, tiles, step, (sr_ref[...], si_ref[...]))
    sr_ref[...] = sr
    si_ref[...] = si

    y = _dot(xr_ref[...].astype(BF16), cre_ref[...]) - _dot(xi_ref[...].astype(BF16), cim_ref[...])
    y = y + d_ref[...] * u
    g = _dot(jax.nn.gelu(y).astype(BF16), wg_ref[...])
    o_ref[...] = g[:, :GW] * _sigmoid(g[:, GW:])


def s5_call(u, bre, bim, are, aim, x0r, x0i, cre, cim, d, wg, *, nb, t_len, steps, t_last):
    assert t_len % steps == 0 and nb in (4, 8)
    rows = steps * nb
    assert rows % SUBLANES == 0
    const = lambda shape: pl.BlockSpec(shape, lambda c: (0, 0))
    return pl.pallas_call(
        functools.partial(_s5_body, nb=nb, tiles=rows // SUBLANES, t_last=t_last),
        grid=(t_len // steps,),
        in_specs=[
            pl.BlockSpec((rows, GW), lambda c: (c, 0)),
            const((GW, S5_N)), const((GW, S5_N)), const((1, S5_N)), const((1, S5_N)),
            const((SUBLANES, S5_N)), const((SUBLANES, S5_N)),
            const((S5_N, GW)), const((S5_N, GW)), const((1, GW)), const((GW, 2 * GW)),
        ],
        out_specs=[pl.BlockSpec((rows, GW), lambda c: (c, 0)), const((SUBLANES, S5_N)), const((SUBLANES, S5_N))],
        out_shape=[jax.ShapeDtypeStruct((t_len * nb, GW), F32),
                   jax.ShapeDtypeStruct((SUBLANES, S5_N), F32), jax.ShapeDtypeStruct((SUBLANES, S5_N), F32)],
        scratch_shapes=[pltpu.VMEM((rows, S5_N), F32), pltpu.VMEM((rows, S5_N), F32),
                        pltpu.VMEM((SUBLANES, S5_N), F32), pltpu.VMEM((SUBLANES, S5_N), F32)],
        compiler_params=_cparams(("arbitrary",)),
        name="s5",
    )(u, bre, bim, are, aim, x0r, x0i, cre, cim, d, wg)


def s5_params(lam_re, lam_im, log_dt, b_re, b_im, c_re, c_im):
    dt = jnp.exp(log_dt)[:, None]
    mag = jnp.exp(lam_re * dt)
    ab_re = mag * jnp.cos(lam_im * dt)
    ab_im = mag * jnp.sin(lam_im * dt)
    den = lam_re * lam_re + lam_im * lam_im
    zr = ((ab_re - 1.0) * lam_re + ab_im * lam_im) / den
    zi = (ab_im * lam_re - (ab_re - 1.0) * lam_im) / den
    bb_re = zr[..., None] * b_re - zi[..., None] * b_im
    bb_im = zr[..., None] * b_im + zi[..., None] * b_re
    eye = jnp.eye(S5_GROUPS, dtype=F32)

    def in_mat(bb):
        return jnp.einsum('gnc,gh->gchn', bb, eye).reshape(GW, S5_N).astype(BF16)

    def out_mat(cc):
        return jnp.einsum('gcn,gh->gnhc', cc, eye).reshape(S5_N, GW).astype(BF16)

    return (in_mat(bb_re), in_mat(bb_im), ab_re.reshape(1, S5_N), ab_im.reshape(1, S5_N),
            out_mat(c_re), out_mat(c_im))


LOG2E = 1.4426950408889634
NEG = -1e30


def _fox_prep_body(f_ref, b_ref, lf_ref, cumt_ref, carry_ref):
    c = pl.program_id(1)

    @pl.when(c == 0)
    def _():
        carry_ref[...] = jnp.zeros_like(carry_ref)

    lf = _log_sigmoid(f_ref[...] + b_ref[...])
    lf_ref[...] = lf
    tc = lf.shape[0]
    ti = lax.broadcasted_iota(jnp.int32, (tc, tc), 0)
    si = lax.broadcasted_iota(jnp.int32, (tc, tc), 1)
    cum = jnp.dot((si <= ti).astype(F32), lf, preferred_element_type=F32, precision=HIGHEST) + carry_ref[...]
    cumt_ref[...] = cum.T[:FOX_HEADS, :]
    carry_ref[...] = cum[tc - 1:tc, :]


def fox_prep_call(p, bias, *, bsz, t_len, tc=256):
    nc = t_len // tc
    assert t_len % tc == 0
    return pl.pallas_call(
        _fox_prep_body,
        grid=(bsz, nc),
        in_specs=[pl.BlockSpec((tc, LANES), lambda b, c: (b * nc + c, C_FF // LANES)),
                  pl.BlockSpec((1, LANES), lambda b, c: (0, 0))],
        out_specs=[pl.BlockSpec((tc, LANES), lambda b, c: (b * nc + c, 0)),
                   pl.BlockSpec((None, FOX_HEADS, tc), lambda b, c: (b, 0, c))],
        out_shape=[jax.ShapeDtypeStruct((bsz * t_len, LANES), F32),
                   jax.ShapeDtypeStruct((bsz, FOX_HEADS, t_len), F32)],
        scratch_shapes=[pltpu.VMEM((1, LANES), F32)],
        compiler_params=_cparams(("parallel", "arbitrary")),
        name="fox_prep",
    )(p, bias)


FOX_PAIRS_PER_STEP = 2


def _fox_attn_body(q_ref, k_ref, v_ref, cumt_ref, o_ref, *, tq, tk):
    hq = pl.program_id(1)
    qi = pl.program_id(2)
    q0 = qi * tq
    npair = FOX_PAIRS_PER_STEP
    lane = lax.broadcasted_iota(jnp.int32, (tq, LANES), 1)
    t_pos = q0 + lax.broadcasted_iota(jnp.int32, (tq, tk), 0)
    s_off = lax.broadcasted_iota(jnp.int32, (tq, tk), 1)
    qs = []
    for pp in range(npair):
        q = q_ref[:, pp * LANES:(pp + 1) * LANES] * (HEAD_DIM ** -0.5 * LOG2E)
        qs.append(jnp.concatenate([jnp.where(lane < HEAD_DIM, q, 0.0), jnp.where(lane >= HEAD_DIM, q, 0.0)],
                                  axis=0).astype(BF16))

    def chunk(kc, carry, masked):
        ks = pl.ds(pl.multiple_of(kc * tk, tk), tk)
        out = []
        for pp in range(npair):
            s = _dot_nt(qs[pp], k_ref[ks, pp * LANES:(pp + 1) * LANES].astype(BF16))
            vv = v_ref[ks, pp * LANES:(pp + 1) * LANES].astype(BF16)
            for hh in range(2):
                m, l, acc = carry[2 * pp + hh]
                head = 2 * (hq * npair + pp) + hh
                sh = s[hh * tq:(hh + 1) * tq] - cumt_ref[pl.ds(head, 1), ks] * LOG2E
                if masked:
                    sh = jnp.where(kc * tk + s_off <= t_pos, sh, NEG)
                m_new = jnp.maximum(m, jnp.max(sh, axis=1, keepdims=True))
                alpha = jnp.exp2(m - m_new)
                pr = jnp.exp2(sh - m_new)
                l = alpha * l + jnp.sum(pr, axis=1, keepdims=True)
                acc = alpha * acc + _dot(pr.astype(BF16), vv)
                out.append((m_new, l, acc))
        return tuple(out)

    init = tuple((jnp.full((tq, 1), NEG, F32), jnp.zeros((tq, 1), F32), jnp.zeros((tq, LANES), F32))
                 for _ in range(2 * npair))
    n_full = q0 // tk
    carry = lax.fori_loop(0, n_full, lambda kc, c: chunk(kc, c, False), init)
    res = chunk(n_full, carry, True)
    for pp in range(npair):
        (_, la, acca), (_, lb, accb) = res[2 * pp], res[2 * pp + 1]
        o_ref[:, pp * LANES:(pp + 1) * LANES] = jnp.where(lane < HEAD_DIM, acca / la, accb / lb)


def fox_attn_call(p, cumt, *, bsz, t_len, tq=128, tk=512):
    tk = min(tk, t_len)
    nq = t_len // tq
    assert t_len % tk == 0 and tk % tq == 0
    width = FOX_PAIRS_PER_STEP * LANES
    assert GW % width == 0
    return pl.pallas_call(
        functools.partial(_fox_attn_body, tq=tq, tk=tk),
        grid=(bsz, GW // width, nq),
        in_specs=[
            pl.BlockSpec((tq, width), lambda b, hq, qi: (b * nq + qi, C_FQ // width + hq)),
            pl.BlockSpec((t_len, width), lambda b, hq, qi: (b, C_FK // width + hq)),
            pl.BlockSpec((t_len, width), lambda b, hq, qi: (b, C_FV // width + hq)),
            pl.BlockSpec((None, FOX_HEADS, t_len), lambda b, hq, qi: (b, 0, 0)),
        ],
        out_specs=pl.BlockSpec((tq, width), lambda b, hq, qi: (b * nq + qi, hq)),
        out_shape=jax.ShapeDtypeStruct((bsz * t_len, GW), F32),
        compiler_params=_cparams(("parallel", "parallel", "arbitrary")),
        name="fox_attn",
    )(p, p, p, cumt)


def _compress_rows(xk_ref, xv_ref, pe_ref, w_ref, nbc):
    half = nbc // 2
    acc = jnp.zeros((nbc, 2 * LANES), F32)
    for r in range(CMP_BLOCK):
        ev = pl.ds(r, half, stride=2 * CMP_BLOCK)
        od = pl.ds(CMP_BLOCK + r, half, stride=2 * CMP_BLOCK)
        rows = jnp.concatenate([jnp.concatenate([xk_ref[ev, :], xv_ref[ev, :]], axis=1),
                                jnp.concatenate([xk_ref[od, :], xv_ref[od, :]], axis=1)], axis=0)
        acc = acc + _dot((rows + pe_ref[r:r + 1, :]).astype(BF16), w_ref[r])
    return acc


def _nsa_cmp_p_body(xk_ref, xv_ref, pe_ref, w_ref, o_ref, *, nbc, hp):
    half = nbc // 2
    acc = _compress_rows(xk_ref, xv_ref, pe_ref, w_ref, nbc)
    o_ref[...] = jnp.zeros_like(o_ref)
    o_ref[0:half, :] = acc[0:half]
    o_ref[hp:hp + half, :] = acc[half:nbc]


def nsa_cmp_p_call(p, pe4, w4, *, bsz, t_len):
    nbc = t_len // CMP_BLOCK
    hp = max(nbc // 2, HEAD_DIM)
    return pl.pallas_call(
        functools.partial(_nsa_cmp_p_body, nbc=nbc, hp=hp),
        grid=(bsz,),
        in_specs=[pl.BlockSpec((t_len, LANES), lambda b: (b, C_NKV // LANES)),
                  pl.BlockSpec((t_len, LANES), lambda b: (b, C_NKV // LANES + 1)),
                  pl.BlockSpec((CMP_BLOCK, 2 * LANES), lambda b: (0, 0)),
                  pl.BlockSpec((CMP_BLOCK, 2 * LANES, 2 * LANES), lambda b: (0, 0, 0))],
        out_specs=pl.BlockSpec((None, 2 * hp, 2 * LANES), lambda b: (b, 0, 0)),
        out_shape=jax.ShapeDtypeStruct((bsz, 2 * hp, 2 * LANES), F32),
        compiler_params=_cparams(("parallel",)),
        name="nsa_cmp_p",
    )(p, p, pe4, w4)


def _cmp_attend(qs, cmp_ref, g, t_pos, blk, slot_ok):
    scale = HEAD_DIM ** -0.5
    kc = cmp_ref[:, g * HEAD_DIM:(g + 1) * HEAD_DIM].astype(BF16)
    vc = cmp_ref[:, 2 * HEAD_DIM + g * HEAD_DIM:2 * HEAD_DIM + (g + 1) * HEAD_DIM].astype(BF16)
    s = _dot_nt(qs, kc) * scale
    valid = slot_ok & ((blk + 1) * CMP_BLOCK - 1 <= t_pos)
    s = jnp.where(valid, s, -jnp.inf)
    m = jnp.max(s, axis=1, keepdims=True)
    m = jnp.where(m > -jnp.inf, m, 0.0)
    e = jnp.where(valid, jnp.exp(s - m), 0.0)
    pc = e / jnp.maximum(jnp.sum(e, axis=1, keepdims=True), TINY)
    return _dot(pc.astype(BF16), vc), pc


def _select_score(imp, t_pos, blk):
    cur = t_pos // SLC_BLOCK
    forced = (blk == 0) | (blk == cur) | (blk == cur - 1)
    avail = blk * SLC_BLOCK <= t_pos
    return jnp.where(avail, jnp.where(forced, FORCE_SCORE, imp), -1.0)


def _nsa_attn_p_body(q_ref, cmp_ref, ks_ref, vs_ref, kw_ref, vw_ref, gl_ref, o_ref, *, tq, tk, nbc, hp):
    qi = pl.program_id(1)
    hg_n = NSA_GROUP
    rows = hg_n * tq
    half = nbc // 2
    nbs = (nbc + 1) // 2
    q0 = qi * tq
    scale = HEAD_DIM ** -0.5
    t_col = q0 + lax.broadcasted_iota(jnp.int32, (tq, 1), 0)
    t_lane = q0 + lax.broadcasted_iota(jnp.int32, (1, tq), 1)
    t_lane_stack = jnp.concatenate([t_lane] * hg_n, axis=1)
    gate = _sigmoid(gl_ref[...])
    crow = lax.broadcasted_iota(jnp.int32, (2 * hp, 1), 0)
    slot = jnp.where(crow < hp, crow, crow - hp)
    cmp_blk = 2 * slot + jnp.where(crow < hp, 0, 1)
    slc_blk = lax.broadcasted_iota(jnp.int32, (hp, 1), 0)

    def add_bias(s, bias):
        n = s.shape[1]
        return (s.reshape(hg_n, tq, n) + bias[None]).reshape(rows, n)

    per_g = []
    for g in range(NSA_KV_HEADS):
        q_f32 = jnp.concatenate(
            [q_ref[:, (g * hg_n + hg) * HEAD_DIM:(g * hg_n + hg + 1) * HEAD_DIM] for hg in range(hg_n)], axis=0)
        qs = q_f32.astype(BF16)
        qs2 = (q_f32 * (scale * LOG2E)).astype(BF16)

        kc = cmp_ref[:, g * HEAD_DIM:(g + 1) * HEAD_DIM].astype(BF16)
        vc = cmp_ref[:, 2 * HEAD_DIM + g * HEAD_DIM:2 * HEAD_DIM + (g + 1) * HEAD_DIM].astype(BF16)
        st = _dot_nt(kc, qs) * scale
        valid = (slot < half) & ((cmp_blk + 1) * CMP_BLOCK - 1 <= t_lane_stack)
        st = jnp.where(valid, st, -jnp.inf)
        m = jnp.max(st, axis=0, keepdims=True)
        m = jnp.where(m > -jnp.inf, m, 0.0)
        e = jnp.where(valid, jnp.exp(st - m), 0.0)
        pt = e / jnp.maximum(jnp.sum(e, axis=0, keepdims=True), TINY)
        o_cmp = _dot_tn(pt.astype(BF16), vc)

        imp = pt[:, 0:tq]
        for hg in range(1, hg_n):
            imp = imp + pt[:, hg * tq:(hg + 1) * tq]
        imp = imp[0:hp] + imp[hp:2 * hp]
        score = _select_score(imp, t_lane, slc_blk)
        rank = jnp.zeros((hp, tq), jnp.int32)
        for i in range(nbs):
            row = score[i:i + 1, :]
            rank = rank + ((row > score) | ((row == score) & (i < slc_blk))).astype(jnp.int32)
        sel_t = ((rank < TOP_N) & (score >= 0.0)).astype(BF16)

        per_g.append((qs2, o_cmp, sel_t))

    def slc_chunk(kc_i, carry):
        ks = pl.ds(pl.multiple_of(kc_i * tk, tk), tk)
        jb = lax.broadcasted_iota(jnp.int32, (hp, tk), 0)
        sp = kc_i * tk + lax.broadcasted_iota(jnp.int32, (hp, tk), 1)
        expand = (jb == sp // SLC_BLOCK).astype(BF16)
        causal = kc_i * tk + lax.broadcasted_iota(jnp.int32, (tq, tk), 1) <= t_col
        out = []
        for g in range(NSA_KV_HEADS):
            qs2, _, sel_t = per_g[g]
            m, l, acc = carry[g]
            hit = _dot_tn(sel_t, expand) > 0.5
            bias = jnp.where(hit & causal, 0.0, NEG)
            s = add_bias(_dot_nt(qs2, ks_ref[ks, g * HEAD_DIM:(g + 1) * HEAD_DIM].astype(BF16)), bias)
            m_new = jnp.maximum(m, jnp.max(s, axis=1, keepdims=True))
            alpha = jnp.exp2(m - m_new)
            pr = jnp.exp2(s - m_new)
            l = alpha * l + jnp.sum(pr, axis=1, keepdims=True)
            acc = alpha * acc + _dot(pr.astype(BF16), vs_ref[ks, g * HEAD_DIM:(g + 1) * HEAD_DIM].astype(BF16))
            out.append((m_new, l, acc))
        return tuple(out)

    init = tuple((jnp.full((rows, 1), NEG, F32), jnp.zeros((rows, 1), F32), jnp.zeros((rows, HEAD_DIM), F32))
                 for _ in range(NSA_KV_HEADS))
    slc = lax.fori_loop(0, (q0 + tq + tk - 1) // tk, slc_chunk, init)

    span = WINDOW + tq
    start = pl.multiple_of(jnp.maximum(q0 - WINDOW, 0), tq)
    ws = pl.ds(start, span)
    diff = t_col - (start + lax.broadcasted_iota(jnp.int32, (tq, span), 1))
    win_bias = jnp.where((diff >= 0) & (diff < WINDOW), 0.0, NEG)
    for g in range(NSA_KV_HEADS):
        qs2, o_cmp, _ = per_g[g]
        _, l, acc = slc[g]
        o_slc = acc / l
        s = add_bias(_dot_nt(qs2, kw_ref[ws, g * HEAD_DIM:(g + 1) * HEAD_DIM].astype(BF16)), win_bias)
        pr = jnp.exp2(s - jnp.max(s, axis=1, keepdims=True))
        o_win = (_dot(pr.astype(BF16), vw_ref[ws, g * HEAD_DIM:(g + 1) * HEAD_DIM].astype(BF16))
                 / jnp.sum(pr, axis=1, keepdims=True))
        for hg in range(hg_n):
            h = g * hg_n + hg
            rs = slice(hg * tq, (hg + 1) * tq)
            o = (gate[:, 3 * h:3 * h + 1] * o_cmp[rs] + gate[:, 3 * h + 1:3 * h + 2] * o_slc[rs]
                 + gate[:, 3 * h + 2:3 * h + 3] * o_win[rs])
            o_ref[:, h * HEAD_DIM:(h + 1) * HEAD_DIM] = o


def nsa_attn_p_call(p, cmp, *, bsz, t_len, tq=128, tk=512):
    nq = t_len // tq
    nbc = t_len // CMP_BLOCK
    hp = cmp.shape[1] // 2
    assert t_len % tk == 0 and WINDOW % tq == 0 and t_len >= WINDOW + tq
    kv = lambda off: pl.BlockSpec((t_len, LANES), lambda b, qi: (b, (C_NKV + off) // LANES))
    return pl.pallas_call(
        functools.partial(_nsa_attn_p_body, tq=tq, tk=tk, nbc=nbc, hp=hp),
        grid=(bsz, nq),
        in_specs=[
            pl.BlockSpec((tq, GW), lambda b, qi: (b * nq + qi, C_NQ // GW)),
            pl.BlockSpec((None, 2 * hp, 2 * LANES), lambda b, qi: (b, 0, 0)),
            kv(2 * LANES), kv(3 * LANES), kv(4 * LANES), kv(5 * LANES),
            pl.BlockSpec((tq, LANES), lambda b, qi: (b * nq + qi, C_NG // LANES)),
        ],
        out_specs=pl.BlockSpec((tq, GW), lambda b, qi: (b * nq + qi, 0)),
        out_shape=jax.ShapeDtypeStruct((bsz * t_len, GW), F32),
        compiler_params=_cparams(("parallel", "arbitrary")),
        name="nsa_attn_p",
    )(p, cmp, p, p, p, p, p)


def nsa_cmp_params(pe, cw):
    pe4 = jnp.concatenate([pe[0], pe[0], pe[1], pe[1]], axis=1)
    wk = cw[0].reshape(CMP_BLOCK, HEAD_DIM, HEAD_DIM)
    wv = cw[1].reshape(CMP_BLOCK, HEAD_DIM, HEAD_DIM)
    z = jnp.zeros_like(wk)
    rows = [jnp.concatenate([m if i == j else z for j in range(4)], axis=2) for i, m in enumerate((wk, wk, wv, wv))]
    return pe4, jnp.concatenate(rows, axis=1).astype(BF16)


def nsa_cmp_params_t(pe, cw):
    per_page = PAGE // CMP_BLOCK
    eye = jnp.eye(per_page, dtype=F32)

    def mat(w):
        wd = jnp.transpose(w.reshape(CMP_BLOCK, HEAD_DIM, HEAD_DIM), (1, 0, 2))
        full = jnp.einsum('dre,mn->dmrne', wd, eye).reshape(HEAD_DIM, PAGE, per_page * HEAD_DIM)
        return full.reshape(HEAD_DIM // 2, 2 * PAGE, per_page * HEAD_DIM).astype(BF16)

    bias = lambda x: jnp.tile(x.T, (1, per_page))
    return bias(pe[0]), bias(pe[1]), mat(cw[0]), mat(cw[1])


T_PAD = SUBLANES
FOX_LF_PAGES = 16
FOX_KV_PAGES = 16
NSA_CMP_PAGES = 16


def _page_spec(block, layer, pages_per_step, i, tail, first_step=0):
    def index_map(b, s, pt):
        j = jnp.maximum(s - first_step, 0) * pages_per_step + i
        return (pt[b, j], layer) + tail
    return pl.BlockSpec(block, index_map)


def _fox_prep_s_body(pt_ref, *refs, n_pages):
    pg = FOX_LF_PAGES
    page_refs, o_ref, a_ref = refs[:pg], refs[pg], refs[pg + 1]
    s = pl.program_id(1)
    for i in range(pg):
        a_ref[pl.ds(pl.multiple_of((s * pg + i) * FOX_HEADS, FOX_HEADS), FOX_HEADS), :] = page_refs[i][...]

    @pl.when(s == pl.num_programs(1) - 1)
    def _():
        n = n_pages * FOX_HEADS
        a = a_ref[...]
        ji = lax.broadcasted_iota(jnp.int32, (PAGE, PAGE), 0)
        si = lax.broadcasted_iota(jnp.int32, (PAGE, PAGE), 1)
        within = jnp.dot(a, (ji > si).astype(F32), preferred_element_type=F32, precision=HIGHEST)
        tot = jnp.broadcast_to(jnp.sum(a, axis=1, keepdims=True), (n, LANES))
        ri = lax.broadcasted_iota(jnp.int32, (n, n), 0)
        ci = lax.broadcasted_iota(jnp.int32, (n, n), 1)
        later = ((ci > ri) & ((ci - ri) % FOX_HEADS == 0)).astype(F32)
        o_ref[...] = within + jnp.dot(later, tot, preferred_element_type=F32, precision=HIGHEST)


def fox_prep_s_call(page_table, cache_lf, *, layer, bsz, n_pages):
    pg = FOX_LF_PAGES
    assert n_pages % pg == 0
    n = n_pages * FOX_HEADS
    grid_spec = pltpu.PrefetchScalarGridSpec(
        num_scalar_prefetch=1,
        grid=(bsz, n_pages // pg),
        in_specs=[_page_spec((None, None, FOX_HEADS, PAGE), layer, pg, i, (0, 0)) for i in range(pg)],
        out_specs=pl.BlockSpec((None, n, LANES), lambda b, s, pt: (b, 0, 0)),
        scratch_shapes=[pltpu.VMEM((n, LANES), F32)],
    )
    return pl.pallas_call(
        functools.partial(_fox_prep_s_body, n_pages=n_pages),
        grid_spec=grid_spec,
        out_shape=jax.ShapeDtypeStruct((bsz, n, LANES), F32),
        compiler_params=_cparams(("parallel", "arbitrary")),
        name="fox_prep_s",
    )(page_table, *([cache_lf] * pg))


def _fox_attn_s_body(pt_ref, q_ref, k_ref, v_ref, f_ref, b_ref, rp_ref, *refs, n_real):
    pg = FOX_KV_PAGES
    page_refs = refs[:pg]
    o_ref, lf_ref, qbd_ref, rqp_ref, m_ref, l_ref, acc_ref = refs[pg:]
    s = pl.program_id(1)
    scale = HEAD_DIM ** -0.5
    nh = FOX_HEADS
    rows = nh * T_PAD
    row_h = lax.broadcasted_iota(jnp.int32, (rows, 1), 0) // T_PAD
    row_t = lax.broadcasted_iota(jnp.int32, (rows, 1), 0) % T_PAD

    def rep_heads(x8):
        return jnp.concatenate([jnp.broadcast_to(x8[h:h + 1, :], (T_PAD, x8.shape[1])) for h in range(nh)], axis=0)

    def attend(scores, values, m, l, acc):
        tile_max = scores[0]
        for sc in scores[1:]:
            tile_max = jnp.maximum(tile_max, sc)
        m_new = jnp.maximum(m, jnp.max(tile_max, axis=1, keepdims=True))
        alpha = jnp.exp(m - m_new)
        acc = alpha * acc
        psum = None
        for sc, pv in zip(scores, values):
            pr = jnp.exp(sc - m_new)
            psum = pr if psum is None else psum + pr
            acc = acc + pv(pr.astype(BF16))
        return m_new, alpha * l + jnp.sum(psum, axis=1, keepdims=True), acc

    @pl.when(s == 0)
    def _():
        lf = _log_sigmoid(f_ref[...] + b_ref[...])
        lf_ref[...] = lf
        tok = lax.broadcasted_iota(jnp.int32, (T_PAD, LANES), 0)
        lfm = jnp.where(tok < n_real, lf, 0.0)
        r_new = jnp.zeros((T_PAD, LANES), F32)
        for j in range(1, n_real):
            r_new = r_new + jnp.where(tok < j, lfm[j:j + 1, :], 0.0)
        tot = jnp.sum(lfm, axis=0, keepdims=True)
        lane = lax.broadcasted_iota(jnp.int32, (rows, LANES), 1)
        pick = lane == row_h
        r_q = jnp.sum(jnp.where(pick, jnp.concatenate([r_new] * nh, axis=0), 0.0), axis=1, keepdims=True)
        t_q = jnp.sum(jnp.where(pick, jnp.broadcast_to(tot, (rows, LANES)), 0.0), axis=1, keepdims=True)
        rqp_ref[...] = r_q - t_q
        col = lax.broadcasted_iota(jnp.int32, (rows, GW), 1)
        qbd = jnp.where(col // HEAD_DIM == row_h, jnp.concatenate([q_ref[...]] * nh, axis=0), 0.0)
        qbd_ref[...] = qbd.astype(BF16)
        zrow = jnp.zeros((PAGE - T_PAD, GW), F32)
        kk = jnp.concatenate([k_ref[...], zrow], axis=0)
        vv = jnp.concatenate([v_ref[...], zrow], axis=0)
        r_pad = jnp.concatenate([r_new, jnp.zeros((PAGE - T_PAD, LANES), F32)], axis=0)
        bias = rep_heads(r_pad.T[:nh, :]) - r_q
        key = lax.broadcasted_iota(jnp.int32, (rows, PAGE), 1)
        mask = (key <= row_t) & (key < n_real)
        sc = jnp.where(mask, _dot_nt(qbd.astype(BF16), kk.astype(BF16)) * scale + bias, -jnp.inf)
        m0 = jnp.full((rows, 1), -jnp.inf, F32)
        m, l, acc = attend([sc], [lambda pr: _dot(pr, vv.astype(BF16))], m0,
                           jnp.zeros((rows, 1), F32), jnp.zeros((rows, GW), F32))
        m_ref[...] = m
        l_ref[...] = l
        acc_ref[...] = acc

    @pl.when(s > 0)
    def _():
        rqp = rqp_ref[...]
        qbd = qbd_ref[...]
        scores = [_dot(qbd, page_refs[i][0:GW, :].astype(BF16)) * scale
                  + (rep_heads(rp_ref[i * nh:(i + 1) * nh, :]) - rqp) for i in range(pg)]
        values = [lambda pr, i=i: _dot_nt(pr, page_refs[i][GW:2 * GW, :].astype(BF16)) for i in range(pg)]
        m, l, acc = attend(scores, values, m_ref[...], l_ref[...], acc_ref[...])
        m_ref[...] = m
        l_ref[...] = l
        acc_ref[...] = acc

    @pl.when(s == pl.num_programs(1) - 1)
    def _():
        col = lax.broadcasted_iota(jnp.int32, (rows, GW), 1)
        o = jnp.where(col // HEAD_DIM == row_h, acc_ref[...] / l_ref[...], 0.0)
        o_ref[...] = jnp.sum(o.reshape(nh, T_PAD, GW), axis=0)


def fox_attn_s_call(page_table, p, bias, r_past, cache_kv, *, layer, bsz, n_pages, n_real):
    pg = FOX_KV_PAGES
    assert n_pages % pg == 0
    rows = FOX_HEADS * T_PAD
    new = lambda width, off: pl.BlockSpec((T_PAD, width), lambda b, s, pt: (b, off // width))
    grid_spec = pltpu.PrefetchScalarGridSpec(
        num_scalar_prefetch=1,
        grid=(bsz, 1 + n_pages // pg),
        in_specs=[new(GW, C_FQ), new(GW, C_FK), new(GW, C_FV), new(LANES, C_FF),
                  pl.BlockSpec((1, LANES), lambda b, s, pt: (0, 0)),
                  pl.BlockSpec((None, pg * FOX_HEADS, LANES), lambda b, s, pt: (b, jnp.maximum(s - 1, 0), 0))]
                 + [_page_spec((None, None, 2 * GW, PAGE), layer, pg, i, (0, 0), first_step=1) for i in range(pg)],
        out_specs=[pl.BlockSpec((T_PAD, GW), lambda b, s, pt: (b, 0)),
                   pl.BlockSpec((T_PAD, LANES), lambda b, s, pt: (b, 0))],
        scratch_shapes=[pltpu.VMEM((rows, GW), BF16), pltpu.VMEM((rows, 1), F32), pltpu.VMEM((rows, 1), F32),
                        pltpu.VMEM((rows, 1), F32), pltpu.VMEM((rows, GW), F32)],
    )
    return pl.pallas_call(
        functools.partial(_fox_attn_s_body, n_real=n_real),
        grid_spec=grid_spec,
        out_shape=[jax.ShapeDtypeStruct((bsz * T_PAD, GW), F32), jax.ShapeDtypeStruct((bsz * T_PAD, LANES), F32)],
        compiler_params=_cparams(("parallel", "arbitrary")),
        name="fox_attn_s",
    )(page_table, p, p, p, p, bias, r_past, *([cache_kv] * pg))


def _nsa_cmp_s_body(pt_ref, *refs, n_pages):
    pg = NSA_CMP_PAGES
    k_pages, v_pages = refs[:pg], refs[pg:2 * pg]
    pek_ref, pev_ref, mk_ref, mv_ref, o_ref, xk_ref, xv_ref = refs[2 * pg:]
    s = pl.program_id(1)
    for i in range(pg):
        rows = pl.ds(pl.multiple_of((s * pg + i) * PAGE, PAGE), PAGE)
        xk_ref[rows, :] = k_pages[i][...]
        xv_ref[rows, :] = v_pages[i][...]

    @pl.when(s == pl.num_programs(1) - 1)
    def _():
        per_page = PAGE // CMP_BLOCK
        width = per_page * HEAD_DIM
        for x_ref, pe_ref, m_ref, off in ((xk_ref, pek_ref, mk_ref, 0), (xv_ref, pev_ref, mv_ref, 2 * HEAD_DIM)):
            acc = jnp.zeros((NSA_KV_HEADS * n_pages, width), F32)
            for d in range(0, HEAD_DIM, 2):
                rows = jnp.concatenate(
                    [jnp.concatenate([x_ref[pl.ds(g * HEAD_DIM + dd, n_pages, stride=PAGE), :] + pe_ref[dd:dd + 1, :]
                                      for g in range(NSA_KV_HEADS)], axis=0) for dd in (d, d + 1)], axis=1)
                acc = acc + _dot(rows.astype(BF16), m_ref[d // 2])
            for g in range(NSA_KV_HEADS):
                for n in range(per_page):
                    o_ref[n * n_pages:(n + 1) * n_pages, off + g * HEAD_DIM:off + (g + 1) * HEAD_DIM] = (
                        acc[g * n_pages:(g + 1) * n_pages, n * HEAD_DIM:(n + 1) * HEAD_DIM])


def nsa_cmp_s_call(page_table, cache_kv, pek, pev, mk, mv, *, layer, bsz, n_pages):
    pg = NSA_CMP_PAGES
    assert n_pages % pg == 0
    per_page = PAGE // CMP_BLOCK
    nbc = n_pages * per_page
    const = lambda shape: pl.BlockSpec(shape, lambda b, s, pt: (0,) * len(shape))
    grid_spec = pltpu.PrefetchScalarGridSpec(
        num_scalar_prefetch=1,
        grid=(bsz, n_pages // pg),
        in_specs=[_page_spec((None, None, PAGE, PAGE), layer, pg, i, (0, 0)) for i in range(pg)]
                 + [_page_spec((None, None, PAGE, PAGE), layer, pg, i, (1, 0)) for i in range(pg)]
                 + [const((HEAD_DIM, PAGE)), const((HEAD_DIM, PAGE)),
                    const((HEAD_DIM // 2, 2 * PAGE, per_page * HEAD_DIM)),
                    const((HEAD_DIM // 2, 2 * PAGE, per_page * HEAD_DIM))],
        out_specs=pl.BlockSpec((None, nbc, 2 * LANES), lambda b, s, pt: (b, 0, 0)),
        scratch_shapes=[pltpu.VMEM((n_pages * PAGE, PAGE), F32), pltpu.VMEM((n_pages * PAGE, PAGE), F32)],
    )
    return pl.pallas_call(
        functools.partial(_nsa_cmp_s_body, n_pages=n_pages),
        grid_spec=grid_spec,
        out_shape=jax.ShapeDtypeStruct((bsz, nbc, 2 * LANES), F32),
        compiler_params=_cparams(("parallel", "arbitrary")),
        name="nsa_cmp_s",
    )(page_table, *([cache_kv] * (2 * pg)), pek, pev, mk, mv)


def _stack_heads(q_ref, g):
    hg_n = NSA_GROUP
    return jnp.concatenate(
        [q_ref[:, (g * hg_n + hg) * HEAD_DIM:(g * hg_n + hg + 1) * HEAD_DIM] for hg in range(hg_n)], axis=0)


def _nsa_topk_s_body(q_ref, cmp_ref, o_ref, idx_ref, *, q_off, nbc):
    hg_n = NSA_GROUP
    per_page = PAGE // CMP_BLOCK
    n_pages = nbc // per_page
    w = 2 * n_pages + LANES
    t_col = q_off + lax.broadcasted_iota(jnp.int32, (T_PAD, 1), 0)
    t_stack = jnp.concatenate([t_col] * hg_n, axis=0)
    c = lax.broadcasted_iota(jnp.int32, (1, nbc), 1)
    cmp_blk = per_page * (c % n_pages) + c // n_pages

    def slc_blk(i):
        return jnp.where(i < n_pages, 2 * i, jnp.where(i < 2 * n_pages, 2 * (i - n_pages) + 1, i))

    blk_i = slc_blk(lax.broadcasted_iota(jnp.int32, (w, w), 0))
    blk_j = slc_blk(lax.broadcasted_iota(jnp.int32, (w, w), 1))
    blk_row = slc_blk(lax.broadcasted_iota(jnp.int32, (1, w), 1))
    rr = lax.broadcasted_iota(jnp.int32, (TOP_N, w), 0)
    lane = lax.broadcasted_iota(jnp.int32, (TOP_N, LANES), 1)
    for g in range(NSA_KV_HEADS):
        qs = _stack_heads(q_ref, g).astype(BF16)
        o_cmp, pc = _cmp_attend(qs, cmp_ref, g, t_stack, cmp_blk, cmp_blk >= 0)
        for hg in range(hg_n):
            h = g * hg_n + hg
            o_ref[:, h * HEAD_DIM:(h + 1) * HEAD_DIM] = o_cmp[hg * T_PAD:(hg + 1) * T_PAD]
        imp = pc[0:T_PAD]
        for hg in range(1, hg_n):
            imp = imp + pc[hg * T_PAD:(hg + 1) * T_PAD]
        imp = jnp.concatenate([imp[:, 0:n_pages] + imp[:, n_pages:2 * n_pages],
                               imp[:, 2 * n_pages:3 * n_pages] + imp[:, 3 * n_pages:4 * n_pages],
                               jnp.zeros((T_PAD, LANES), F32)], axis=1)
        score = _select_score(imp, t_col, blk_row)
        score_t = jnp.concatenate([score, jnp.zeros((LANES - T_PAD, w), F32)], axis=0).T
        out = jnp.zeros((TOP_N, LANES), F32)
        for t in range(T_PAD):
            col = score_t[:, t:t + 1]
            row = score[t:t + 1, :]
            ahead = (col > row) | ((col == row) & (blk_i < blk_j))
            rank = jnp.sum(ahead.astype(F32), axis=0, keepdims=True)
            hit = (rank == rr.astype(F32)) & (row >= 0.0)
            found = jnp.sum(hit.astype(F32), axis=1, keepdims=True)
            which = jnp.sum(jnp.where(hit, blk_row.astype(F32), 0.0), axis=1, keepdims=True)
            out = jnp.where(lane == t, jnp.where(found > 0.5, which, -1.0), out)
        idx_ref[g] = out.astype(jnp.int32)


def nsa_topk_s_call(p, cmp, *, bsz, q_off):
    nbc = cmp.shape[1]
    return pl.pallas_call(
        functools.partial(_nsa_topk_s_body, q_off=q_off, nbc=nbc),
        grid=(bsz,),
        in_specs=[pl.BlockSpec((T_PAD, GW), lambda b: (b, C_NQ // GW)),
                  pl.BlockSpec((None, nbc, 2 * LANES), lambda b: (b, 0, 0))],
        out_specs=[pl.BlockSpec((T_PAD, GW), lambda b: (b, 0)),
                   pl.BlockSpec((None, NSA_KV_HEADS, TOP_N, LANES), lambda b: (b, 0, 0, 0))],
        out_shape=[jax.ShapeDtypeStruct((bsz * T_PAD, GW), F32),
                   jax.ShapeDtypeStruct((bsz, NSA_KV_HEADS, TOP_N, LANES), jnp.int32)],
        compiler_params=_cparams(("parallel",)),
        name="nsa_topk_s",
    )(p, cmp)


def _nsa_sel_s_body(pt_ref, idx_ref, q_ref, kn_ref, vn_ref, *refs, n_real, n_tok, q_off):
    k_blocks, v_blocks, o_ref = refs[:TOP_N], refs[TOP_N:2 * TOP_N], refs[2 * TOP_N]
    b, g, t = pl.program_id(0), pl.program_id(1), pl.program_id(2)
    scale = HEAD_DIM ** -0.5
    hg_n = NSA_GROUP
    base = ((b * NSA_KV_HEADS + g) * n_tok + t) * TOP_N
    new_blk = q_off // SLC_BLOCK
    qrow = q_ref[pl.ds(t, 1), :]
    qs = jnp.concatenate([qrow[:, hg * HEAD_DIM:(hg + 1) * HEAD_DIM] for hg in range(hg_n)]
                         + [jnp.zeros((SUBLANES - hg_n, HEAD_DIM), F32)], axis=0).astype(BF16)

    def pick(blk):
        return jnp.where(g == 0, blk[:, 0:HEAD_DIM], blk[:, HEAD_DIM:2 * HEAD_DIM])

    lane_half = lax.broadcasted_iota(jnp.int32, (SUBLANES, PAGE), 1) // SLC_BLOCK
    logits, values = [], []
    has_new = jnp.bool_(False)
    for r in range(TOP_N):
        j = idx_ref[base + r]
        from_cache = (j >= 0) & (j < new_blk)
        has_new = has_new | (j == new_blk)
        sc = _dot(qs, k_blocks[r][...].astype(BF16)) * scale
        logits.append(jnp.where(from_cache & (lane_half == j % 2), sc, -jnp.inf))
        values.append(lambda pr, r=r: _dot_nt(pr, v_blocks[r][...].astype(BF16)))
    sc = _dot_nt(qs, pick(kn_ref[...]).astype(BF16)) * scale
    key = lax.broadcasted_iota(jnp.int32, (SUBLANES, T_PAD), 1)
    logits.append(jnp.where(has_new & (key <= t) & (key < n_real), sc, -jnp.inf))
    values.append(lambda pr: _dot(pr, pick(vn_ref[...]).astype(BF16)))
    m = logits[0].max(axis=1, keepdims=True)
    for x in logits[1:]:
        m = jnp.maximum(m, x.max(axis=1, keepdims=True))
    l = jnp.zeros((SUBLANES, 1), F32)
    acc = jnp.zeros((SUBLANES, HEAD_DIM), F32)
    for x, pv in zip(logits, values):
        pr = jnp.exp(x - m)
        l = l + jnp.sum(pr, axis=1, keepdims=True)
        acc = acc + pv(pr.astype(BF16))
    o_ref[...] = acc / l


def nsa_sel_s_call(page_table, idx, p, cache_kv, *, layer, bsz, n_tok, n_real, q_off):
    n_cached = q_off // SLC_BLOCK
    blocks_per_page = PAGE // SLC_BLOCK

    def blk_spec(r, kind):
        def index_map(b, g, t, pt, ix):
            j = jnp.clip(ix[((b * NSA_KV_HEADS + g) * n_tok + t) * TOP_N + r], 0, n_cached - 1)
            return (pt[b, j // blocks_per_page], layer, kind * NSA_KV_HEADS + g, 0)
        return pl.BlockSpec((None, None, HEAD_DIM, PAGE), index_map)

    grid_spec = pltpu.PrefetchScalarGridSpec(
        num_scalar_prefetch=2,
        grid=(bsz, NSA_KV_HEADS, n_tok),
        in_specs=[pl.BlockSpec((T_PAD, 2 * LANES), lambda b, g, t, pt, ix: (b, C_NQ // (2 * LANES) + g)),
                  pl.BlockSpec((T_PAD, LANES), lambda b, g, t, pt, ix: (b, C_NKV // LANES + 2)),
                  pl.BlockSpec((T_PAD, LANES), lambda b, g, t, pt, ix: (b, C_NKV // LANES + 3))]
                 + [blk_spec(r, 2) for r in range(TOP_N)] + [blk_spec(r, 3) for r in range(TOP_N)],
        out_specs=pl.BlockSpec((None, None, None, SUBLANES, HEAD_DIM), lambda b, g, t, pt, ix: (b, g, t, 0, 0)),
    )
    return pl.pallas_call(
        functools.partial(_nsa_sel_s_body, n_real=n_real, n_tok=n_tok, q_off=q_off),
        grid_spec=grid_spec,
        out_shape=jax.ShapeDtypeStruct((bsz, NSA_KV_HEADS, n_tok, SUBLANES, HEAD_DIM), F32),
        compiler_params=_cparams(("parallel", "arbitrary", "arbitrary")),
        name="nsa_sel_s",
    )(page_table, idx, p, p, p, *([cache_kv] * (2 * TOP_N)))


def _nsa_win_s_body(q_ref, gl_ref, oc_ref, os_ref, win_ref, new_ref, o_ref, *, n_real, win_len):
    hg_n = NSA_GROUP
    rows = hg_n * T_PAD
    scale = HEAD_DIM ** -0.5
    gate = _sigmoid(gl_ref[...])
    t_row = lax.broadcasted_iota(jnp.int32, (rows, 1), 0) % T_PAD
    key = lax.broadcasted_iota(jnp.int32, (rows, win_len), 1)
    mask_c = key + WINDOW > t_row + win_len
    new_i = lax.broadcasted_iota(jnp.int32, (rows, PAGE), 1)
    mask_n = (new_i <= t_row) & (new_i < n_real)
    zpad = jnp.zeros((PAGE - T_PAD, HEAD_DIM), F32)
    for g in range(NSA_KV_HEADS):
        qs = _stack_heads(q_ref, g).astype(BF16)
        kt = win_ref[g * HEAD_DIM:(g + 1) * HEAD_DIM, :].astype(BF16)
        vt = win_ref[LANES + g * HEAD_DIM:LANES + (g + 1) * HEAD_DIM, :].astype(BF16)
        kn = jnp.concatenate([new_ref[:, g * HEAD_DIM:(g + 1) * HEAD_DIM], zpad], axis=0).astype(BF16)
        vn = jnp.concatenate([new_ref[:, LANES + g * HEAD_DIM:LANES + (g + 1) * HEAD_DIM], zpad], axis=0).astype(BF16)
        sc_c = jnp.where(mask_c, _dot(qs, kt) * scale, -jnp.inf)
        sc_n = jnp.where(mask_n, _dot_nt(qs, kn) * scale, -jnp.inf)
        m = jnp.maximum(jnp.max(sc_c, axis=1, keepdims=True), jnp.max(sc_n, axis=1, keepdims=True))
        pr_c = jnp.exp(sc_c - m)
        pr_n = jnp.exp(sc_n - m)
        l = jnp.sum(pr_c, axis=1, keepdims=True) + jnp.sum(pr_n, axis=1, keepdims=True)
        o_win = (_dot_nt(pr_c.astype(BF16), vt) + _dot(pr_n.astype(BF16), vn)) / l
        for hg in range(hg_n):
            h = g * hg_n + hg
            cs = slice(h * HEAD_DIM, (h + 1) * HEAD_DIM)
            o_ref[:, cs] = (gate[:, 3 * h:3 * h + 1] * oc_ref[:, cs] + gate[:, 3 * h + 1:3 * h + 2] * os_ref[:, cs]
                            + gate[:, 3 * h + 2:3 * h + 3] * o_win[hg * T_PAD:(hg + 1) * T_PAD])


def nsa_win_s_call(p, o_cmp, o_slc, cache_win, *, layer, bsz, n_real):
    win_len = cache_win.shape[3]
    assert win_len == WINDOW
    row = lambda width, off: pl.BlockSpec((T_PAD, width), lambda b: (b, off // width))
    return pl.pallas_call(
        functools.partial(_nsa_win_s_body, n_real=n_real, win_len=win_len),
        grid=(bsz,),
        in_specs=[row(GW, C_NQ), row(LANES, C_NG), row(GW, 0), row(GW, 0),
                  pl.BlockSpec((None, None, 2 * LANES, win_len), lambda b: (b, layer, 0, 0)),
                  row(2 * LANES, C_NKV + 4 * LANES)],
        out_specs=pl.BlockSpec((T_PAD, GW), lambda b: (b, 0)),
        out_shape=jax.ShapeDtypeStruct((bsz * T_PAD, GW), F32),
        compiler_params=_cparams(("parallel",)),
        name="nsa_win_s",
    )(p, p, o_cmp, o_slc, cache_win, p)


def _lane_pad(v):
    return jnp.pad(v, (0, LANES - v.shape[0]))[None, :]


def _layer_weights(w, l):
    win = w['w_in'][l]
    o_fox = GW + SSD_CONV_DIM + SSD_HEADS
    o_nsa = o_fox + 3 * GW + FOX_HEADS
    o_s5 = o_nsa + GW + 6 * NSA_KV_HEADS * HEAD_DIM + 3 * NSA_HEADS
    cols = lambda a, b: win[:, a:b]
    zpad = lambda n: jnp.zeros((D_MODEL, n), F32)
    w_in = jnp.concatenate([
        cols(GW, GW + SSD_CONV_DIM),
        cols(GW + SSD_CONV_DIM, o_fox), zpad(LANES - SSD_HEADS),
        cols(o_fox + 3 * GW, o_nsa), zpad(LANES - FOX_HEADS),
        cols(0, GW),
        cols(o_s5, o_s5 + GW),
        cols(o_fox, o_fox + 3 * GW),
        cols(o_nsa, o_nsa + GW),
        cols(o_nsa + GW, o_nsa + GW + 6 * LANES),
        cols(o_nsa + GW + 6 * LANES, o_s5), zpad(LANES - 3 * NSA_HEADS),
        zpad(P_W - C_NG - LANES),
    ], axis=1).astype(BF16)
    assert w_in.shape == (D_MODEL, P_W)
    pe4, w4 = nsa_cmp_params(w['nsa_cmp_pe'][l], w['nsa_cmp_w'][l])
    s5 = s5_params(w['s5_lambda_re'][l], w['s5_lambda_im'][l], w['s5_log_dt'][l],
                   w['s5_b_re'][l], w['s5_b_im'][l], w['s5_c_re'][l], w['s5_c_im'][l])
    return dict(
        ffn1=(w['ffn1_norm'][l][None, :], w['ffn1_w1'], w['ffn1_w3'], w['ffn1_w2']),
        ffn2=(w['ffn2_norm'][l][None, :], w['ffn2_w1'], w['ffn2_w3'], w['ffn2_w2']),
        mix_norm=w['mix_norm'][l][None, :], w_in=w_in,
        ssd=(w['ssd_conv_w'][l], w['ssd_conv_b'][l][None, :], _lane_pad(w['ssd_dt_bias'][l]),
             _lane_pad(w['ssd_a_log'][l]), _lane_pad(w['ssd_d'][l])),
        fox_bias=_lane_pad(w['fox_f_bias'][l]),
        pe4=pe4, w4=w4, cmp_t=nsa_cmp_params_t(w['nsa_cmp_pe'][l], w['nsa_cmp_w'][l]), s5=s5, s5_d=w['s5_d'][l][None, :], s5_glu=w['s5_w_glu'][l].astype(BF16),
        gains=jnp.stack([w['ssd_norm'][l], w['fox_out_norm'][l], w['nsa_out_norm'][l], w['s5_out_norm'][l]]),
        w_out=w['w_out'],
    )


def _time_major(x, bsz, t_len):
    return jnp.swapaxes(x.reshape(bsz, t_len, -1), 0, 1).reshape(t_len * bsz, -1)


def _batch_major(x, bsz, t_len):
    return jnp.swapaxes(x.reshape(t_len, bsz, -1), 0, 1).reshape(bsz * t_len, -1)


def _prompt_layer(x, lw, final_gain, *, layer, bsz, t_len, last):
    tm = 512
    x = ffn_call(x, *lw['ffn1'], final_gain, tm=FFN_ROWS, layer=layer, final_norm=False)
    p, fox_kv, nsa_kv, nsa_win = inproj_call(x, lw['mix_norm'], lw['w_in'], tm=tm)
    conv0 = jnp.zeros((bsz, CONV_W - 1, SSD_CONV_DIM), F32)
    ssm0 = jnp.zeros((bsz, SSD_HEADS, HEAD_DIM, SSD_STATE), F32)
    y_ssd, ssm, conv = ssd_call(p, conv0, ssm0, *lw['ssd'], bsz=bsz, t_len=t_len, n_real=SSD_CHUNK)
    lf, cumt = fox_prep_call(p, lw['fox_bias'], bsz=bsz, t_len=t_len)
    y_fox = fox_attn_call(p, cumt, bsz=bsz, t_len=t_len)
    cmp = nsa_cmp_p_call(p, lw['pe4'], lw['w4'], bsz=bsz, t_len=t_len)
    y_nsa = nsa_attn_p_call(p, cmp, bsz=bsz, t_len=t_len)
    bre, bim, are, aim, cre, cim = lw['s5']
    x0 = jnp.zeros((SUBLANES, S5_N), F32)
    u = _time_major(p[:, C_U:C_U + GW], bsz, t_len)
    y_s5, xr, xi = s5_call(u, bre, bim, are, aim, x0, x0, cre, cim, lw['s5_d'], lw['s5_glu'],
                           nb=bsz, t_len=t_len, steps=256, t_last=t_len - 1)
    y_s5 = _batch_major(y_s5, bsz, t_len)
    x = outproj_call(x, y_ssd, y_fox, y_nsa, y_s5, lw['gains'], lw['w_out'], tm=tm, layer=layer)
    x = ffn_call(x, *lw['ffn2'], final_gain, tm=FFN_ROWS, layer=layer, final_norm=last)
    keep = min(WINDOW, t_len)
    off = ((t_len - 1) % (SUBLANES // bsz)) * bsz
    states = (
        fox_kv.reshape(bsz, t_len, 2, FOX_HEADS, HEAD_DIM),
        lf.reshape(bsz, t_len, LANES)[:, :, :FOX_HEADS],
        nsa_kv.reshape(bsz, t_len, 4, NSA_KV_HEADS, HEAD_DIM),
        nsa_win.reshape(bsz, t_len, 2, NSA_KV_HEADS, HEAD_DIM)[:, t_len - keep:],
        ssm, conv,
        xr[off:off + bsz].reshape(bsz, S5_GROUPS, S5_STATE),
        xi[off:off + bsz].reshape(bsz, S5_GROUPS, S5_STATE),
    )
    return x, states


def _sample_layer(x, lw, final_gain, caches, page_table, *, layer, bsz, n_real, q_off, last):
    fox_kv_t, fox_lf_t, nsa_kv_t, nsa_win_t, cache_nsa_win, st_ssd, st_conv, st_re, st_im = caches
    tm = bsz * T_PAD
    n_pages = page_table.shape[1]
    assert q_off == n_pages * PAGE and q_off % SLC_BLOCK == 0 and bsz == SUBLANES
    x = ffn_call(x, *lw['ffn1'], final_gain, tm=tm, layer=layer, final_norm=False)
    p, fox_kv, nsa_kv, nsa_win = inproj_call(x, lw['mix_norm'], lw['w_in'], tm=tm)
    p3 = p.reshape(bsz, T_PAD, P_W)

    p_ssd = jnp.pad(p3[:, :, :C_U], ((0, 0), (0, SSD_CHUNK - T_PAD), (0, 0))).reshape(bsz * SSD_CHUNK, C_U)
    y_ssd, ssm, conv = ssd_call(p_ssd, st_conv[:, layer], st_ssd[:, layer], *lw['ssd'],
                                bsz=bsz, t_len=SSD_CHUNK, n_real=n_real)
    y_ssd = y_ssd.reshape(bsz, SSD_CHUNK, GW)[:, :T_PAD].reshape(tm, GW)

    r_past = fox_prep_s_call(page_table, fox_lf_t, layer=layer, bsz=bsz, n_pages=n_pages)
    y_fox, lf = fox_attn_s_call(page_table, p, lw['fox_bias'], r_past, fox_kv_t,
                                layer=layer, bsz=bsz, n_pages=n_pages, n_real=n_real)

    cmp = nsa_cmp_s_call(page_table, nsa_kv_t, *lw['cmp_t'], layer=layer, bsz=bsz, n_pages=n_pages)
    o_cmp, idx = nsa_topk_s_call(p, cmp, bsz=bsz, q_off=q_off)
    idx = jnp.swapaxes(idx[:, :, :, :n_real], 2, 3).reshape(-1)
    o_slc = nsa_sel_s_call(page_table, idx, p, nsa_kv_t,
                           layer=layer, bsz=bsz, n_tok=n_real, n_real=n_real, q_off=q_off)
    o_slc = jnp.transpose(o_slc[:, :, :, :NSA_GROUP], (0, 2, 1, 3, 4)).reshape(bsz, n_real, GW)
    o_slc = jnp.pad(o_slc, ((0, 0), (0, T_PAD - n_real), (0, 0))).reshape(tm, GW)
    y_nsa = nsa_win_s_call(p, o_cmp, o_slc, nsa_win_t, layer=layer, bsz=bsz, n_real=n_real)

    bre, bim, are, aim, cre, cim = lw['s5']
    u = _time_major(p[:, C_U:C_U + GW], bsz, T_PAD)
    y_s5, xr, xi = s5_call(u, bre, bim, are, aim, st_re[:, layer].reshape(bsz, S5_N), st_im[:, layer].reshape(bsz, S5_N),
                           cre, cim, lw['s5_d'], lw['s5_glu'], nb=bsz, t_len=T_PAD, steps=T_PAD, t_last=n_real - 1)
    y_s5 = _batch_major(y_s5, bsz, T_PAD)

    x = outproj_call(x, y_ssd, y_fox, y_nsa, y_s5, lw['gains'], lw['w_out'], tm=tm, layer=layer)
    x = ffn_call(x, *lw['ffn2'], final_gain, tm=tm, layer=layer, final_norm=last)
    real = lambda a: a.reshape(bsz, T_PAD, -1)[:, :n_real]
    win_rows = real(nsa_win).reshape(bsz, n_real, 2, NSA_KV_HEADS, HEAD_DIM)
    states = (
        real(fox_kv).reshape(bsz, n_real, 2, FOX_HEADS, HEAD_DIM),
        lf.reshape(bsz, T_PAD, LANES)[:, :n_real, :FOX_HEADS],
        real(nsa_kv).reshape(bsz, n_real, 4, NSA_KV_HEADS, HEAD_DIM),
        jnp.concatenate([cache_nsa_win[:, layer, n_real:], win_rows], axis=1),
        ssm, conv,
        xr.reshape(bsz, S5_GROUPS, S5_STATE), xi.reshape(bsz, S5_GROUPS, S5_STATE),
    )
    return x, states


def kernel(x_prompt, x_sample, cache_fox_kv, cache_fox_logf, cache_nsa_kv, cache_nsa_win_kv, state_ssd,
           state_ssd_conv, state_s5_re, state_s5_im, page_table, ffn1_norm, ffn1_w1, ffn1_w3, ffn1_w2, mix_norm,
           w_in, ssd_conv_w, ssd_conv_b, ssd_dt_bias, ssd_a_log, ssd_d, ssd_norm, fox_f_bias, fox_out_norm,
           nsa_cmp_pe, nsa_cmp_w, nsa_out_norm, s5_lambda_re, s5_lambda_im, s5_log_dt, s5_b_re, s5_b_im,
           s5_c_re, s5_c_im, s5_d, s5_w_glu, s5_out_norm, w_out, ffn2_norm, ffn2_w1, ffn2_w3, ffn2_w2, final_norm):
    w = dict(ffn1_norm=ffn1_norm, ffn1_w1=ffn1_w1, ffn1_w3=ffn1_w3, ffn1_w2=ffn1_w2, mix_norm=mix_norm, w_in=w_in,
             ssd_conv_w=ssd_conv_w, ssd_conv_b=ssd_conv_b, ssd_dt_bias=ssd_dt_bias, ssd_a_log=ssd_a_log, ssd_d=ssd_d,
             ssd_norm=ssd_norm, fox_f_bias=fox_f_bias, fox_out_norm=fox_out_norm, nsa_cmp_pe=nsa_cmp_pe,
             nsa_cmp_w=nsa_cmp_w, nsa_out_norm=nsa_out_norm, s5_lambda_re=s5_lambda_re, s5_lambda_im=s5_lambda_im,
             s5_log_dt=s5_log_dt, s5_b_re=s5_b_re, s5_b_im=s5_b_im, s5_c_re=s5_c_re, s5_c_im=s5_c_im, s5_d=s5_d,
             s5_w_glu=s5_w_glu, s5_out_norm=s5_out_norm, w_out=w_out, ffn2_norm=ffn2_norm, ffn2_w1=ffn2_w1,
             ffn2_w3=ffn2_w3, ffn2_w2=ffn2_w2)
    bsz_p, t_len, _ = x_prompt.shape
    bsz_s, n_real, _ = x_sample.shape
    depth = w_in.shape[0]
    q_off = page_table.shape[1] * PAGE
    fg = final_norm[None, :]
    n_pool = cache_fox_kv.shape[0]
    keys_minor = (0, 1, 3, 4, 5, 2)
    caches = (jnp.transpose(cache_fox_kv, keys_minor).reshape(n_pool, depth, 2 * GW, PAGE),
              jnp.transpose(cache_fox_logf, (0, 1, 3, 2)),
              jnp.transpose(cache_nsa_kv, keys_minor).reshape(n_pool, depth, 4 * LANES, PAGE),
              jnp.transpose(cache_nsa_win_kv, keys_minor).reshape(bsz_s, depth, 2 * LANES, WINDOW),
              cache_nsa_win_kv, state_ssd, state_ssd_conv, state_s5_re, state_s5_im)
    xp = x_prompt.reshape(bsz_p * t_len, D_MODEL)
    xs = jnp.pad(x_sample, ((0, 0), (0, T_PAD - n_real), (0, 0))).reshape(bsz_s * T_PAD, D_MODEL)
    st_p, st_s = [], []
    for l in range(depth):
        lw = _layer_weights(w, l)
        last = l == depth - 1
        xp, sp = _prompt_layer(xp, lw, fg, layer=l, bsz=bsz_p, t_len=t_len, last=last)
        xs, ss = _sample_layer(xs, lw, fg, caches, page_table, layer=l, bsz=bsz_s, n_real=n_real, q_off=q_off,
                               last=last)
        st_p.append(sp)
        st_s.append(ss)
    y_p = xp.reshape(bsz_p, t_len, D_MODEL)
    y_s = xs.reshape(bsz_s, T_PAD, D_MODEL)[:, :n_real]
    out = [y_p, y_s]
    for i in range(8):
        out.append(jnp.stack([s[i] for s in st_p], axis=1))
        out.append(jnp.stack([s[i] for s in st_s], axis=1))
    return tuple(out)
```

```python
import functools
import math

import jax
import jax.numpy as jnp
from jax import lax
from jax.experimental import pallas as pl
from jax.experimental.pallas import tpu as pltpu

F32 = jnp.float32
BF16 = jnp.bfloat16
HIGHEST = lax.Precision.HIGHEST

D_MODEL = 2048
DEPTH = 2
HEAD_DIM = 64
GW = D_MODEL // 4
D_FF = ((8 * D_MODEL // 3 + 127) // 128) * 128
EPS = 1e-6
TINY = 1e-30
SSD_HEADS = GW // HEAD_DIM
SSD_GROUPS = 2
SSD_STATE = 64
CONV_W = 4
SSD_CONV_DIM = GW + 2 * SSD_GROUPS * SSD_STATE
SSD_CHUNK = 128
FOX_HEADS = GW // HEAD_DIM
NSA_HEADS = GW // HEAD_DIM
NSA_KV_HEADS = 2
NSA_GROUP = NSA_HEADS // NSA_KV_HEADS
CMP_BLOCK = 32
SLC_BLOCK = 64
TOP_N = 16
WINDOW = 512
FORCE_SCORE = 1e4
S5_CH = 16
S5_GROUPS = GW // S5_CH
S5_STATE = 64
S5_N = S5_GROUPS * S5_STATE
PAGE = 128

LANES = 128
SUBLANES = 8
VMEM_LIMIT = 56 * 1024 * 1024

C_XBC = 0
C_DT = 768
C_FF = 896
C_Z = 1024
C_U = 1536
C_FQ = 2048
C_FK = 2560
C_FV = 3072
C_NQ = 3584
C_NKV = 4096
C_NG = 4864
P_W = 5120
FF_TILE = 256
FFN_ROWS = 1024


def _cparams(sem):
    return pltpu.CompilerParams(dimension_semantics=sem, vmem_limit_bytes=VMEM_LIMIT)


def _rms(x, g):
    ms = jnp.mean(x * x, axis=-1, keepdims=True)
    return x * lax.rsqrt(ms + EPS) * g


def _sigmoid(x):
    return 1.0 / (1.0 + jnp.exp(-x))


def _silu(x):
    return x * _sigmoid(x)


def _softplus(x):
    return jnp.maximum(x, 0.0) + jnp.log(1.0 + jnp.exp(-jnp.abs(x)))


def _log_sigmoid(x):
    return jnp.minimum(x, 0.0) - jnp.log(1.0 + jnp.exp(-jnp.abs(x)))


def _dot(a, b):
    return jnp.dot(a, b, preferred_element_type=F32)


def _dot_nt(a, b):
    return lax.dot_general(a, b, (((1,), (1,)), ((), ())), preferred_element_type=F32)


def _dot_tn(a, b):
    return lax.dot_general(a, b, (((0,), (0,)), ((), ())), preferred_element_type=F32)


def _ffn_body(x_ref, g_ref, w1_ref, w3_ref, w2_ref, fg_ref, o_ref, h_ref, acc_ref, *, final_norm, nk, tf):
    k = pl.program_id(1)
    tail = D_FF - (nk - 1) * tf

    @pl.when(k == 0)
    def _():
        h_ref[...] = _rms(x_ref[...], g_ref[...]).astype(BF16)
        acc_ref[...] = jnp.zeros_like(acc_ref)

    def partial_sum(width):
        h = h_ref[...]
        a = _dot(h, w1_ref[:, :width].astype(BF16))
        b = _dot(h, w3_ref[:, :width].astype(BF16))
        return _dot((_silu(a) * b).astype(BF16), w2_ref[:width, :].astype(BF16))

    @pl.when(k < nk - 1)
    def _():
        acc_ref[...] += partial_sum(tf)

    @pl.when(k == nk - 1)
    def _():
        y = x_ref[...] + 0.5 * (acc_ref[...] + partial_sum(tail))
        if final_norm:
            y = _rms(y, fg_ref[...])
        o_ref[...] = y


def ffn_call(x, g, w1, w3, w2, fg, *, tm, layer, final_norm, tf=FF_TILE):
    m = x.shape[0]
    nk = pl.cdiv(D_FF, tf)
    assert m % tm == 0 and w1.shape[1:] == (D_MODEL, D_FF) and w2.shape[1:] == (D_FF, D_MODEL)
    once = pl.Buffered(1)
    return pl.pallas_call(
        functools.partial(_ffn_body, final_norm=final_norm, nk=nk, tf=tf),
        grid=(m // tm, nk),
        in_specs=[
            pl.BlockSpec((tm, D_MODEL), lambda i, k: (i, 0), pipeline_mode=once),
            pl.BlockSpec((1, D_MODEL), lambda i, k: (0, 0)),
            pl.BlockSpec((None, D_MODEL, tf), lambda i, k: (layer, 0, k)),
            pl.BlockSpec((None, D_MODEL, tf), lambda i, k: (layer, 0, k)),
            pl.BlockSpec((None, tf, D_MODEL), lambda i, k: (layer, k, 0)),
            pl.BlockSpec((1, D_MODEL), lambda i, k: (0, 0)),
        ],
        out_specs=pl.BlockSpec((tm, D_MODEL), lambda i, k: (i, 0), pipeline_mode=once),
        out_shape=jax.ShapeDtypeStruct((m, D_MODEL), F32),
        scratch_shapes=[pltpu.VMEM((tm, D_MODEL), BF16), pltpu.VMEM((tm, D_MODEL), F32)],
        compiler_params=_cparams(("parallel", "arbitrary")),
        name="ffn",
    )(x, g, w1, w3, w2, fg)


IN_TILE = P_W // 2


def _inproj_body(x_ref, g_ref, w_ref, o_ref, fkv_ref, nkv_ref, win_ref, h_ref, *, keys_minor):
    j = pl.program_id(1)

    @pl.when(j == 0)
    def _():
        h_ref[...] = _rms(x_ref[...], g_ref[...]).astype(BF16)
        o_ref[...] = _dot(h_ref[...], w_ref[...])

    @pl.when(j == 1)
    def _():
        y = _dot(h_ref[...], w_ref[...])
        o_ref[...] = y
        fkv = y[:, C_FK - IN_TILE:C_FK - IN_TILE + 2 * GW]
        nkv = y[:, C_NKV - IN_TILE:C_NKV - IN_TILE + 4 * LANES]
        win = y[:, C_NKV - IN_TILE + 4 * LANES:C_NKV - IN_TILE + 6 * LANES]
        fkv_ref[...] = fkv.T if keys_minor else fkv
        nkv_ref[...] = nkv.T if keys_minor else nkv
        win_ref[...] = win.T if keys_minor else win


def inproj_call(x, g, w, *, tm, keys_minor=None):
    m = x.shape[0]
    assert m % tm == 0 and IN_TILE <= C_FK
    widths = (2 * GW, 4 * LANES, 2 * LANES)
    if keys_minor is None:
        side_specs = [pl.BlockSpec((tm, wd), lambda i, j: (i, 0)) for wd in widths]
        side_shapes = [jax.ShapeDtypeStruct((m, wd), F32) for wd in widths]
    else:
        bsz, t_len = keys_minor
        nt = t_len // tm
        assert t_len % tm == 0 and m == bsz * t_len
        side_specs = [pl.BlockSpec((None, wd, tm), lambda i, j: (i // nt, 0, i % nt)) for wd in widths]
        side_shapes = [jax.ShapeDtypeStruct((bsz, wd, t_len), F32) for wd in widths]
    return pl.pallas_call(
        functools.partial(_inproj_body, keys_minor=keys_minor is not None),
        grid=(m // tm, P_W // IN_TILE),
        in_specs=[
            pl.BlockSpec((tm, D_MODEL), lambda i, j: (i, 0)),
            pl.BlockSpec((1, D_MODEL), lambda i, j: (0, 0)),
            pl.BlockSpec((D_MODEL, IN_TILE), lambda i, j: (0, j)),
        ],
        out_specs=[pl.BlockSpec((tm, IN_TILE), lambda i, j: (i, j))] + side_specs,
        out_shape=[jax.ShapeDtypeStruct((m, P_W), F32)] + side_shapes,
        scratch_shapes=[pltpu.VMEM((tm, D_MODEL), BF16)],
        compiler_params=_cparams(("parallel", "arbitrary")),
        name="inproj",
    )(x, g, w)


def _outproj_body(x_ref, a_ref, b_ref, c_ref, d_ref, gn_ref, w_ref, o_ref):
    y = x_ref[...]
    for i, r in enumerate((a_ref, b_ref, c_ref, d_ref)):
        y = y + _dot(_rms(r[...], gn_ref[i:i + 1, :]).astype(BF16), w_ref[i * GW:(i + 1) * GW, :].astype(BF16))
    o_ref[...] = y


def outproj_call(x, ya, yb, yc, yd, gains, w, *, tm, layer):
    m = x.shape[0]
    assert m % tm == 0
    yspec = pl.BlockSpec((tm, GW), lambda i: (i, 0))
    return pl.pallas_call(
        _outproj_body,
        grid=(m // tm,),
        in_specs=[
            pl.BlockSpec((tm, D_MODEL), lambda i: (i, 0)),
            yspec, yspec, yspec, yspec,
            pl.BlockSpec((4, GW), lambda i: (0, 0)),
            pl.BlockSpec((None, D_MODEL, D_MODEL), lambda i: (layer, 0, 0), pipeline_mode=pl.Buffered(1)),
        ],
        out_specs=pl.BlockSpec((tm, D_MODEL), lambda i: (i, 0)),
        out_shape=jax.ShapeDtypeStruct((m, D_MODEL), F32),
        compiler_params=_cparams(("parallel",)),
        name="outproj",
    )(x, ya, yb, yc, yd, gains, w)


def _ssd_body(xbc_ref, dt_ref, z_ref, conv0_ref, ssm0_ref, cw_ref, cb_ref, dtb_ref, alog_ref, dd_ref,
              y_ref, ssm_ref, conv_ref, xp_ref, act_ref, st_ref, *, n_real):
    c = pl.program_id(1)
    nc = pl.num_programs(1)
    q = SSD_CHUNK
    halo = SUBLANES

    @pl.when(c == 0)
    def _():
        xp_ref[halo - 3:halo, :] = conv0_ref[...]
        st_ref[...] = ssm0_ref[...]

    xr = xbc_ref[...]
    xp_ref[halo:halo + q, :] = xr
    conv = (cb_ref[...] + cw_ref[3:4, :] * xr
            + cw_ref[2:3, :] * xp_ref[halo - 1:halo - 1 + q, :]
            + cw_ref[1:2, :] * xp_ref[halo - 2:halo - 2 + q, :]
            + cw_ref[0:1, :] * xp_ref[halo - 3:halo - 3 + q, :])
    act_ref[...] = _silu(conv)

    row = lax.broadcasted_iota(jnp.int32, (q, LANES), 0)
    dt = jnp.where(row < n_real, _softplus(dt_ref[...] + dtb_ref[...]), 0.0)
    a = -jnp.exp(alog_ref[...])
    ti = lax.broadcasted_iota(jnp.int32, (q, q), 0)
    si = lax.broadcasted_iota(jnp.int32, (q, q), 1)
    causal = si <= ti
    acs = jnp.dot(causal.astype(F32), dt * a, preferred_element_type=F32, precision=HIGHEST)
    acs_t = acs.T
    e_acs = jnp.exp(acs)
    acs_last = acs[q - 1:q, :]
    w_end = jnp.exp(acs_last - acs) * dt
    e_last = jnp.exp(acs_last)

    for g in range(SSD_GROUPS):
        bm = act_ref[:, GW + g * SSD_STATE:GW + (g + 1) * SSD_STATE]
        cm = act_ref[:, GW + (SSD_GROUPS + g) * SSD_STATE:GW + (SSD_GROUPS + g + 1) * SSD_STATE]
        bm16 = bm.astype(BF16)
        cm16 = cm.astype(BF16)
        cb = _dot_nt(cm16, bm16)
        for hh in range(SSD_HEADS // SSD_GROUPS):
            h = g * (SSD_HEADS // SSD_GROUPS) + hh
            xs = act_ref[:, h * HEAD_DIM:(h + 1) * HEAD_DIM]
            seg = acs[:, h:h + 1] - acs_t[h:h + 1, :]
            decay = jnp.exp(jnp.where(causal, seg, -jnp.inf))
            y = _dot((cb * decay).astype(BF16), (xs * dt[:, h:h + 1]).astype(BF16))
            s_in = st_ref[h]
            y = y + _dot_nt(cm16, s_in.astype(BF16)) * e_acs[:, h:h + 1]
            y = y + dd_ref[:, h:h + 1] * xs
            cs = _dot_tn((xs * w_end[:, h:h + 1]).astype(BF16), bm16)
            st_ref[h] = e_last[:, h:h + 1] * s_in + cs
            zs = z_ref[:, h * HEAD_DIM:(h + 1) * HEAD_DIM]
            y_ref[:, h * HEAD_DIM:(h + 1) * HEAD_DIM] = y * _silu(zs)

    last_real = min(n_real, q)
    conv_ref[...] = xp_ref[halo + last_real - 3:halo + last_real, :]
    xp_ref[halo - 3:halo, :] = xp_ref[halo + q - 3:halo + q, :]

    @pl.when(c == nc - 1)
    def _():
        ssm_ref[...] = st_ref[...]


def ssd_call(p, conv0, ssm0, cw, cb, dtb, alog, dd, *, bsz, t_len, n_real):
    q = SSD_CHUNK
    nc = t_len // q
    assert t_len % q == 0 and (nc == 1 or n_real == q)
    row = lambda b, c: b * nc + c
    vec = lambda shape: pl.BlockSpec(shape, lambda b, c: (0, 0))
    return pl.pallas_call(
        functools.partial(_ssd_body, n_real=n_real),
        grid=(bsz, nc),
        in_specs=[
            pl.BlockSpec((q, SSD_CONV_DIM), lambda b, c: (row(b, c), C_XBC // SSD_CONV_DIM)),
            pl.BlockSpec((q, LANES), lambda b, c: (row(b, c), C_DT // LANES)),
            pl.BlockSpec((q, GW), lambda b, c: (row(b, c), C_Z // GW)),
            pl.BlockSpec((None, CONV_W - 1, SSD_CONV_DIM), lambda b, c: (b, 0, 0)),
            pl.BlockSpec((None, SSD_HEADS, HEAD_DIM, SSD_STATE), lambda b, c: (b, 0, 0, 0)),
            vec((CONV_W, SSD_CONV_DIM)), vec((1, SSD_CONV_DIM)), vec((1, LANES)), vec((1, LANES)), vec((1, LANES)),
        ],
        out_specs=[
            pl.BlockSpec((q, GW), lambda b, c: (row(b, c), 0)),
            pl.BlockSpec((None, SSD_HEADS, HEAD_DIM, SSD_STATE), lambda b, c: (b, 0, 0, 0)),
            pl.BlockSpec((None, CONV_W - 1, SSD_CONV_DIM), lambda b, c: (b, 0, 0)),
        ],
        out_shape=[
            jax.ShapeDtypeStruct((bsz * t_len, GW), F32),
            jax.ShapeDtypeStruct((bsz, SSD_HEADS, HEAD_DIM, SSD_STATE), F32),
            jax.ShapeDtypeStruct((bsz, CONV_W - 1, SSD_CONV_DIM), F32),
        ],
        scratch_shapes=[
            pltpu.VMEM((SUBLANES + q, SSD_CONV_DIM), F32),
            pltpu.VMEM((q, SSD_CONV_DIM), F32),
            pltpu.VMEM((SSD_HEADS, HEAD_DIM, SSD_STATE), F32),
        ],
        compiler_params=_cparams(("parallel", "arbitrary")),
        name="ssd",
    )(p, p, p, conv0, ssm0, cw, cb, dtb, alog, dd)


def _s5_body(u_ref, bre_ref, bim_ref, are_ref, aim_ref, x0r_ref, x0i_ref, cre_ref, cim_ref, d_ref, wg_ref,
             o_ref, xr_out, xi_out, xr_ref, xi_ref, sr_ref, si_ref, mix_ref, *, nb, steps, t_last):
    c = pl.program_id(0)
    per = SUBLANES // nb
    tiles = steps * nb // SUBLANES
    n_chunk = GW // LANES

    @pl.when(c == 0)
    def _():
        sr_ref[...] = x0r_ref[...]
        si_ref[...] = x0i_ref[...]

    for b in range(nb):
        for cc in range(n_chunk):
            mix_ref[cc, pl.ds(b, steps, stride=nb), :] = u_ref[b, :, cc * LANES:(cc + 1) * LANES]
    u = jnp.concatenate([mix_ref[cc] for cc in range(n_chunk)], axis=1)
    u16 = u.astype(BF16)
    xr_ref[...] = _dot(u16, bre_ref[...])
    xi_ref[...] = _dot(u16, bim_ref[...])
    ar = are_ref[...]
    ai = aim_ref[...]
    first = lax.broadcasted_iota(jnp.int32, (SUBLANES, S5_N), 0) < nb

    def step(j, carry):
        sr, si = carry
        rows = pl.ds(pl.multiple_of(j * SUBLANES, SUBLANES), SUBLANES)
        br = xr_ref[rows, :]
        bi = xi_ref[rows, :]
        vr = ar * sr - ai * si + br
        vi = ar * si + ai * sr + bi
        if per == 2:
            pr = pltpu.roll(vr, nb, 0)
            pi = pltpu.roll(vi, nb, 0)
            wr = ar * pr - ai * pi + br
            wi = ar * pi + ai * pr + bi
            outr = jnp.where(first, vr, wr)
            outi = jnp.where(first, vi, wi)
            nxt = (pltpu.roll(wr, nb, 0), pltpu.roll(wi, nb, 0))
        else:
            outr, outi, nxt = vr, vi, (vr, vi)
        xr_ref[rows, :] = outr
        xi_ref[rows, :] = outi

        @pl.when(c * tiles + j == t_last // per)
        def _():
            xr_out[...] = outr
            xi_out[...] = outi

        return nxt

    sr, si = lax.fori_loop(0, tiles, step, (sr_ref[...], si_ref[...]))
    sr_ref[...] = sr
    si_ref[...] = si

    y = _dot(xr_ref[...].astype(BF16), cre_ref[...]) - _dot(xi_ref[...].astype(BF16), cim_ref[...])
    y = y + d_ref[...] * u
    g = _dot(jax.nn.gelu(y).astype(BF16), wg_ref[...])
    o = g[:, :GW] * _sigmoid(g[:, GW:])
    for cc in range(n_chunk):
        mix_ref[cc] = o[:, cc * LANES:(cc + 1) * LANES]
    for b in range(nb):
        o_ref[b] = jnp.concatenate([mix_ref[cc, pl.ds(b, steps, stride=nb), :] for cc in range(n_chunk)], axis=1)


def s5_call(p3, bre, bim, are, aim, x0r, x0i, cre, cim, d, wg, *, t_len, steps, t_last):
    nb = p3.shape[0]
    assert t_len % steps == 0 and nb in (4, 8) and (steps * nb) % SUBLANES == 0
    rows = steps * nb
    const = lambda shape: pl.BlockSpec(shape, lambda c: (0, 0))
    return pl.pallas_call(
        functools.partial(_s5_body, nb=nb, steps=steps, t_last=t_last),
        grid=(t_len // steps,),
        in_specs=[
            pl.BlockSpec((nb, steps, GW), lambda c: (0, c, C_U // GW)),
            const((GW, S5_N)), const((GW, S5_N)), const((1, S5_N)), const((1, S5_N)),
            const((SUBLANES, S5_N)), const((SUBLANES, S5_N)),
            const((S5_N, GW)), const((S5_N, GW)), const((1, GW)), const((GW, 2 * GW)),
        ],
        out_specs=[pl.BlockSpec((nb, steps, GW), lambda c: (0, c, 0)), const((SUBLANES, S5_N)), const((SUBLANES, S5_N))],
        out_shape=[jax.ShapeDtypeStruct((nb, t_len, GW), F32),
                   jax.ShapeDtypeStruct((SUBLANES, S5_N), F32), jax.ShapeDtypeStruct((SUBLANES, S5_N), F32)],
        scratch_shapes=[pltpu.VMEM((rows, S5_N), F32), pltpu.VMEM((rows, S5_N), F32),
                        pltpu.VMEM((SUBLANES, S5_N), F32), pltpu.VMEM((SUBLANES, S5_N), F32),
                        pltpu.VMEM((GW // LANES, rows, LANES), F32)],
        compiler_params=_cparams(("arbitrary",)),
        name="s5",
    )(p3, bre, bim, are, aim, x0r, x0i, cre, cim, d, wg)


def s5_params(lam_re, lam_im, log_dt, b_re, b_im, c_re, c_im):
    dt = jnp.exp(log_dt)[:, None]
    mag = jnp.exp(lam_re * dt)
    ab_re = mag * jnp.cos(lam_im * dt)
    ab_im = mag * jnp.sin(lam_im * dt)
    den = lam_re * lam_re + lam_im * lam_im
    zr = ((ab_re - 1.0) * lam_re + ab_im * lam_im) / den
    zi = (ab_im * lam_re - (ab_re - 1.0) * lam_im) / den
    bb_re = zr[..., None] * b_re - zi[..., None] * b_im
    bb_im = zr[..., None] * b_im + zi[..., None] * b_re
    eye = jnp.eye(S5_GROUPS, dtype=F32)

    def in_mat(bb):
        return jnp.einsum('gnc,gh->gchn', bb, eye).reshape(GW, S5_N).astype(BF16)

    def out_mat(cc):
        return jnp.einsum('gcn,gh->gnhc', cc, eye).reshape(S5_N, GW).astype(BF16)

    return (in_mat(bb_re), in_mat(bb_im), ab_re.reshape(1, S5_N), ab_im.reshape(1, S5_N),
            out_mat(c_re), out_mat(c_im))


LOG2E = 1.4426950408889634
NEG = -1e30


def _fox_prep_body(f_ref, b_ref, lf_ref, cumt_ref, carry_ref):
    c = pl.program_id(1)

    @pl.when(c == 0)
    def _():
        carry_ref[...] = jnp.zeros_like(carry_ref)

    lf = _log_sigmoid(f_ref[...] + b_ref[...])
    lf_ref[...] = lf.T[:FOX_HEADS, :]
    tc = lf.shape[0]
    ti = lax.broadcasted_iota(jnp.int32, (tc, tc), 0)
    si = lax.broadcasted_iota(jnp.int32, (tc, tc), 1)
    cum = jnp.dot((si <= ti).astype(F32), lf, preferred_element_type=F32, precision=HIGHEST) + carry_ref[...]
    cumt_ref[...] = cum.T[:FOX_HEADS, :]
    carry_ref[...] = cum[tc - 1:tc, :]


def fox_prep_call(p, bias, *, bsz, t_len, tc=256):
    nc = t_len // tc
    assert t_len % tc == 0
    head_rows = pl.BlockSpec((None, FOX_HEADS, tc), lambda b, c: (b, 0, c))
    return pl.pallas_call(
        _fox_prep_body,
        grid=(bsz, nc),
        in_specs=[pl.BlockSpec((tc, LANES), lambda b, c: (b * nc + c, C_FF // LANES)),
                  pl.BlockSpec((1, LANES), lambda b, c: (0, 0))],
        out_specs=[head_rows, head_rows],
        out_shape=[jax.ShapeDtypeStruct((bsz, FOX_HEADS, t_len), F32),
                   jax.ShapeDtypeStruct((bsz, FOX_HEADS, t_len), F32)],
        scratch_shapes=[pltpu.VMEM((1, LANES), F32)],
        compiler_params=_cparams(("parallel", "arbitrary")),
        name="fox_prep",
    )(p, bias)


FOX_PAIRS_PER_STEP = 2


def _fox_attn_body(q_ref, k_ref, v_ref, cumt_ref, o_ref, *, tq, tk):
    hq = pl.program_id(1)
    qi = pl.program_id(2)
    q0 = qi * tq
    npair = FOX_PAIRS_PER_STEP
    lane = lax.broadcasted_iota(jnp.int32, (tq, LANES), 1)
    t_pos = q0 + lax.broadcasted_iota(jnp.int32, (tq, tk), 0)
    s_off = lax.broadcasted_iota(jnp.int32, (tq, tk), 1)
    qs = []
    for pp in range(npair):
        q = q_ref[:, pp * LANES:(pp + 1) * LANES] * (HEAD_DIM ** -0.5 * LOG2E)
        qs.append(jnp.concatenate([jnp.where(lane < HEAD_DIM, q, 0.0), jnp.where(lane >= HEAD_DIM, q, 0.0)],
                                  axis=0).astype(BF16))

    def chunk(kc, carry, masked):
        ks = pl.ds(pl.multiple_of(kc * tk, tk), tk)
        out = []
        for pp in range(npair):
            s = _dot_nt(qs[pp], k_ref[ks, pp * LANES:(pp + 1) * LANES].astype(BF16))
            vv = v_ref[ks, pp * LANES:(pp + 1) * LANES].astype(BF16)
            for hh in range(2):
                m, l, acc = carry[2 * pp + hh]
                head = 2 * (hq * npair + pp) + hh
                sh = s[hh * tq:(hh + 1) * tq] - cumt_ref[pl.ds(head, 1), ks] * LOG2E
                if masked:
                    sh = jnp.where(kc * tk + s_off <= t_pos, sh, NEG)
                m_new = jnp.maximum(m, jnp.max(sh, axis=1, keepdims=True))
                alpha = jnp.exp2(m - m_new)
                pr = jnp.exp2(sh - m_new)
                l = alpha * l + jnp.sum(pr, axis=1, keepdims=True)
                acc = alpha * acc + _dot(pr.astype(BF16), vv)
                out.append((m_new, l, acc))
        return tuple(out)

    init = tuple((jnp.full((tq, 1), NEG, F32), jnp.zeros((tq, 1), F32), jnp.zeros((tq, LANES), F32))
                 for _ in range(2 * npair))
    n_full = q0 // tk
    carry = lax.fori_loop(0, n_full, lambda kc, c: chunk(kc, c, False), init)
    res = chunk(n_full, carry, True)
    for pp in range(npair):
        (_, la, acca), (_, lb, accb) = res[2 * pp], res[2 * pp + 1]
        o_ref[:, pp * LANES:(pp + 1) * LANES] = jnp.where(lane < HEAD_DIM, acca / la, accb / lb)


def fox_attn_call(p, cumt, *, bsz, t_len, tq=128, tk=512):
    tk = min(tk, t_len)
    nq = t_len // tq
    assert t_len % tk == 0 and tk % tq == 0
    width = FOX_PAIRS_PER_STEP * LANES
    assert GW % width == 0
    return pl.pallas_call(
        functools.partial(_fox_attn_body, tq=tq, tk=tk),
        grid=(bsz, GW // width, nq),
        in_specs=[
            pl.BlockSpec((tq, width), lambda b, hq, qi: (b * nq + qi, C_FQ // width + hq)),
            pl.BlockSpec((t_len, width), lambda b, hq, qi: (b, C_FK // width + hq)),
            pl.BlockSpec((t_len, width), lambda b, hq, qi: (b, C_FV // width + hq)),
            pl.BlockSpec((None, FOX_HEADS, t_len), lambda b, hq, qi: (b, 0, 0)),
        ],
        out_specs=pl.BlockSpec((tq, width), lambda b, hq, qi: (b * nq + qi, hq)),
        out_shape=jax.ShapeDtypeStruct((bsz * t_len, GW), F32),
        compiler_params=_cparams(("parallel", "parallel", "arbitrary")),
        name="fox_attn",
    )(p, p, p, cumt)


def _compress_rows(xk_ref, xv_ref, pe_ref, w_ref, nbc):
    half = nbc // 2
    acc = jnp.zeros((nbc, 2 * LANES), F32)
    for r in range(CMP_BLOCK):
        ev = pl.ds(r, half, stride=2 * CMP_BLOCK)
        od = pl.ds(CMP_BLOCK + r, half, stride=2 * CMP_BLOCK)
        rows = jnp.concatenate([jnp.concatenate([xk_ref[ev, :], xv_ref[ev, :]], axis=1),
                                jnp.concatenate([xk_ref[od, :], xv_ref[od, :]], axis=1)], axis=0)
        acc = acc + _dot((rows + pe_ref[r:r + 1, :]).astype(BF16), w_ref[r])
    return acc


def _nsa_cmp_p_body(xk_ref, xv_ref, pe_ref, w_ref, o_ref, *, nbc, hp):
    half = nbc // 2
    acc = _compress_rows(xk_ref, xv_ref, pe_ref, w_ref, nbc)
    o_ref[...] = jnp.zeros_like(o_ref)
    o_ref[0:half, :] = acc[0:half]
    o_ref[hp:hp + half, :] = acc[half:nbc]


def nsa_cmp_p_call(p, pe4, w4, *, bsz, t_len):
    nbc = t_len // CMP_BLOCK
    hp = max(nbc // 2, HEAD_DIM)
    return pl.pallas_call(
        functools.partial(_nsa_cmp_p_body, nbc=nbc, hp=hp),
        grid=(bsz,),
        in_specs=[pl.BlockSpec((t_len, LANES), lambda b: (b, C_NKV // LANES)),
                  pl.BlockSpec((t_len, LANES), lambda b: (b, C_NKV // LANES + 1)),
                  pl.BlockSpec((CMP_BLOCK, 2 * LANES), lambda b: (0, 0)),
                  pl.BlockSpec((CMP_BLOCK, 2 * LANES, 2 * LANES), lambda b: (0, 0, 0))],
        out_specs=pl.BlockSpec((None, 2 * hp, 2 * LANES), lambda b: (b, 0, 0)),
        out_shape=jax.ShapeDtypeStruct((bsz, 2 * hp, 2 * LANES), F32),
        compiler_params=_cparams(("parallel",)),
        name="nsa_cmp_p",
    )(p, p, pe4, w4)


def _cmp_attend(qs, cmp_ref, g, t_pos, blk, slot_ok):
    scale = HEAD_DIM ** -0.5
    kc = cmp_ref[:, g * HEAD_DIM:(g + 1) * HEAD_DIM].astype(BF16)
    vc = cmp_ref[:, 2 * HEAD_DIM + g * HEAD_DIM:2 * HEAD_DIM + (g + 1) * HEAD_DIM].astype(BF16)
    s = _dot_nt(qs, kc) * scale
    valid = slot_ok & ((blk + 1) * CMP_BLOCK - 1 <= t_pos)
    s = jnp.where(valid, s, -jnp.inf)
    m = jnp.max(s, axis=1, keepdims=True)
    m = jnp.where(m > -jnp.inf, m, 0.0)
    e = jnp.where(valid, jnp.exp(s - m), 0.0)
    pc = e / jnp.maximum(jnp.sum(e, axis=1, keepdims=True), TINY)
    return _dot(pc.astype(BF16), vc), pc


def _select_score(imp, t_pos, blk):
    cur = t_pos // SLC_BLOCK
    forced = (blk == 0) | (blk == cur) | (blk == cur - 1)
    avail = blk * SLC_BLOCK <= t_pos
    return jnp.where(avail, jnp.where(forced, FORCE_SCORE, imp), -1.0)


def _nsa_attn_p_body(q_ref, cmp_ref, ks_ref, vs_ref, kw_ref, vw_ref, gl_ref, o_ref, *, tq, tk, nbc, hp):
    qi = pl.program_id(1)
    hg_n = NSA_GROUP
    rows = hg_n * tq
    half = nbc // 2
    nbs = (nbc + 1) // 2
    q0 = qi * tq
    scale = HEAD_DIM ** -0.5
    t_col = q0 + lax.broadcasted_iota(jnp.int32, (tq, 1), 0)
    t_lane = q0 + lax.broadcasted_iota(jnp.int32, (1, tq), 1)
    t_lane_stack = jnp.concatenate([t_lane] * hg_n, axis=1)
    gate = _sigmoid(gl_ref[...])
    crow = lax.broadcasted_iota(jnp.int32, (2 * hp, 1), 0)
    slot = jnp.where(crow < hp, crow, crow - hp)
    cmp_blk = 2 * slot + jnp.where(crow < hp, 0, 1)
    slc_blk = lax.broadcasted_iota(jnp.int32, (hp, 1), 0)

    def add_bias(s, bias):
        n = s.shape[1]
        return (s.reshape(hg_n, tq, n) + bias[None]).reshape(rows, n)

    per_g = []
    for g in range(NSA_KV_HEADS):
        q_f32 = jnp.concatenate(
            [q_ref[:, (g * hg_n + hg) * HEAD_DIM:(g * hg_n + hg + 1) * HEAD_DIM] for hg in range(hg_n)], axis=0)
        qs = q_f32.astype(BF16)
        qs2 = (q_f32 * (scale * LOG2E)).astype(BF16)

        kc = cmp_ref[:, g * HEAD_DIM:(g + 1) * HEAD_DIM].astype(BF16)
        vc = cmp_ref[:, 2 * HEAD_DIM + g * HEAD_DIM:2 * HEAD_DIM + (g + 1) * HEAD_DIM].astype(BF16)
        st = _dot_nt(kc, qs) * scale
        valid = (slot < half) & ((cmp_blk + 1) * CMP_BLOCK - 1 <= t_lane_stack)
        st = jnp.where(valid, st, -jnp.inf)
        m = jnp.max(st, axis=0, keepdims=True)
        m = jnp.where(m > -jnp.inf, m, 0.0)
        e = jnp.where(valid, jnp.exp(st - m), 0.0)
        pt = e / jnp.maximum(jnp.sum(e, axis=0, keepdims=True), TINY)
        o_cmp = _dot_tn(pt.astype(BF16), vc)

        imp = pt[:, 0:tq]
        for hg in range(1, hg_n):
            imp = imp + pt[:, hg * tq:(hg + 1) * tq]
        imp = imp[0:hp] + imp[hp:2 * hp]
        score = _select_score(imp, t_lane, slc_blk)
        rank = jnp.zeros((hp, tq), jnp.int32)
        for i in range(nbs):
            row = score[i:i + 1, :]
            rank = rank + ((row > score) | ((row == score) & (i < slc_blk))).astype(jnp.int32)
        sel_t = ((rank < TOP_N) & (score >= 0.0)).astype(BF16)

        per_g.append((qs2, o_cmp, sel_t))

    def slc_chunk(kc_i, carry):
        ks = pl.ds(pl.multiple_of(kc_i * tk, tk), tk)
        jb = lax.broadcasted_iota(jnp.int32, (hp, tk), 0)
        sp = kc_i * tk + lax.broadcasted_iota(jnp.int32, (hp, tk), 1)
        expand = (jb == sp // SLC_BLOCK).astype(BF16)
        causal = kc_i * tk + lax.broadcasted_iota(jnp.int32, (tq, tk), 1) <= t_col
        out = []
        for g in range(NSA_KV_HEADS):
            qs2, _, sel_t = per_g[g]
            m, l, acc = carry[g]
            hit = _dot_tn(sel_t, expand) > 0.5
            bias = jnp.where(hit & causal, 0.0, NEG)
            s = add_bias(_dot_nt(qs2, ks_ref[ks, g * HEAD_DIM:(g + 1) * HEAD_DIM].astype(BF16)), bias)
            m_new = jnp.maximum(m, jnp.max(s, axis=1, keepdims=True))
            alpha = jnp.exp2(m - m_new)
            pr = jnp.exp2(s - m_new)
            l = alpha * l + jnp.sum(pr, axis=1, keepdims=True)
            acc = alpha * acc + _dot(pr.astype(BF16), vs_ref[ks, g * HEAD_DIM:(g + 1) * HEAD_DIM].astype(BF16))
            out.append((m_new, l, acc))
        return tuple(out)

    init = tuple((jnp.full((rows, 1), NEG, F32), jnp.zeros((rows, 1), F32), jnp.zeros((rows, HEAD_DIM), F32))
                 for _ in range(NSA_KV_HEADS))
    slc = lax.fori_loop(0, (q0 + tq + tk - 1) // tk, slc_chunk, init)

    span = WINDOW + tq
    start = pl.multiple_of(jnp.maximum(q0 - WINDOW, 0), tq)
    ws = pl.ds(start, span)
    diff = t_col - (start + lax.broadcasted_iota(jnp.int32, (tq, span), 1))
    win_bias = jnp.where((diff >= 0) & (diff < WINDOW), 0.0, NEG)
    for g in range(NSA_KV_HEADS):
        qs2, o_cmp, _ = per_g[g]
        _, l, acc = slc[g]
        o_slc = acc / l
        s = add_bias(_dot_nt(qs2, kw_ref[ws, g * HEAD_DIM:(g + 1) * HEAD_DIM].astype(BF16)), win_bias)
        pr = jnp.exp2(s - jnp.max(s, axis=1, keepdims=True))
        o_win = (_dot(pr.astype(BF16), vw_ref[ws, g * HEAD_DIM:(g + 1) * HEAD_DIM].astype(BF16))
                 / jnp.sum(pr, axis=1, keepdims=True))
        for hg in range(hg_n):
            h = g * hg_n + hg
            rs = slice(hg * tq, (hg + 1) * tq)
            o = (gate[:, 3 * h:3 * h + 1] * o_cmp[rs] + gate[:, 3 * h + 1:3 * h + 2] * o_slc[rs]
                 + gate[:, 3 * h + 2:3 * h + 3] * o_win[rs])
            o_ref[:, h * HEAD_DIM:(h + 1) * HEAD_DIM] = o


def nsa_attn_p_call(p, cmp, *, bsz, t_len, tq=128, tk=512):
    nq = t_len // tq
    nbc = t_len // CMP_BLOCK
    hp = cmp.shape[1] // 2
    assert t_len % tk == 0 and WINDOW % tq == 0 and t_len >= WINDOW + tq
    kv = lambda off: pl.BlockSpec((t_len, LANES), lambda b, qi: (b, (C_NKV + off) // LANES))
    return pl.pallas_call(
        functools.partial(_nsa_attn_p_body, tq=tq, tk=tk, nbc=nbc, hp=hp),
        grid=(bsz, nq),
        in_specs=[
            pl.BlockSpec((tq, GW), lambda b, qi: (b * nq + qi, C_NQ // GW)),
            pl.BlockSpec((None, 2 * hp, 2 * LANES), lambda b, qi: (b, 0, 0)),
            kv(2 * LANES), kv(3 * LANES), kv(4 * LANES), kv(5 * LANES),
            pl.BlockSpec((tq, LANES), lambda b, qi: (b * nq + qi, C_NG // LANES)),
        ],
        out_specs=pl.BlockSpec((tq, GW), lambda b, qi: (b * nq + qi, 0)),
        out_shape=jax.ShapeDtypeStruct((bsz * t_len, GW), F32),
        compiler_params=_cparams(("parallel", "arbitrary")),
        name="nsa_attn_p",
    )(p, cmp, p, p, p, p, p)


def nsa_cmp_params(pe, cw):
    pe4 = jnp.concatenate([pe[0], pe[0], pe[1], pe[1]], axis=1)
    wk = cw[0].reshape(CMP_BLOCK, HEAD_DIM, HEAD_DIM)
    wv = cw[1].reshape(CMP_BLOCK, HEAD_DIM, HEAD_DIM)
    z = jnp.zeros_like(wk)
    rows = [jnp.concatenate([m if i == j else z for j in range(4)], axis=2) for i, m in enumerate((wk, wk, wv, wv))]
    return pe4, jnp.concatenate(rows, axis=1).astype(BF16)


def nsa_cmp_params_t(pe, cw):
    per_page = PAGE // CMP_BLOCK
    eye = jnp.eye(per_page, dtype=F32)

    def mat(w):
        wd = jnp.transpose(w.reshape(CMP_BLOCK, HEAD_DIM, HEAD_DIM), (1, 0, 2))
        full = jnp.einsum('dre,mn->dmrne', wd, eye).reshape(HEAD_DIM, PAGE, per_page * HEAD_DIM)
        return full.reshape(HEAD_DIM // 2, 2 * PAGE, per_page * HEAD_DIM).astype(BF16)

    bias = lambda x: jnp.tile(x.T, (1, per_page))
    return bias(pe[0]), bias(pe[1]), mat(cw[0]), mat(cw[1])


T_PAD = SUBLANES
FOX_LF_PAGES = 16
FOX_KV_PAGES = 16
NSA_CMP_PAGES = 16


def _page_spec(block, layer, pages_per_step, i, tail, first_step=0):
    def index_map(b, s, pt):
        j = jnp.maximum(s - first_step, 0) * pages_per_step + i
        return (pt[b, j], layer) + tail
    return pl.BlockSpec(block, index_map)


def _fox_prep_s_body(pt_ref, *refs, n_pages):
    pg = FOX_LF_PAGES
    page_refs, o_ref, a_ref = refs[:pg], refs[pg], refs[pg + 1]
    s = pl.program_id(1)
    for i in range(pg):
        a_ref[pl.ds(pl.multiple_of((s * pg + i) * FOX_HEADS, FOX_HEADS), FOX_HEADS), :] = page_refs[i][...]

    @pl.when(s == pl.num_programs(1) - 1)
    def _():
        n = n_pages * FOX_HEADS
        a = a_ref[...]
        ji = lax.broadcasted_iota(jnp.int32, (PAGE, PAGE), 0)
        si = lax.broadcasted_iota(jnp.int32, (PAGE, PAGE), 1)
        within = jnp.dot(a, (ji > si).astype(F32), preferred_element_type=F32, precision=HIGHEST)
        tot = jnp.broadcast_to(jnp.sum(a, axis=1, keepdims=True), (n, LANES))
        ri = lax.broadcasted_iota(jnp.int32, (n, n), 0)
        ci = lax.broadcasted_iota(jnp.int32, (n, n), 1)
        later = ((ci > ri) & ((ci - ri) % FOX_HEADS == 0)).astype(F32)
        o_ref[...] = within + jnp.dot(later, tot, preferred_element_type=F32, precision=HIGHEST)


def fox_prep_s_call(page_table, cache_lf, *, layer, bsz, n_pages):
    pg = FOX_LF_PAGES
    assert n_pages % pg == 0
    n = n_pages * FOX_HEADS
    grid_spec = pltpu.PrefetchScalarGridSpec(
        num_scalar_prefetch=1,
        grid=(bsz, n_pages // pg),
        in_specs=[_page_spec((None, None, FOX_HEADS, PAGE), layer, pg, i, (0, 0)) for i in range(pg)],
        out_specs=pl.BlockSpec((None, n, LANES), lambda b, s, pt: (b, 0, 0)),
        scratch_shapes=[pltpu.VMEM((n, LANES), F32)],
    )
    return pl.pallas_call(
        functools.partial(_fox_prep_s_body, n_pages=n_pages),
        grid_spec=grid_spec,
        out_shape=jax.ShapeDtypeStruct((bsz, n, LANES), F32),
        compiler_params=_cparams(("parallel", "arbitrary")),
        name="fox_prep_s",
    )(page_table, *([cache_lf] * pg))


def _fox_attn_s_body(pt_ref, q_ref, k_ref, v_ref, f_ref, b_ref, rp_ref, *refs, n_real):
    pg = FOX_KV_PAGES
    page_refs = refs[:pg]
    o_ref, lf_ref, qbd_ref, rqp_ref, m_ref, l_ref, acc_ref = refs[pg:]
    s = pl.program_id(1)
    scale = HEAD_DIM ** -0.5
    nh = FOX_HEADS
    rows = nh * T_PAD
    row_h = lax.broadcasted_iota(jnp.int32, (rows, 1), 0) // T_PAD
    row_t = lax.broadcasted_iota(jnp.int32, (rows, 1), 0) % T_PAD

    def rep_heads(x8):
        return jnp.concatenate([jnp.broadcast_to(x8[h:h + 1, :], (T_PAD, x8.shape[1])) for h in range(nh)], axis=0)

    def attend(scores, values, m, l, acc):
        tile_max = scores[0]
        for sc in scores[1:]:
            tile_max = jnp.maximum(tile_max, sc)
        m_new = jnp.maximum(m, jnp.max(tile_max, axis=1, keepdims=True))
        alpha = jnp.exp(m - m_new)
        acc = alpha * acc
        psum = None
        for sc, pv in zip(scores, values):
            pr = jnp.exp(sc - m_new)
            psum = pr if psum is None else psum + pr
            acc = acc + pv(pr.astype(BF16))
        return m_new, alpha * l + jnp.sum(psum, axis=1, keepdims=True), acc

    @pl.when(s == 0)
    def _():
        lf = _log_sigmoid(f_ref[...] + b_ref[...])
        lf_ref[...] = lf
        tok = lax.broadcasted_iota(jnp.int32, (T_PAD, LANES), 0)
        lfm = jnp.where(tok < n_real, lf, 0.0)
        r_new = jnp.zeros((T_PAD, LANES), F32)
        for j in range(1, n_real):
            r_new = r_new + jnp.where(tok < j, lfm[j:j + 1, :], 0.0)
        tot = jnp.sum(lfm, axis=0, keepdims=True)
        lane = lax.broadcasted_iota(jnp.int32, (rows, LANES), 1)
        pick = lane == row_h
        r_q = jnp.sum(jnp.where(pick, jnp.concatenate([r_new] * nh, axis=0), 0.0), axis=1, keepdims=True)
        t_q = jnp.sum(jnp.where(pick, jnp.broadcast_to(tot, (rows, LANES)), 0.0), axis=1, keepdims=True)
        rqp_ref[...] = r_q - t_q
        col = lax.broadcasted_iota(jnp.int32, (rows, GW), 1)
        qbd = jnp.where(col // HEAD_DIM == row_h, jnp.concatenate([q_ref[...]] * nh, axis=0), 0.0)
        qbd_ref[...] = qbd.astype(BF16)
        zrow = jnp.zeros((PAGE - T_PAD, GW), F32)
        kk = jnp.concatenate([k_ref[...], zrow], axis=0)
        vv = jnp.concatenate([v_ref[...], zrow], axis=0)
        r_pad = jnp.concatenate([r_new, jnp.zeros((PAGE - T_PAD, LANES), F32)], axis=0)
        bias = rep_heads(r_pad.T[:nh, :]) - r_q
        key = lax.broadcasted_iota(jnp.int32, (rows, PAGE), 1)
        mask = (key <= row_t) & (key < n_real)
        sc = jnp.where(mask, _dot_nt(qbd.astype(BF16), kk.astype(BF16)) * scale + bias, -jnp.inf)
        m0 = jnp.full((rows, 1), -jnp.inf, F32)
        m, l, acc = attend([sc], [lambda pr: _dot(pr, vv.astype(BF16))], m0,
                           jnp.zeros((rows, 1), F32), jnp.zeros((rows, GW), F32))
        m_ref[...] = m
        l_ref[...] = l
        acc_ref[...] = acc

    @pl.when(s > 0)
    def _():
        rqp = rqp_ref[...]
        qbd = qbd_ref[...]
        scores = [_dot(qbd, page_refs[i][0:GW, :].astype(BF16)) * scale
                  + (rep_heads(rp_ref[i * nh:(i + 1) * nh, :]) - rqp) for i in range(pg)]
        values = [lambda pr, i=i: _dot_nt(pr, page_refs[i][GW:2 * GW, :].astype(BF16)) for i in range(pg)]
        m, l, acc = attend(scores, values, m_ref[...], l_ref[...], acc_ref[...])
        m_ref[...] = m
        l_ref[...] = l
        acc_ref[...] = acc

    @pl.when(s == pl.num_programs(1) - 1)
    def _():
        col = lax.broadcasted_iota(jnp.int32, (rows, GW), 1)
        o = jnp.where(col // HEAD_DIM == row_h, acc_ref[...] / l_ref[...], 0.0)
        o_ref[...] = jnp.sum(o.reshape(nh, T_PAD, GW), axis=0)


def fox_attn_s_call(page_table, p, bias, r_past, cache_kv, *, layer, bsz, n_pages, n_real):
    pg = FOX_KV_PAGES
    assert n_pages % pg == 0
    rows = FOX_HEADS * T_PAD
    new = lambda width, off: pl.BlockSpec((T_PAD, width), lambda b, s, pt: (b, off // width))
    grid_spec = pltpu.PrefetchScalarGridSpec(
        num_scalar_prefetch=1,
        grid=(bsz, 1 + n_pages // pg),
        in_specs=[new(GW, C_FQ), new(GW, C_FK), new(GW, C_FV), new(LANES, C_FF),
                  pl.BlockSpec((1, LANES), lambda b, s, pt: (0, 0)),
                  pl.BlockSpec((None, pg * FOX_HEADS, LANES), lambda b, s, pt: (b, jnp.maximum(s - 1, 0), 0))]
                 + [_page_spec((None, None, 2 * GW, PAGE), layer, pg, i, (0, 0), first_step=1) for i in range(pg)],
        out_specs=[pl.BlockSpec((T_PAD, GW), lambda b, s, pt: (b, 0)),
                   pl.BlockSpec((T_PAD, LANES), lambda b, s, pt: (b, 0))],
        scratch_shapes=[pltpu.VMEM((rows, GW), BF16), pltpu.VMEM((rows, 1), F32), pltpu.VMEM((rows, 1), F32),
                        pltpu.VMEM((rows, 1), F32), pltpu.VMEM((rows, GW), F32)],
    )
    return pl.pallas_call(
        functools.partial(_fox_attn_s_body, n_real=n_real),
        grid_spec=grid_spec,
        out_shape=[jax.ShapeDtypeStruct((bsz * T_PAD, GW), F32), jax.ShapeDtypeStruct((bsz * T_PAD, LANES), F32)],
        compiler_params=_cparams(("parallel", "arbitrary")),
        name="fox_attn_s",
    )(page_table, p, p, p, p, bias, r_past, *([cache_kv] * pg))


def _nsa_cmp_s_body(pt_ref, *refs, n_pages):
    pg = NSA_CMP_PAGES
    k_pages, v_pages = refs[:pg], refs[pg:2 * pg]
    pek_ref, pev_ref, mk_ref, mv_ref, o_ref, xk_ref, xv_ref = refs[2 * pg:]
    s = pl.program_id(1)
    for i in range(pg):
        rows = pl.ds(pl.multiple_of((s * pg + i) * PAGE, PAGE), PAGE)
        xk_ref[rows, :] = k_pages[i][...]
        xv_ref[rows, :] = v_pages[i][...]

    @pl.when(s == pl.num_programs(1) - 1)
    def _():
        per_page = PAGE // CMP_BLOCK
        width = per_page * HEAD_DIM
        for x_ref, pe_ref, m_ref, off in ((xk_ref, pek_ref, mk_ref, 0), (xv_ref, pev_ref, mv_ref, 2 * HEAD_DIM)):
            acc = jnp.zeros((NSA_KV_HEADS * n_pages, width), F32)
            for d in range(0, HEAD_DIM, 2):
                rows = jnp.concatenate(
                    [jnp.concatenate([x_ref[pl.ds(g * HEAD_DIM + dd, n_pages, stride=PAGE), :] + pe_ref[dd:dd + 1, :]
                                      for g in range(NSA_KV_HEADS)], axis=0) for dd in (d, d + 1)], axis=1)
                acc = acc + _dot(rows.astype(BF16), m_ref[d // 2])
            for g in range(NSA_KV_HEADS):
                for n in range(per_page):
                    o_ref[n * n_pages:(n + 1) * n_pages, off + g * HEAD_DIM:off + (g + 1) * HEAD_DIM] = (
                        acc[g * n_pages:(g + 1) * n_pages, n * HEAD_DIM:(n + 1) * HEAD_DIM])


def nsa_cmp_s_call(page_table, cache_kv, pek, pev, mk, mv, *, layer, bsz, n_pages):
    pg = NSA_CMP_PAGES
    assert n_pages % pg == 0
    per_page = PAGE // CMP_BLOCK
    nbc = n_pages * per_page
    const = lambda shape: pl.BlockSpec(shape, lambda b, s, pt: (0,) * len(shape))
    grid_spec = pltpu.PrefetchScalarGridSpec(
        num_scalar_prefetch=1,
        grid=(bsz, n_pages // pg),
        in_specs=[_page_spec((None, None, PAGE, PAGE), layer, pg, i, (0, 0)) for i in range(pg)]
                 + [_page_spec((None, None, PAGE, PAGE), layer, pg, i, (1, 0)) for i in range(pg)]
                 + [const((HEAD_DIM, PAGE)), const((HEAD_DIM, PAGE)),
                    const((HEAD_DIM // 2, 2 * PAGE, per_page * HEAD_DIM)),
                    const((HEAD_DIM // 2, 2 * PAGE, per_page * HEAD_DIM))],
        out_specs=pl.BlockSpec((None, nbc, 2 * LANES), lambda b, s, pt: (b, 0, 0)),
        scratch_shapes=[pltpu.VMEM((n_pages * PAGE, PAGE), F32), pltpu.VMEM((n_pages * PAGE, PAGE), F32)],
    )
    return pl.pallas_call(
        functools.partial(_nsa_cmp_s_body, n_pages=n_pages),
        grid_spec=grid_spec,
        out_shape=jax.ShapeDtypeStruct((bsz, nbc, 2 * LANES), F32),
        compiler_params=_cparams(("parallel", "arbitrary")),
        name="nsa_cmp_s",
    )(page_table, *([cache_kv] * (2 * pg)), pek, pev, mk, mv)


def _stack_heads(q_ref, g):
    hg_n = NSA_GROUP
    return jnp.concatenate(
        [q_ref[:, (g * hg_n + hg) * HEAD_DIM:(g * hg_n + hg + 1) * HEAD_DIM] for hg in range(hg_n)], axis=0)


def _nsa_topk_s_body(q_ref, cmp_ref, o_ref, idx_ref, *, q_off, nbc):
    hg_n = NSA_GROUP
    per_page = PAGE // CMP_BLOCK
    n_pages = nbc // per_page
    w = 2 * n_pages + LANES
    t_col = q_off + lax.broadcasted_iota(jnp.int32, (T_PAD, 1), 0)
    t_stack = jnp.concatenate([t_col] * hg_n, axis=0)
    c = lax.broadcasted_iota(jnp.int32, (1, nbc), 1)
    cmp_blk = per_page * (c % n_pages) + c // n_pages

    def slc_blk(i):
        return jnp.where(i < n_pages, 2 * i, jnp.where(i < 2 * n_pages, 2 * (i - n_pages) + 1, i))

    blk_i = slc_blk(lax.broadcasted_iota(jnp.int32, (w, w), 0))
    blk_j = slc_blk(lax.broadcasted_iota(jnp.int32, (w, w), 1))
    blk_row = slc_blk(lax.broadcasted_iota(jnp.int32, (1, w), 1))
    rr = lax.broadcasted_iota(jnp.int32, (TOP_N, w), 0)
    lane = lax.broadcasted_iota(jnp.int32, (TOP_N, LANES), 1)
    for g in range(NSA_KV_HEADS):
        qs = _stack_heads(q_ref, g).astype(BF16)
        o_cmp, pc = _cmp_attend(qs, cmp_ref, g, t_stack, cmp_blk, cmp_blk >= 0)
        for hg in range(hg_n):
            h = g * hg_n + hg
            o_ref[:, h * HEAD_DIM:(h + 1) * HEAD_DIM] = o_cmp[hg * T_PAD:(hg + 1) * T_PAD]
        imp = pc[0:T_PAD]
        for hg in range(1, hg_n):
            imp = imp + pc[hg * T_PAD:(hg + 1) * T_PAD]
        imp = jnp.concatenate([imp[:, 0:n_pages] + imp[:, n_pages:2 * n_pages],
                               imp[:, 2 * n_pages:3 * n_pages] + imp[:, 3 * n_pages:4 * n_pages],
                               jnp.zeros((T_PAD, LANES), F32)], axis=1)
        score = _select_score(imp, t_col, blk_row)
        score_t = jnp.concatenate([score, jnp.zeros((LANES - T_PAD, w), F32)], axis=0).T
        out = jnp.zeros((TOP_N, LANES), F32)
        for t in range(T_PAD):
            col = score_t[:, t:t + 1]
            row = score[t:t + 1, :]
            ahead = (col > row) | ((col == row) & (blk_i < blk_j))
            rank = jnp.sum(ahead.astype(F32), axis=0, keepdims=True)
            hit = (rank == rr.astype(F32)) & (row >= 0.0)
            found = jnp.sum(hit.astype(F32), axis=1, keepdims=True)
            which = jnp.sum(jnp.where(hit, blk_row.astype(F32), 0.0), axis=1, keepdims=True)
            out = jnp.where(lane == t, jnp.where(found > 0.5, which, -1.0), out)
        idx_ref[g] = out.astype(jnp.int32)


def nsa_topk_s_call(p, cmp, *, bsz, q_off):
    nbc = cmp.shape[1]
    return pl.pallas_call(
        functools.partial(_nsa_topk_s_body, q_off=q_off, nbc=nbc),
        grid=(bsz,),
        in_specs=[pl.BlockSpec((T_PAD, GW), lambda b: (b, C_NQ // GW)),
                  pl.BlockSpec((None, nbc, 2 * LANES), lambda b: (b, 0, 0))],
        out_specs=[pl.BlockSpec((T_PAD, GW), lambda b: (b, 0)),
                   pl.BlockSpec((None, NSA_KV_HEADS, TOP_N, LANES), lambda b: (b, 0, 0, 0))],
        out_shape=[jax.ShapeDtypeStruct((bsz * T_PAD, GW), F32),
                   jax.ShapeDtypeStruct((bsz, NSA_KV_HEADS, TOP_N, LANES), jnp.int32)],
        compiler_params=_cparams(("parallel",)),
        name="nsa_topk_s",
    )(p, cmp)


def _nsa_sel_s_body(pt_ref, idx_ref, q_ref, kn_ref, vn_ref, *refs, n_real, n_tok, q_off):
    k_blocks, v_blocks, o_ref = refs[:TOP_N], refs[TOP_N:2 * TOP_N], refs[2 * TOP_N]
    b, g, t = pl.program_id(0), pl.program_id(1), pl.program_id(2)
    scale = HEAD_DIM ** -0.5
    hg_n = NSA_GROUP
    base = ((b * NSA_KV_HEADS + g) * n_tok + t) * TOP_N
    new_blk = q_off // SLC_BLOCK
    qrow = q_ref[pl.ds(t, 1), :]
    qs = jnp.concatenate([qrow[:, hg * HEAD_DIM:(hg + 1) * HEAD_DIM] for hg in range(hg_n)]
                         + [jnp.zeros((SUBLANES - hg_n, HEAD_DIM), F32)], axis=0).astype(BF16)

    def pick(blk):
        return jnp.where(g == 0, blk[:, 0:HEAD_DIM], blk[:, HEAD_DIM:2 * HEAD_DIM])

    lane_half = lax.broadcasted_iota(jnp.int32, (SUBLANES, PAGE), 1) // SLC_BLOCK
    logits, values = [], []
    has_new = jnp.bool_(False)
    for r in range(TOP_N):
        j = idx_ref[base + r]
        from_cache = (j >= 0) & (j < new_blk)
        has_new = has_new | (j == new_blk)
        sc = _dot(qs, k_blocks[r][...].astype(BF16)) * scale
        logits.append(jnp.where(from_cache & (lane_half == j % 2), sc, -jnp.inf))
        values.append(lambda pr, r=r: _dot_nt(pr, v_blocks[r][...].astype(BF16)))
    sc = _dot_nt(qs, pick(kn_ref[...]).astype(BF16)) * scale
    key = lax.broadcasted_iota(jnp.int32, (SUBLANES, T_PAD), 1)
    logits.append(jnp.where(has_new & (key <= t) & (key < n_real), sc, -jnp.inf))
    values.append(lambda pr: _dot(pr, pick(vn_ref[...]).astype(BF16)))
    m = logits[0].max(axis=1, keepdims=True)
    for x in logits[1:]:
        m = jnp.maximum(m, x.max(axis=1, keepdims=True))
    l = jnp.zeros((SUBLANES, 1), F32)
    acc = jnp.zeros((SUBLANES, HEAD_DIM), F32)
    for x, pv in zip(logits, values):
        pr = jnp.exp(x - m)
        l = l + jnp.sum(pr, axis=1, keepdims=True)
        acc = acc + pv(pr.astype(BF16))
    o_ref[...] = acc / l


def nsa_sel_s_call(page_table, idx, p, cache_kv, *, layer, bsz, n_tok, n_real, q_off):
    n_cached = q_off // SLC_BLOCK
    blocks_per_page = PAGE // SLC_BLOCK

    def blk_spec(r, kind):
        def index_map(b, g, t, pt, ix):
            j = jnp.clip(ix[((b * NSA_KV_HEADS + g) * n_tok + t) * TOP_N + r], 0, n_cached - 1)
            return (pt[b, j // blocks_per_page], layer, kind * NSA_KV_HEADS + g, 0)
        return pl.BlockSpec((None, None, HEAD_DIM, PAGE), index_map)

    grid_spec = pltpu.PrefetchScalarGridSpec(
        num_scalar_prefetch=2,
        grid=(bsz, NSA_KV_HEADS, n_tok),
        in_specs=[pl.BlockSpec((T_PAD, 2 * LANES), lambda b, g, t, pt, ix: (b, C_NQ // (2 * LANES) + g)),
                  pl.BlockSpec((T_PAD, LANES), lambda b, g, t, pt, ix: (b, C_NKV // LANES + 2)),
                  pl.BlockSpec((T_PAD, LANES), lambda b, g, t, pt, ix: (b, C_NKV // LANES + 3))]
                 + [blk_spec(r, 2) for r in range(TOP_N)] + [blk_spec(r, 3) for r in range(TOP_N)],
        out_specs=pl.BlockSpec((None, None, None, SUBLANES, HEAD_DIM), lambda b, g, t, pt, ix: (b, g, t, 0, 0)),
    )
    return pl.pallas_call(
        functools.partial(_nsa_sel_s_body, n_real=n_real, n_tok=n_tok, q_off=q_off),
        grid_spec=grid_spec,
        out_shape=jax.ShapeDtypeStruct((bsz, NSA_KV_HEADS, n_tok, SUBLANES, HEAD_DIM), F32),
        compiler_params=_cparams(("parallel", "arbitrary", "arbitrary")),
        name="nsa_sel_s",
    )(page_table, idx, p, p, p, *([cache_kv] * (2 * TOP_N)))


def _nsa_win_s_body(q_ref, gl_ref, oc_ref, os_ref, win_ref, new_ref, o_ref, *, n_real, win_len):
    hg_n = NSA_GROUP
    rows = hg_n * T_PAD
    scale = HEAD_DIM ** -0.5
    gate = _sigmoid(gl_ref[...])
    t_row = lax.broadcasted_iota(jnp.int32, (rows, 1), 0) % T_PAD
    key = lax.broadcasted_iota(jnp.int32, (rows, win_len), 1)
    mask_c = key + WINDOW > t_row + win_len
    new_i = lax.broadcasted_iota(jnp.int32, (rows, PAGE), 1)
    mask_n = (new_i <= t_row) & (new_i < n_real)
    zpad = jnp.zeros((PAGE - T_PAD, HEAD_DIM), F32)
    for g in range(NSA_KV_HEADS):
        qs = _stack_heads(q_ref, g).astype(BF16)
        kt = win_ref[g * HEAD_DIM:(g + 1) * HEAD_DIM, :].astype(BF16)
        vt = win_ref[LANES + g * HEAD_DIM:LANES + (g + 1) * HEAD_DIM, :].astype(BF16)
        kn = jnp.concatenate([new_ref[:, g * HEAD_DIM:(g + 1) * HEAD_DIM], zpad], axis=0).astype(BF16)
        vn = jnp.concatenate([new_ref[:, LANES + g * HEAD_DIM:LANES + (g + 1) * HEAD_DIM], zpad], axis=0).astype(BF16)
        sc_c = jnp.where(mask_c, _dot(qs, kt) * scale, -jnp.inf)
        sc_n = jnp.where(mask_n, _dot_nt(qs, kn) * scale, -jnp.inf)
        m = jnp.maximum(jnp.max(sc_c, axis=1, keepdims=True), jnp.max(sc_n, axis=1, keepdims=True))
        pr_c = jnp.exp(sc_c - m)
        pr_n = jnp.exp(sc_n - m)
        l = jnp.sum(pr_c, axis=1, keepdims=True) + jnp.sum(pr_n, axis=1, keepdims=True)
        o_win = (_dot_nt(pr_c.astype(BF16), vt) + _dot(pr_n.astype(BF16), vn)) / l
        for hg in range(hg_n):
            h = g * hg_n + hg
            cs = slice(h * HEAD_DIM, (h + 1) * HEAD_DIM)
            o_ref[:, cs] = (gate[:, 3 * h:3 * h + 1] * oc_ref[:, cs] + gate[:, 3 * h + 1:3 * h + 2] * os_ref[:, cs]
                            + gate[:, 3 * h + 2:3 * h + 3] * o_win[hg * T_PAD:(hg + 1) * T_PAD])


def nsa_win_s_call(p, o_cmp, o_slc, cache_win, *, layer, bsz, n_real):
    win_len = cache_win.shape[3]
    assert win_len == WINDOW
    row = lambda width, off: pl.BlockSpec((T_PAD, width), lambda b: (b, off // width))
    return pl.pallas_call(
        functools.partial(_nsa_win_s_body, n_real=n_real, win_len=win_len),
        grid=(bsz,),
        in_specs=[row(GW, C_NQ), row(LANES, C_NG), row(GW, 0), row(GW, 0),
                  pl.BlockSpec((None, None, 2 * LANES, win_len), lambda b: (b, layer, 0, 0)),
                  row(2 * LANES, C_NKV + 4 * LANES)],
        out_specs=pl.BlockSpec((T_PAD, GW), lambda b: (b, 0)),
        out_shape=jax.ShapeDtypeStruct((bsz * T_PAD, GW), F32),
        compiler_params=_cparams(("parallel",)),
        name="nsa_win_s",
    )(p, p, o_cmp, o_slc, cache_win, p)


def _lane_pad(v):
    return jnp.pad(v, (0, LANES - v.shape[0]))[None, :]


def _layer_weights(w, l):
    win = w['w_in'][l]
    o_fox = GW + SSD_CONV_DIM + SSD_HEADS
    o_nsa = o_fox + 3 * GW + FOX_HEADS
    o_s5 = o_nsa + GW + 6 * NSA_KV_HEADS * HEAD_DIM + 3 * NSA_HEADS
    cols = lambda a, b: win[:, a:b]
    zpad = lambda n: jnp.zeros((D_MODEL, n), F32)
    w_in = jnp.concatenate([
        cols(GW, GW + SSD_CONV_DIM),
        cols(GW + SSD_CONV_DIM, o_fox), zpad(LANES - SSD_HEADS),
        cols(o_fox + 3 * GW, o_nsa), zpad(LANES - FOX_HEADS),
        cols(0, GW),
        cols(o_s5, o_s5 + GW),
        cols(o_fox, o_fox + 3 * GW),
        cols(o_nsa, o_nsa + GW),
        cols(o_nsa + GW, o_nsa + GW + 6 * LANES),
        cols(o_nsa + GW + 6 * LANES, o_s5), zpad(LANES - 3 * NSA_HEADS),
        zpad(P_W - C_NG - LANES),
    ], axis=1).astype(BF16)
    assert w_in.shape == (D_MODEL, P_W)
    pe4, w4 = nsa_cmp_params(w['nsa_cmp_pe'][l], w['nsa_cmp_w'][l])
    s5 = s5_params(w['s5_lambda_re'][l], w['s5_lambda_im'][l], w['s5_log_dt'][l],
                   w['s5_b_re'][l], w['s5_b_im'][l], w['s5_c_re'][l], w['s5_c_im'][l])
    return dict(
        ffn1=(w['ffn1_norm'][l][None, :], w['ffn1_w1'], w['ffn1_w3'], w['ffn1_w2']),
        ffn2=(w['ffn2_norm'][l][None, :], w['ffn2_w1'], w['ffn2_w3'], w['ffn2_w2']),
        mix_norm=w['mix_norm'][l][None, :], w_in=w_in,
        ssd=(w['ssd_conv_w'][l], w['ssd_conv_b'][l][None, :], _lane_pad(w['ssd_dt_bias'][l]),
             _lane_pad(w['ssd_a_log'][l]), _lane_pad(w['ssd_d'][l])),
        fox_bias=_lane_pad(w['fox_f_bias'][l]),
        pe4=pe4, w4=w4, cmp_t=nsa_cmp_params_t(w['nsa_cmp_pe'][l], w['nsa_cmp_w'][l]), s5=s5, s5_d=w['s5_d'][l][None, :], s5_glu=w['s5_w_glu'][l].astype(BF16),
        gains=jnp.stack([w['ssd_norm'][l], w['fox_out_norm'][l], w['nsa_out_norm'][l], w['s5_out_norm'][l]]),
        w_out=w['w_out'],
    )


def _prompt_layer(x, lw, final_gain, *, layer, bsz, t_len, last):
    tm = 512
    x = ffn_call(x, *lw['ffn1'], final_gain, tm=FFN_ROWS, layer=layer, final_norm=False)
    p, fox_kv, nsa_kv, nsa_win = inproj_call(x, lw['mix_norm'], lw['w_in'], tm=tm, keys_minor=(bsz, t_len))
    conv0 = jnp.zeros((bsz, CONV_W - 1, SSD_CONV_DIM), F32)
    ssm0 = jnp.zeros((bsz, SSD_HEADS, HEAD_DIM, SSD_STATE), F32)
    y_ssd, ssm, conv = ssd_call(p, conv0, ssm0, *lw['ssd'], bsz=bsz, t_len=t_len, n_real=SSD_CHUNK)
    lf, cumt = fox_prep_call(p, lw['fox_bias'], bsz=bsz, t_len=t_len)
    y_fox = fox_attn_call(p, cumt, bsz=bsz, t_len=t_len)
    cmp = nsa_cmp_p_call(p, lw['pe4'], lw['w4'], bsz=bsz, t_len=t_len)
    y_nsa = nsa_attn_p_call(p, cmp, bsz=bsz, t_len=t_len)
    bre, bim, are, aim, cre, cim = lw['s5']
    x0 = jnp.zeros((SUBLANES, S5_N), F32)
    y_s5, xr, xi = s5_call(p.reshape(bsz, t_len, P_W), bre, bim, are, aim, x0, x0, cre, cim, lw['s5_d'], lw['s5_glu'],
                           t_len=t_len, steps=256, t_last=t_len - 1)
    y_s5 = y_s5.reshape(bsz * t_len, GW)
    x = outproj_call(x, y_ssd, y_fox, y_nsa, y_s5, lw['gains'], lw['w_out'], tm=tm, layer=layer)
    x = ffn_call(x, *lw['ffn2'], final_gain, tm=FFN_ROWS, layer=layer, final_norm=last)
    keep = min(WINDOW, t_len)
    off = ((t_len - 1) % (SUBLANES // bsz)) * bsz
    time_second = lambda a, *dims: jnp.moveaxis(a.reshape(bsz, *dims, a.shape[-1]), -1, 1)
    states = (
        time_second(fox_kv, 2, FOX_HEADS, HEAD_DIM),
        time_second(lf, FOX_HEADS),
        time_second(nsa_kv, 4, NSA_KV_HEADS, HEAD_DIM),
        time_second(nsa_win[:, :, t_len - keep:], 2, NSA_KV_HEADS, HEAD_DIM),
        ssm, conv,
        xr[off:off + bsz].reshape(bsz, S5_GROUPS, S5_STATE),
        xi[off:off + bsz].reshape(bsz, S5_GROUPS, S5_STATE),
    )
    return x, states


def _sample_layer(x, lw, final_gain, caches, page_table, *, layer, bsz, n_real, q_off, last):
    fox_kv_t, fox_lf_t, nsa_kv_t, nsa_win_t, cache_nsa_win, st_ssd, st_conv, st_re, st_im = caches
    tm = bsz * T_PAD
    n_pages = page_table.shape[1]
    assert q_off == n_pages * PAGE and q_off % SLC_BLOCK == 0 and bsz == SUBLANES
    x = ffn_call(x, *lw['ffn1'], final_gain, tm=tm, layer=layer, final_norm=False)
    p, fox_kv, nsa_kv, nsa_win = inproj_call(x, lw['mix_norm'], lw['w_in'], tm=tm)
    p3 = p.reshape(bsz, T_PAD, P_W)

    p_ssd = jnp.pad(p3[:, :, :C_U], ((0, 0), (0, SSD_CHUNK - T_PAD), (0, 0))).reshape(bsz * SSD_CHUNK, C_U)
    y_ssd, ssm, conv = ssd_call(p_ssd, st_conv[:, layer], st_ssd[:, layer], *lw['ssd'],
                                bsz=bsz, t_len=SSD_CHUNK, n_real=n_real)
    y_ssd = y_ssd.reshape(bsz, SSD_CHUNK, GW)[:, :T_PAD].reshape(tm, GW)

    r_past = fox_prep_s_call(page_table, fox_lf_t, layer=layer, bsz=bsz, n_pages=n_pages)
    y_fox, lf = fox_attn_s_call(page_table, p, lw['fox_bias'], r_past, fox_kv_t,
                                layer=layer, bsz=bsz, n_pages=n_pages, n_real=n_real)

    cmp = nsa_cmp_s_call(page_table, nsa_kv_t, *lw['cmp_t'], layer=layer, bsz=bsz, n_pages=n_pages)
    o_cmp, idx = nsa_topk_s_call(p, cmp, bsz=bsz, q_off=q_off)
    idx = jnp.swapaxes(idx[:, :, :, :n_real], 2, 3).reshape(-1)
    o_slc = nsa_sel_s_call(page_table, idx, p, nsa_kv_t,
                           layer=layer, bsz=bsz, n_tok=n_real, n_real=n_real, q_off=q_off)
    o_slc = jnp.transpose(o_slc[:, :, :, :NSA_GROUP], (0, 2, 1, 3, 4)).reshape(bsz, n_real, GW)
    o_slc = jnp.pad(o_slc, ((0, 0), (0, T_PAD - n_real), (0, 0))).reshape(tm, GW)
    y_nsa = nsa_win_s_call(p, o_cmp, o_slc, nsa_win_t, layer=layer, bsz=bsz, n_real=n_real)

    bre, bim, are, aim, cre, cim = lw['s5']
    y_s5, xr, xi = s5_call(p3, bre, bim, are, aim, st_re[:, layer].reshape(bsz, S5_N), st_im[:, layer].reshape(bsz, S5_N),
                           cre, cim, lw['s5_d'], lw['s5_glu'], t_len=T_PAD, steps=T_PAD, t_last=n_real - 1)
    y_s5 = y_s5.reshape(tm, GW)

    x = outproj_call(x, y_ssd, y_fox, y_nsa, y_s5, lw['gains'], lw['w_out'], tm=tm, layer=layer)
    x = ffn_call(x, *lw['ffn2'], final_gain, tm=tm, layer=layer, final_norm=last)
    real = lambda a: a.reshape(bsz, T_PAD, -1)[:, :n_real]
    win_rows = real(nsa_win).reshape(bsz, n_real, 2, NSA_KV_HEADS, HEAD_DIM)
    states = (
        real(fox_kv).reshape(bsz, n_real, 2, FOX_HEADS, HEAD_DIM),
        lf.reshape(bsz, T_PAD, LANES)[:, :n_real, :FOX_HEADS],
        real(nsa_kv).reshape(bsz, n_real, 4, NSA_KV_HEADS, HEAD_DIM),
        jnp.concatenate([cache_nsa_win[:, layer, n_real:], win_rows], axis=1),
        ssm, conv,
        xr.reshape(bsz, S5_GROUPS, S5_STATE), xi.reshape(bsz, S5_GROUPS, S5_STATE),
    )
    return x, states


def kernel(x_prompt, x_sample, cache_fox_kv, cache_fox_logf, cache_nsa_kv, cache_nsa_win_kv, state_ssd,
           state_ssd_conv, state_s5_re, state_s5_im, page_table, ffn1_norm, ffn1_w1, ffn1_w3, ffn1_w2, mix_norm,
           w_in, ssd_conv_w, ssd_conv_b, ssd_dt_bias, ssd_a_log, ssd_d, ssd_norm, fox_f_bias, fox_out_norm,
           nsa_cmp_pe, nsa_cmp_w, nsa_out_norm, s5_lambda_re, s5_lambda_im, s5_log_dt, s5_b_re, s5_b_im,
           s5_c_re, s5_c_im, s5_d, s5_w_glu, s5_out_norm, w_out, ffn2_norm, ffn2_w1, ffn2_w3, ffn2_w2, final_norm):
    w = dict(ffn1_norm=ffn1_norm, ffn1_w1=ffn1_w1, ffn1_w3=ffn1_w3, ffn1_w2=ffn1_w2, mix_norm=mix_norm, w_in=w_in,
             ssd_conv_w=ssd_conv_w, ssd_conv_b=ssd_conv_b, ssd_dt_bias=ssd_dt_bias, ssd_a_log=ssd_a_log, ssd_d=ssd_d,
             ssd_norm=ssd_norm, fox_f_bias=fox_f_bias, fox_out_norm=fox_out_norm, nsa_cmp_pe=nsa_cmp_pe,
             nsa_cmp_w=nsa_cmp_w, nsa_out_norm=nsa_out_norm, s5_lambda_re=s5_lambda_re, s5_lambda_im=s5_lambda_im,
             s5_log_dt=s5_log_dt, s5_b_re=s5_b_re, s5_b_im=s5_b_im, s5_c_re=s5_c_re, s5_c_im=s5_c_im, s5_d=s5_d,
             s5_w_glu=s5_w_glu, s5_out_norm=s5_out_norm, w_out=w_out, ffn2_norm=ffn2_norm, ffn2_w1=ffn2_w1,
             ffn2_w3=ffn2_w3, ffn2_w2=ffn2_w2)
    bsz_p, t_len, _ = x_prompt.shape
    bsz_s, n_real, _ = x_sample.shape
    depth = w_in.shape[0]
    q_off = page_table.shape[1] * PAGE
    fg = final_norm[None, :]
    n_pool = cache_fox_kv.shape[0]
    keys_minor = (0, 1, 3, 4, 5, 2)
    caches = (jnp.transpose(cache_fox_kv, keys_minor).reshape(n_pool, depth, 2 * GW, PAGE),
              jnp.transpose(cache_fox_logf, (0, 1, 3, 2)),
              jnp.transpose(cache_nsa_kv, keys_minor).reshape(n_pool, depth, 4 * LANES, PAGE),
              jnp.transpose(cache_nsa_win_kv, keys_minor).reshape(bsz_s, depth, 2 * LANES, WINDOW),
              cache_nsa_win_kv, state_ssd, state_ssd_conv, state_s5_re, state_s5_im)
    xp = x_prompt.reshape(bsz_p * t_len, D_MODEL)
    xs = jnp.pad(x_sample, ((0, 0), (0, T_PAD - n_real), (0, 0))).reshape(bsz_s * T_PAD, D_MODEL)
    st_p, st_s = [], []
    for l in range(depth):
        lw = _layer_weights(w, l)
        last = l == depth - 1
        xp, sp = _prompt_layer(xp, lw, fg, layer=l, bsz=bsz_p, t_len=t_len, last=last)
        xs, ss = _sample_layer(xs, lw, fg, caches, page_table, layer=l, bsz=bsz_s, n_real=n_real, q_off=q_off,
                               last=last)
        st_p.append(sp)
        st_s.append(ss)
    y_p = xp.reshape(bsz_p, t_len, D_MODEL)
    y_s = xs.reshape(bsz_s, T_PAD, D_MODEL)[:, :n_real]
    out = [y_p, y_s]
    for i in range(8):
        out.append(jnp.stack([s[i] for s in st_p], axis=1))
        out.append(jnp.stack([s[i] for s in st_s], axis=1))
    return tuple(out)
```

```python
import functools
import math

import jax
import jax.numpy as jnp
from jax import lax
from jax.experimental import pallas as pl
from jax.experimental.pallas import tpu as pltpu

F32 = jnp.float32
BF16 = jnp.bfloat16
HIGHEST = lax.Precision.HIGHEST

D_MODEL = 2048
DEPTH = 2
HEAD_DIM = 64
GW = D_MODEL // 4
D_FF = ((8 * D_MODEL // 3 + 127) // 128) * 128
EPS = 1e-6
TINY = 1e-30
SSD_HEADS = GW // HEAD_DIM
SSD_GROUPS = 2
SSD_STATE = 64
CONV_W = 4
SSD_CONV_DIM = GW + 2 * SSD_GROUPS * SSD_STATE
SSD_CHUNK = 128
FOX_HEADS = GW // HEAD_DIM
NSA_HEADS = GW // HEAD_DIM
NSA_KV_HEADS = 2
NSA_GROUP = NSA_HEADS // NSA_KV_HEADS
CMP_BLOCK = 32
SLC_BLOCK = 64
TOP_N = 16
WINDOW = 512
FORCE_SCORE = 1e4
S5_CH = 16
S5_GROUPS = GW // S5_CH
S5_STATE = 64
S5_N = S5_GROUPS * S5_STATE
PAGE = 128

LANES = 128
SUBLANES = 8
VMEM_LIMIT = 56 * 1024 * 1024

C_XBC = 0
C_DT = 768
C_FF = 896
C_Z = 1024
C_U = 1536
C_FQ = 2048
C_FK = 2560
C_FV = 3072
C_NQ = 3584
C_NKV = 4096
C_NG = 4864
P_W = 5120
FF_TILE = 256
FFN_ROWS = 1024


def _cparams(sem):
    return pltpu.CompilerParams(dimension_semantics=sem, vmem_limit_bytes=VMEM_LIMIT)


def _rms(x, g):
    ms = jnp.mean(x * x, axis=-1, keepdims=True)
    return x * lax.rsqrt(ms + EPS) * g


def _sigmoid(x):
    return 1.0 / (1.0 + jnp.exp(-x))


def _silu(x):
    return x * _sigmoid(x)


def _softplus(x):
    return jnp.maximum(x, 0.0) + jnp.log(1.0 + jnp.exp(-jnp.abs(x)))


def _log_sigmoid(x):
    return jnp.minimum(x, 0.0) - jnp.log(1.0 + jnp.exp(-jnp.abs(x)))


def _dot(a, b):
    return jnp.dot(a, b, preferred_element_type=F32)


def _dot_nt(a, b):
    return lax.dot_general(a, b, (((1,), (1,)), ((), ())), preferred_element_type=F32)


def _dot_tn(a, b):
    return lax.dot_general(a, b, (((0,), (0,)), ((), ())), preferred_element_type=F32)


def _ffn_body(x_ref, g_ref, w1_ref, w3_ref, w2_ref, fg_ref, o_ref, h_ref, acc_ref, *, final_norm, nk, tf):
    k = pl.program_id(1)
    tail = D_FF - (nk - 1) * tf

    @pl.when(k == 0)
    def _():
        h_ref[...] = _rms(x_ref[...], g_ref[...]).astype(BF16)
        acc_ref[...] = jnp.zeros_like(acc_ref)

    def partial_sum(width):
        h = h_ref[...]
        a = _dot(h, w1_ref[:, :width].astype(BF16))
        b = _dot(h, w3_ref[:, :width].astype(BF16))
        return _dot((_silu(a) * b).astype(BF16), w2_ref[:width, :].astype(BF16))

    @pl.when(k < nk - 1)
    def _():
        acc_ref[...] += partial_sum(tf)

    @pl.when(k == nk - 1)
    def _():
        y = x_ref[...] + 0.5 * (acc_ref[...] + partial_sum(tail))
        if final_norm:
            y = _rms(y, fg_ref[...])
        o_ref[...] = y


def ffn_call(x, g, w1, w3, w2, fg, *, tm, layer, final_norm, tf=FF_TILE):
    m = x.shape[0]
    nk = pl.cdiv(D_FF, tf)
    assert m % tm == 0 and w1.shape[1:] == (D_MODEL, D_FF) and w2.shape[1:] == (D_FF, D_MODEL)
    once = pl.Buffered(1)
    return pl.pallas_call(
        functools.partial(_ffn_body, final_norm=final_norm, nk=nk, tf=tf),
        grid=(m // tm, nk),
        in_specs=[
            pl.BlockSpec((tm, D_MODEL), lambda i, k: (i, 0), pipeline_mode=once),
            pl.BlockSpec((1, D_MODEL), lambda i, k: (0, 0)),
            pl.BlockSpec((None, D_MODEL, tf), lambda i, k: (layer, 0, k)),
            pl.BlockSpec((None, D_MODEL, tf), lambda i, k: (layer, 0, k)),
            pl.BlockSpec((None, tf, D_MODEL), lambda i, k: (layer, k, 0)),
            pl.BlockSpec((1, D_MODEL), lambda i, k: (0, 0)),
        ],
        out_specs=pl.BlockSpec((tm, D_MODEL), lambda i, k: (i, 0), pipeline_mode=once),
        out_shape=jax.ShapeDtypeStruct((m, D_MODEL), F32),
        scratch_shapes=[pltpu.VMEM((tm, D_MODEL), BF16), pltpu.VMEM((tm, D_MODEL), F32)],
        compiler_params=_cparams(("parallel", "arbitrary")),
        name="ffn",
    )(x, g, w1, w3, w2, fg)


IN_TILE = P_W // 2


def _inproj_body(x_ref, g_ref, w_ref, o_ref, fkv_ref, nkv_ref, win_ref, h_ref, *, keys_minor):
    j = pl.program_id(1)

    @pl.when(j == 0)
    def _():
        h_ref[...] = _rms(x_ref[...], g_ref[...]).astype(BF16)
        o_ref[...] = _dot_nt(h_ref[...], w_ref[...])

    @pl.when(j == 1)
    def _():
        y = _dot_nt(h_ref[...], w_ref[...])
        o_ref[...] = y
        fkv = y[:, C_FK - IN_TILE:C_FK - IN_TILE + 2 * GW]
        nkv = y[:, C_NKV - IN_TILE:C_NKV - IN_TILE + 4 * LANES]
        win = y[:, C_NKV - IN_TILE + 4 * LANES:C_NKV - IN_TILE + 6 * LANES]
        fkv_ref[...] = fkv.T if keys_minor else fkv
        nkv_ref[...] = nkv.T if keys_minor else nkv
        win_ref[...] = win.T if keys_minor else win


def inproj_call(x, g, w, *, tm, keys_minor=None):
    m = x.shape[0]
    assert m % tm == 0 and IN_TILE <= C_FK
    widths = (2 * GW, 4 * LANES, 2 * LANES)
    if keys_minor is None:
        side_specs = [pl.BlockSpec((tm, wd), lambda i, j: (i, 0)) for wd in widths]
        side_shapes = [jax.ShapeDtypeStruct((m, wd), F32) for wd in widths]
    else:
        bsz, t_len = keys_minor
        nt = t_len // tm
        assert t_len % tm == 0 and m == bsz * t_len
        side_specs = [pl.BlockSpec((None, wd, tm), lambda i, j: (i // nt, 0, i % nt)) for wd in widths]
        side_shapes = [jax.ShapeDtypeStruct((bsz, wd, t_len), F32) for wd in widths]
    return pl.pallas_call(
        functools.partial(_inproj_body, keys_minor=keys_minor is not None),
        grid=(m // tm, P_W // IN_TILE),
        in_specs=[
            pl.BlockSpec((tm, D_MODEL), lambda i, j: (i, 0)),
            pl.BlockSpec((1, D_MODEL), lambda i, j: (0, 0)),
            pl.BlockSpec((IN_TILE, D_MODEL), lambda i, j: (j, 0)),
        ],
        out_specs=[pl.BlockSpec((tm, IN_TILE), lambda i, j: (i, j))] + side_specs,
        out_shape=[jax.ShapeDtypeStruct((m, P_W), F32)] + side_shapes,
        scratch_shapes=[pltpu.VMEM((tm, D_MODEL), BF16)],
        compiler_params=_cparams(("parallel", "arbitrary")),
        name="inproj",
    )(x, g, w)


def _outproj_body(x_ref, a_ref, b_ref, c_ref, d_ref, gn_ref, w_ref, o_ref):
    y = x_ref[...]
    for i, r in enumerate((a_ref, b_ref, c_ref, d_ref)):
        y = y + _dot(_rms(r[...], gn_ref[i:i + 1, :]).astype(BF16), w_ref[i * GW:(i + 1) * GW, :].astype(BF16))
    o_ref[...] = y


def outproj_call(x, ya, yb, yc, yd, gains, w, *, tm, layer):
    m = x.shape[0]
    assert m % tm == 0
    yspec = pl.BlockSpec((tm, GW), lambda i: (i, 0))
    return pl.pallas_call(
        _outproj_body,
        grid=(m // tm,),
        in_specs=[
            pl.BlockSpec((tm, D_MODEL), lambda i: (i, 0)),
            yspec, yspec, yspec, yspec,
            pl.BlockSpec((4, GW), lambda i: (0, 0)),
            pl.BlockSpec((None, D_MODEL, D_MODEL), lambda i: (layer, 0, 0), pipeline_mode=pl.Buffered(1)),
        ],
        out_specs=pl.BlockSpec((tm, D_MODEL), lambda i: (i, 0)),
        out_shape=jax.ShapeDtypeStruct((m, D_MODEL), F32),
        compiler_params=_cparams(("parallel",)),
        name="outproj",
    )(x, ya, yb, yc, yd, gains, w)


def _ssd_body(xbc_ref, dt_ref, z_ref, conv0_ref, ssm0_ref, cw_ref, cb_ref, dtb_ref, alog_ref, dd_ref,
              y_ref, ssm_ref, conv_ref, xp_ref, act_ref, st_ref, *, n_real):
    c = pl.program_id(1)
    nc = pl.num_programs(1)
    q = SSD_CHUNK
    halo = SUBLANES

    @pl.when(c == 0)
    def _():
        xp_ref[halo - 3:halo, :] = conv0_ref[...]
        st_ref[...] = ssm0_ref[...]

    xr = xbc_ref[...]
    xp_ref[halo:halo + q, :] = xr
    conv = (cb_ref[...] + cw_ref[3:4, :] * xr
            + cw_ref[2:3, :] * xp_ref[halo - 1:halo - 1 + q, :]
            + cw_ref[1:2, :] * xp_ref[halo - 2:halo - 2 + q, :]
            + cw_ref[0:1, :] * xp_ref[halo - 3:halo - 3 + q, :])
    act_ref[...] = _silu(conv)

    row = lax.broadcasted_iota(jnp.int32, (q, LANES), 0)
    dt = jnp.where(row < n_real, _softplus(dt_ref[...] + dtb_ref[...]), 0.0)
    a = -jnp.exp(alog_ref[...])
    ti = lax.broadcasted_iota(jnp.int32, (q, q), 0)
    si = lax.broadcasted_iota(jnp.int32, (q, q), 1)
    causal = si <= ti
    acs = jnp.dot(causal.astype(F32), dt * a, preferred_element_type=F32, precision=HIGHEST)
    acs_t = acs.T
    e_acs = jnp.exp(acs)
    acs_last = acs[q - 1:q, :]
    w_end = jnp.exp(acs_last - acs) * dt
    e_last = jnp.exp(acs_last)

    for g in range(SSD_GROUPS):
        bm = act_ref[:, GW + g * SSD_STATE:GW + (g + 1) * SSD_STATE]
        cm = act_ref[:, GW + (SSD_GROUPS + g) * SSD_STATE:GW + (SSD_GROUPS + g + 1) * SSD_STATE]
        bm16 = bm.astype(BF16)
        cm16 = cm.astype(BF16)
        cb = _dot_nt(cm16, bm16)
        for hh in range(SSD_HEADS // SSD_GROUPS):
            h = g * (SSD_HEADS // SSD_GROUPS) + hh
            xs = act_ref[:, h * HEAD_DIM:(h + 1) * HEAD_DIM]
            seg = acs[:, h:h + 1] - acs_t[h:h + 1, :]
            decay = jnp.exp(jnp.where(causal, seg, -jnp.inf))
            y = _dot((cb * decay).astype(BF16), (xs * dt[:, h:h + 1]).astype(BF16))
            s_in = st_ref[h]
            y = y + _dot_nt(cm16, s_in.astype(BF16)) * e_acs[:, h:h + 1]
            y = y + dd_ref[:, h:h + 1] * xs
            cs = _dot_tn((xs * w_end[:, h:h + 1]).astype(BF16), bm16)
            st_ref[h] = e_last[:, h:h + 1] * s_in + cs
            zs = z_ref[:, h * HEAD_DIM:(h + 1) * HEAD_DIM]
            y_ref[:, h * HEAD_DIM:(h + 1) * HEAD_DIM] = y * _silu(zs)

    last_real = min(n_real, q)
    conv_ref[...] = xp_ref[halo + last_real - 3:halo + last_real, :]
    xp_ref[halo - 3:halo, :] = xp_ref[halo + q - 3:halo + q, :]

    @pl.when(c == nc - 1)
    def _():
        ssm_ref[...] = st_ref[...]


def ssd_call(p, conv0, ssm0, cw, cb, dtb, alog, dd, *, bsz, t_len, n_real):
    q = SSD_CHUNK
    nc = t_len // q
    assert t_len % q == 0 and (nc == 1 or n_real == q)
    row = lambda b, c: b * nc + c
    vec = lambda shape: pl.BlockSpec(shape, lambda b, c: (0, 0))
    return pl.pallas_call(
        functools.partial(_ssd_body, n_real=n_real),
        grid=(bsz, nc),
        in_specs=[
            pl.BlockSpec((q, SSD_CONV_DIM), lambda b, c: (row(b, c), C_XBC // SSD_CONV_DIM)),
            pl.BlockSpec((q, LANES), lambda b, c: (row(b, c), C_DT // LANES)),
            pl.BlockSpec((q, GW), lambda b, c: (row(b, c), C_Z // GW)),
            pl.BlockSpec((None, CONV_W - 1, SSD_CONV_DIM), lambda b, c: (b, 0, 0)),
            pl.BlockSpec((None, SSD_HEADS, HEAD_DIM, SSD_STATE), lambda b, c: (b, 0, 0, 0)),
            vec((CONV_W, SSD_CONV_DIM)), vec((1, SSD_CONV_DIM)), vec((1, LANES)), vec((1, LANES)), vec((1, LANES)),
        ],
        out_specs=[
            pl.BlockSpec((q, GW), lambda b, c: (row(b, c), 0)),
            pl.BlockSpec((None, SSD_HEADS, HEAD_DIM, SSD_STATE), lambda b, c: (b, 0, 0, 0)),
            pl.BlockSpec((None, CONV_W - 1, SSD_CONV_DIM), lambda b, c: (b, 0, 0)),
        ],
        out_shape=[
            jax.ShapeDtypeStruct((bsz * t_len, GW), F32),
            jax.ShapeDtypeStruct((bsz, SSD_HEADS, HEAD_DIM, SSD_STATE), F32),
            jax.ShapeDtypeStruct((bsz, CONV_W - 1, SSD_CONV_DIM), F32),
        ],
        scratch_shapes=[
            pltpu.VMEM((SUBLANES + q, SSD_CONV_DIM), F32),
            pltpu.VMEM((q, SSD_CONV_DIM), F32),
            pltpu.VMEM((SSD_HEADS, HEAD_DIM, SSD_STATE), F32),
        ],
        compiler_params=_cparams(("parallel", "arbitrary")),
        name="ssd",
    )(p, p, p, conv0, ssm0, cw, cb, dtb, alog, dd)


def _s5_body(u_ref, bre_ref, bim_ref, are_ref, aim_ref, x0r_ref, x0i_ref, cre_ref, cim_ref, d_ref, wg_ref,
             o_ref, xr_out, xi_out, xr_ref, xi_ref, sr_ref, si_ref, mix_ref, *, nb, steps, t_last):
    c = pl.program_id(0)
    per = SUBLANES // nb
    tiles = steps * nb // SUBLANES
    n_chunk = GW // LANES

    @pl.when(c == 0)
    def _():
        sr_ref[...] = x0r_ref[...]
        si_ref[...] = x0i_ref[...]

    for b in range(nb):
        for cc in range(n_chunk):
            mix_ref[cc, pl.ds(b, steps, stride=nb), :] = u_ref[b, :, cc * LANES:(cc + 1) * LANES]
    u = jnp.concatenate([mix_ref[cc] for cc in range(n_chunk)], axis=1)
    u16 = u.astype(BF16)
    xr_ref[...] = _dot(u16, bre_ref[...])
    xi_ref[...] = _dot(u16, bim_ref[...])
    ar = are_ref[...]
    ai = aim_ref[...]
    first = lax.broadcasted_iota(jnp.int32, (SUBLANES, S5_N), 0) < nb

    def step(j, carry):
        sr, si = carry
        rows = pl.ds(pl.multiple_of(j * SUBLANES, SUBLANES), SUBLANES)
        br = xr_ref[rows, :]
        bi = xi_ref[rows, :]
        vr = ar * sr - ai * si + br
        vi = ar * si + ai * sr + bi
        if per == 2:
            pr = pltpu.roll(vr, nb, 0)
            pi = pltpu.roll(vi, nb, 0)
            wr = ar * pr - ai * pi + br
            wi = ar * pi + ai * pr + bi
            outr = jnp.where(first, vr, wr)
            outi = jnp.where(first, vi, wi)
            nxt = (pltpu.roll(wr, nb, 0), pltpu.roll(wi, nb, 0))
        else:
            outr, outi, nxt = vr, vi, (vr, vi)
        xr_ref[rows, :] = outr
        xi_ref[rows, :] = outi

        @pl.when(c * tiles + j == t_last // per)
        def _():
            xr_out[...] = outr
            xi_out[...] = outi

        return nxt

    sr, si = lax.fori_loop(0, tiles, step, (sr_ref[...], si_ref[...]))
    sr_ref[...] = sr
    si_ref[...] = si

    y = _dot(xr_ref[...].astype(BF16), cre_ref[...]) - _dot(xi_ref[...].astype(BF16), cim_ref[...])
    y = y + d_ref[...] * u
    g = _dot(jax.nn.gelu(y).astype(BF16), wg_ref[...])
    o = g[:, :GW] * _sigmoid(g[:, GW:])
    for cc in range(n_chunk):
        mix_ref[cc] = o[:, cc * LANES:(cc + 1) * LANES]
    for b in range(nb):
        o_ref[b] = jnp.concatenate([mix_ref[cc, pl.ds(b, steps, stride=nb), :] for cc in range(n_chunk)], axis=1)


def s5_call(p3, bre, bim, are, aim, x0r, x0i, cre, cim, d, wg, *, t_len, steps, t_last):
    nb = p3.shape[0]
    assert t_len % steps == 0 and nb in (4, 8) and (steps * nb) % SUBLANES == 0
    rows = steps * nb
    const = lambda shape: pl.BlockSpec(shape, lambda c: (0, 0))
    return pl.pallas_call(
        functools.partial(_s5_body, nb=nb, steps=steps, t_last=t_last),
        grid=(t_len // steps,),
        in_specs=[
            pl.BlockSpec((nb, steps, GW), lambda c: (0, c, C_U // GW)),
            const((GW, S5_N)), const((GW, S5_N)), const((1, S5_N)), const((1, S5_N)),
            const((SUBLANES, S5_N)), const((SUBLANES, S5_N)),
            const((S5_N, GW)), const((S5_N, GW)), const((1, GW)), const((GW, 2 * GW)),
        ],
        out_specs=[pl.BlockSpec((nb, steps, GW), lambda c: (0, c, 0)), const((SUBLANES, S5_N)), const((SUBLANES, S5_N))],
        out_shape=[jax.ShapeDtypeStruct((nb, t_len, GW), F32),
                   jax.ShapeDtypeStruct((SUBLANES, S5_N), F32), jax.ShapeDtypeStruct((SUBLANES, S5_N), F32)],
        scratch_shapes=[pltpu.VMEM((rows, S5_N), F32), pltpu.VMEM((rows, S5_N), F32),
                        pltpu.VMEM((SUBLANES, S5_N), F32), pltpu.VMEM((SUBLANES, S5_N), F32),
                        pltpu.VMEM((GW // LANES, rows, LANES), F32)],
        compiler_params=_cparams(("arbitrary",)),
        name="s5",
    )(p3, bre, bim, are, aim, x0r, x0i, cre, cim, d, wg)


def s5_params(lam_re, lam_im, log_dt, b_re, b_im, c_re, c_im):
    dt = jnp.exp(log_dt)[:, None]
    mag = jnp.exp(lam_re * dt)
    ab_re = mag * jnp.cos(lam_im * dt)
    ab_im = mag * jnp.sin(lam_im * dt)
    den = lam_re * lam_re + lam_im * lam_im
    zr = ((ab_re - 1.0) * lam_re + ab_im * lam_im) / den
    zi = (ab_im * lam_re - (ab_re - 1.0) * lam_im) / den
    bb_re = zr[..., None] * b_re - zi[..., None] * b_im
    bb_im = zr[..., None] * b_im + zi[..., None] * b_re
    eye = jnp.eye(S5_GROUPS, dtype=F32)

    def in_mat(bb):
        return jnp.einsum('gnc,gh->gchn', bb, eye).reshape(GW, S5_N).astype(BF16)

    def out_mat(cc):
        return jnp.einsum('gcn,gh->gnhc', cc, eye).reshape(S5_N, GW).astype(BF16)

    return (in_mat(bb_re), in_mat(bb_im), ab_re.reshape(1, S5_N), ab_im.reshape(1, S5_N),
            out_mat(c_re), out_mat(c_im))


LOG2E = 1.4426950408889634
NEG = -1e30


def _fox_prep_body(f_ref, b_ref, lf_ref, cumt_ref, carry_ref):
    c = pl.program_id(1)

    @pl.when(c == 0)
    def _():
        carry_ref[...] = jnp.zeros_like(carry_ref)

    lf = _log_sigmoid(f_ref[...] + b_ref[...])
    lf_ref[...] = lf.T[:FOX_HEADS, :]
    tc = lf.shape[0]
    ti = lax.broadcasted_iota(jnp.int32, (tc, tc), 0)
    si = lax.broadcasted_iota(jnp.int32, (tc, tc), 1)
    cum = jnp.dot((si <= ti).astype(F32), lf, preferred_element_type=F32, precision=HIGHEST) + carry_ref[...]
    cumt_ref[...] = cum.T[:FOX_HEADS, :]
    carry_ref[...] = cum[tc - 1:tc, :]


def fox_prep_call(p, bias, *, bsz, t_len, tc=256):
    nc = t_len // tc
    assert t_len % tc == 0
    head_rows = pl.BlockSpec((None, FOX_HEADS, tc), lambda b, c: (b, 0, c))
    return pl.pallas_call(
        _fox_prep_body,
        grid=(bsz, nc),
        in_specs=[pl.BlockSpec((tc, LANES), lambda b, c: (b * nc + c, C_FF // LANES)),
                  pl.BlockSpec((1, LANES), lambda b, c: (0, 0))],
        out_specs=[head_rows, head_rows],
        out_shape=[jax.ShapeDtypeStruct((bsz, FOX_HEADS, t_len), F32),
                   jax.ShapeDtypeStruct((bsz, FOX_HEADS, t_len), F32)],
        scratch_shapes=[pltpu.VMEM((1, LANES), F32)],
        compiler_params=_cparams(("parallel", "arbitrary")),
        name="fox_prep",
    )(p, bias)


FOX_PAIRS_PER_STEP = 4


def _fox_attn_body(q_ref, k_ref, v_ref, cumt_ref, o_ref, *, tq, tk):
    hq = pl.program_id(1)
    qi = pl.program_id(2)
    q0 = qi * tq
    npair = FOX_PAIRS_PER_STEP
    lane = lax.broadcasted_iota(jnp.int32, (tq, LANES), 1)
    t_pos = q0 + lax.broadcasted_iota(jnp.int32, (tq, tk), 0)
    s_off = lax.broadcasted_iota(jnp.int32, (tq, tk), 1)
    qs = []
    for pp in range(npair):
        q = q_ref[:, pp * LANES:(pp + 1) * LANES] * (HEAD_DIM ** -0.5 * LOG2E)
        qs.append(jnp.concatenate([jnp.where(lane < HEAD_DIM, q, 0.0), jnp.where(lane >= HEAD_DIM, q, 0.0)],
                                  axis=0).astype(BF16))

    def chunk(kc, carry, masked):
        ks = pl.ds(pl.multiple_of(kc * tk, tk), tk)
        out = []
        for pp in range(npair):
            s = _dot_nt(qs[pp], k_ref[ks, pp * LANES:(pp + 1) * LANES].astype(BF16))
            vv = v_ref[ks, pp * LANES:(pp + 1) * LANES].astype(BF16)
            for hh in range(2):
                m, l, acc = carry[2 * pp + hh]
                head = 2 * (hq * npair + pp) + hh
                sh = s[hh * tq:(hh + 1) * tq] - cumt_ref[pl.ds(head, 1), ks] * LOG2E
                if masked:
                    sh = jnp.where(kc * tk + s_off <= t_pos, sh, NEG)
                m_new = jnp.maximum(m, jnp.max(sh, axis=1, keepdims=True))
                alpha = jnp.exp2(m - m_new)
                pr = jnp.exp2(sh - m_new)
                l = alpha * l + jnp.sum(pr, axis=1, keepdims=True)
                acc = alpha * acc + _dot(pr.astype(BF16), vv)
                out.append((m_new, l, acc))
        return tuple(out)

    init = tuple((jnp.full((tq, 1), NEG, F32), jnp.zeros((tq, 1), F32), jnp.zeros((tq, LANES), F32))
                 for _ in range(2 * npair))
    n_full = q0 // tk
    carry = lax.fori_loop(0, n_full, lambda kc, c: chunk(kc, c, False), init)
    res = chunk(n_full, carry, True)
    for pp in range(npair):
        (_, la, acca), (_, lb, accb) = res[2 * pp], res[2 * pp + 1]
        o_ref[:, pp * LANES:(pp + 1) * LANES] = jnp.where(lane < HEAD_DIM, acca / la, accb / lb)


def fox_attn_call(p, cumt, *, bsz, t_len, tq=128, tk=512):
    tk = min(tk, t_len)
    nq = t_len // tq
    assert t_len % tk == 0 and tk % tq == 0
    width = FOX_PAIRS_PER_STEP * LANES
    assert GW % width == 0
    return pl.pallas_call(
        functools.partial(_fox_attn_body, tq=tq, tk=tk),
        grid=(bsz, GW // width, nq),
        in_specs=[
            pl.BlockSpec((tq, width), lambda b, hq, qi: (b * nq + qi, C_FQ // width + hq)),
            pl.BlockSpec((t_len, width), lambda b, hq, qi: (b, C_FK // width + hq)),
            pl.BlockSpec((t_len, width), lambda b, hq, qi: (b, C_FV // width + hq)),
            pl.BlockSpec((None, FOX_HEADS, t_len), lambda b, hq, qi: (b, 0, 0)),
        ],
        out_specs=pl.BlockSpec((tq, width), lambda b, hq, qi: (b * nq + qi, hq)),
        out_shape=jax.ShapeDtypeStruct((bsz * t_len, GW), F32),
        compiler_params=_cparams(("parallel", "parallel", "arbitrary")),
        name="fox_attn",
    )(p, p, p, cumt)


def _compress_rows(xk_ref, xv_ref, pe_ref, w_ref, nbc):
    half = nbc // 2
    acc = jnp.zeros((nbc, 2 * LANES), F32)
    for r in range(CMP_BLOCK):
        ev = pl.ds(r, half, stride=2 * CMP_BLOCK)
        od = pl.ds(CMP_BLOCK + r, half, stride=2 * CMP_BLOCK)
        rows = jnp.concatenate([jnp.concatenate([xk_ref[ev, :], xv_ref[ev, :]], axis=1),
                                jnp.concatenate([xk_ref[od, :], xv_ref[od, :]], axis=1)], axis=0)
        acc = acc + _dot((rows + pe_ref[r:r + 1, :]).astype(BF16), w_ref[r])
    return acc


def _nsa_cmp_p_body(xk_ref, xv_ref, pe_ref, w_ref, o_ref, *, nbc, hp):
    half = nbc // 2
    acc = _compress_rows(xk_ref, xv_ref, pe_ref, w_ref, nbc)
    o_ref[...] = jnp.zeros_like(o_ref)
    o_ref[0:half, :] = acc[0:half]
    o_ref[hp:hp + half, :] = acc[half:nbc]


def nsa_cmp_p_call(p, pe4, w4, *, bsz, t_len):
    nbc = t_len // CMP_BLOCK
    hp = max(nbc // 2, HEAD_DIM)
    return pl.pallas_call(
        functools.partial(_nsa_cmp_p_body, nbc=nbc, hp=hp),
        grid=(bsz,),
        in_specs=[pl.BlockSpec((t_len, LANES), lambda b: (b, C_NKV // LANES)),
                  pl.BlockSpec((t_len, LANES), lambda b: (b, C_NKV // LANES + 1)),
                  pl.BlockSpec((CMP_BLOCK, 2 * LANES), lambda b: (0, 0)),
                  pl.BlockSpec((CMP_BLOCK, 2 * LANES, 2 * LANES), lambda b: (0, 0, 0))],
        out_specs=pl.BlockSpec((None, 2 * hp, 2 * LANES), lambda b: (b, 0, 0)),
        out_shape=jax.ShapeDtypeStruct((bsz, 2 * hp, 2 * LANES), F32),
        compiler_params=_cparams(("parallel",)),
        name="nsa_cmp_p",
    )(p, p, pe4, w4)


def _cmp_attend(qs, cmp_ref, g, t_pos, blk, slot_ok):
    scale = HEAD_DIM ** -0.5
    kc = cmp_ref[:, g * HEAD_DIM:(g + 1) * HEAD_DIM].astype(BF16)
    vc = cmp_ref[:, 2 * HEAD_DIM + g * HEAD_DIM:2 * HEAD_DIM + (g + 1) * HEAD_DIM].astype(BF16)
    s = _dot_nt(qs, kc) * scale
    valid = slot_ok & ((blk + 1) * CMP_BLOCK - 1 <= t_pos)
    s = jnp.where(valid, s, -jnp.inf)
    m = jnp.max(s, axis=1, keepdims=True)
    m = jnp.where(m > -jnp.inf, m, 0.0)
    e = jnp.where(valid, jnp.exp(s - m), 0.0)
    pc = e / jnp.maximum(jnp.sum(e, axis=1, keepdims=True), TINY)
    return _dot(pc.astype(BF16), vc), pc


def _select_score(imp, t_pos, blk):
    cur = t_pos // SLC_BLOCK
    forced = (blk == 0) | (blk == cur) | (blk == cur - 1)
    avail = blk * SLC_BLOCK <= t_pos
    return jnp.where(avail, jnp.where(forced, FORCE_SCORE, imp), -1.0)


def _nsa_attn_p_body(q_ref, cmp_ref, ks_ref, vs_ref, kw_ref, vw_ref, gl_ref, o_ref, *, tq, tk, nbc, hp):
    qi = pl.program_id(1)
    hg_n = NSA_GROUP
    rows = hg_n * tq
    half = nbc // 2
    nbs = (nbc + 1) // 2
    q0 = qi * tq
    scale = HEAD_DIM ** -0.5
    t_col = q0 + lax.broadcasted_iota(jnp.int32, (tq, 1), 0)
    t_lane = q0 + lax.broadcasted_iota(jnp.int32, (1, tq), 1)
    t_lane_stack = jnp.concatenate([t_lane] * hg_n, axis=1)
    gate = _sigmoid(gl_ref[...])
    crow = lax.broadcasted_iota(jnp.int32, (2 * hp, 1), 0)
    slot = jnp.where(crow < hp, crow, crow - hp)
    cmp_blk = 2 * slot + jnp.where(crow < hp, 0, 1)
    slc_blk = lax.broadcasted_iota(jnp.int32, (hp, 1), 0)

    def add_bias(s, bias):
        n = s.shape[1]
        return (s.reshape(hg_n, tq, n) + bias[None]).reshape(rows, n)

    per_g = []
    for g in range(NSA_KV_HEADS):
        q_f32 = jnp.concatenate(
            [q_ref[:, (g * hg_n + hg) * HEAD_DIM:(g * hg_n + hg + 1) * HEAD_DIM] for hg in range(hg_n)], axis=0)
        qs = q_f32.astype(BF16)
        qs2 = (q_f32 * (scale * LOG2E)).astype(BF16)

        kc = cmp_ref[:, g * HEAD_DIM:(g + 1) * HEAD_DIM].astype(BF16)
        vc = cmp_ref[:, 2 * HEAD_DIM + g * HEAD_DIM:2 * HEAD_DIM + (g + 1) * HEAD_DIM].astype(BF16)
        st = _dot_nt(kc, qs) * scale
        valid = (slot < half) & ((cmp_blk + 1) * CMP_BLOCK - 1 <= t_lane_stack)
        st = jnp.where(valid, st, -jnp.inf)
        m = jnp.max(st, axis=0, keepdims=True)
        m = jnp.where(m > -jnp.inf, m, 0.0)
        e = jnp.where(valid, jnp.exp(st - m), 0.0)
        pt = e / jnp.maximum(jnp.sum(e, axis=0, keepdims=True), TINY)
        o_cmp = _dot_tn(pt.astype(BF16), vc)

        imp = pt[:, 0:tq]
        for hg in range(1, hg_n):
            imp = imp + pt[:, hg * tq:(hg + 1) * tq]
        imp = imp[0:hp] + imp[hp:2 * hp]
        score = _select_score(imp, t_lane, slc_blk)
        rank = jnp.zeros((hp, tq), jnp.int32)
        for i in range(nbs):
            row = score[i:i + 1, :]
            rank = rank + ((row > score) | ((row == score) & (i < slc_blk))).astype(jnp.int32)
        sel_t = ((rank < TOP_N) & (score >= 0.0)).astype(BF16)

        per_g.append((qs2, o_cmp, sel_t))

    def slc_chunk(kc_i, carry):
        ks = pl.ds(pl.multiple_of(kc_i * tk, tk), tk)
        jb = lax.broadcasted_iota(jnp.int32, (hp, tk), 0)
        sp = kc_i * tk + lax.broadcasted_iota(jnp.int32, (hp, tk), 1)
        expand = (jb == sp // SLC_BLOCK).astype(BF16)
        causal = kc_i * tk + lax.broadcasted_iota(jnp.int32, (tq, tk), 1) <= t_col
        out = []
        for g in range(NSA_KV_HEADS):
            qs2, _, sel_t = per_g[g]
            m, l, acc = carry[g]
            hit = _dot_tn(sel_t, expand) > 0.5
            bias = jnp.where(hit & causal, 0.0, NEG)
            s = add_bias(_dot_nt(qs2, ks_ref[ks, g * HEAD_DIM:(g + 1) * HEAD_DIM].astype(BF16)), bias)
            m_new = jnp.maximum(m, jnp.max(s, axis=1, keepdims=True))
            alpha = jnp.exp2(m - m_new)
            pr = jnp.exp2(s - m_new)
            l = alpha * l + jnp.sum(pr, axis=1, keepdims=True)
            acc = alpha * acc + _dot(pr.astype(BF16), vs_ref[ks, g * HEAD_DIM:(g + 1) * HEAD_DIM].astype(BF16))
            out.append((m_new, l, acc))
        return tuple(out)

    init = tuple((jnp.full((rows, 1), NEG, F32), jnp.zeros((rows, 1), F32), jnp.zeros((rows, HEAD_DIM), F32))
                 for _ in range(NSA_KV_HEADS))
    slc = lax.fori_loop(0, (q0 + tq + tk - 1) // tk, slc_chunk, init)

    span = WINDOW + tq
    start = pl.multiple_of(jnp.maximum(q0 - WINDOW, 0), tq)
    ws = pl.ds(start, span)
    diff = t_col - (start + lax.broadcasted_iota(jnp.int32, (tq, span), 1))
    win_bias = jnp.where((diff >= 0) & (diff < WINDOW), 0.0, NEG)
    for g in range(NSA_KV_HEADS):
        qs2, o_cmp, _ = per_g[g]
        _, l, acc = slc[g]
        o_slc = acc / l
        s = add_bias(_dot_nt(qs2, kw_ref[ws, g * HEAD_DIM:(g + 1) * HEAD_DIM].astype(BF16)), win_bias)
        pr = jnp.exp2(s - jnp.max(s, axis=1, keepdims=True))
        o_win = (_dot(pr.astype(BF16), vw_ref[ws, g * HEAD_DIM:(g + 1) * HEAD_DIM].astype(BF16))
                 / jnp.sum(pr, axis=1, keepdims=True))
        for hg in range(hg_n):
            h = g * hg_n + hg
            rs = slice(hg * tq, (hg + 1) * tq)
            o = (gate[:, 3 * h:3 * h + 1] * o_cmp[rs] + gate[:, 3 * h + 1:3 * h + 2] * o_slc[rs]
                 + gate[:, 3 * h + 2:3 * h + 3] * o_win[rs])
            o_ref[:, h * HEAD_DIM:(h + 1) * HEAD_DIM] = o


def nsa_attn_p_call(p, cmp, *, bsz, t_len, tq=128, tk=512):
    nq = t_len // tq
    nbc = t_len // CMP_BLOCK
    hp = cmp.shape[1] // 2
    assert t_len % tk == 0 and WINDOW % tq == 0 and t_len >= WINDOW + tq
    kv = lambda off: pl.BlockSpec((t_len, LANES), lambda b, qi: (b, (C_NKV + off) // LANES))
    return pl.pallas_call(
        functools.partial(_nsa_attn_p_body, tq=tq, tk=tk, nbc=nbc, hp=hp),
        grid=(bsz, nq),
        in_specs=[
            pl.BlockSpec((tq, GW), lambda b, qi: (b * nq + qi, C_NQ // GW)),
            pl.BlockSpec((None, 2 * hp, 2 * LANES), lambda b, qi: (b, 0, 0)),
            kv(2 * LANES), kv(3 * LANES), kv(4 * LANES), kv(5 * LANES),
            pl.BlockSpec((tq, LANES), lambda b, qi: (b * nq + qi, C_NG // LANES)),
        ],
        out_specs=pl.BlockSpec((tq, GW), lambda b, qi: (b * nq + qi, 0)),
        out_shape=jax.ShapeDtypeStruct((bsz * t_len, GW), F32),
        compiler_params=_cparams(("parallel", "arbitrary")),
        name="nsa_attn_p",
    )(p, cmp, p, p, p, p, p)


def nsa_cmp_params(pe, cw):
    pe4 = jnp.concatenate([pe[0], pe[0], pe[1], pe[1]], axis=1)
    wk = cw[0].reshape(CMP_BLOCK, HEAD_DIM, HEAD_DIM)
    wv = cw[1].reshape(CMP_BLOCK, HEAD_DIM, HEAD_DIM)
    z = jnp.zeros_like(wk)
    rows = [jnp.concatenate([m if i == j else z for j in range(4)], axis=2) for i, m in enumerate((wk, wk, wv, wv))]
    return pe4, jnp.concatenate(rows, axis=1).astype(BF16)


def nsa_cmp_params_t(pe, cw):
    per_page = PAGE // CMP_BLOCK
    s_blk = lax.broadcasted_iota(jnp.int32, (PAGE, per_page * HEAD_DIM), 0) // CMP_BLOCK
    c_blk = lax.broadcasted_iota(jnp.int32, (PAGE, per_page * HEAD_DIM), 1) // HEAD_DIM

    def mat(w):
        wd = jnp.transpose(w.reshape(CMP_BLOCK, HEAD_DIM, HEAD_DIM), (1, 0, 2))
        full = jnp.where(s_blk == c_blk, jnp.tile(wd, (1, per_page, per_page)), 0.0)
        return full.astype(BF16).reshape(HEAD_DIM // 2, 2 * PAGE, per_page * HEAD_DIM)

    bias = lambda x: jnp.tile(x.T, (1, per_page))
    return bias(pe[0]), bias(pe[1]), mat(cw[0]), mat(cw[1])


T_PAD = SUBLANES
FOX_LF_PAGES = 16
FOX_KV_PAGES = 16
NSA_CMP_PAGES = 16


def _page_spec(block, layer, pages_per_step, i, tail, first_step=0):
    def index_map(b, s, pt):
        j = jnp.maximum(s - first_step, 0) * pages_per_step + i
        return (pt[b, j], layer) + tail
    return pl.BlockSpec(block, index_map)


def _fox_prep_s_body(pt_ref, *refs, n_pages):
    pg = FOX_LF_PAGES
    page_refs, o_ref, a_ref = refs[:pg], refs[pg], refs[pg + 1]
    s = pl.program_id(1)
    for i in range(pg):
        a_ref[pl.ds(pl.multiple_of((s * pg + i) * FOX_HEADS, FOX_HEADS), FOX_HEADS), :] = page_refs[i][...]

    @pl.when(s == pl.num_programs(1) - 1)
    def _():
        n = n_pages * FOX_HEADS
        a = a_ref[...]
        ji = lax.broadcasted_iota(jnp.int32, (PAGE, PAGE), 0)
        si = lax.broadcasted_iota(jnp.int32, (PAGE, PAGE), 1)
        within = jnp.dot(a, (ji > si).astype(F32), preferred_element_type=F32, precision=HIGHEST)
        tot = jnp.broadcast_to(jnp.sum(a, axis=1, keepdims=True), (n, LANES))
        ri = lax.broadcasted_iota(jnp.int32, (n, n), 0)
        ci = lax.broadcasted_iota(jnp.int32, (n, n), 1)
        later = ((ci > ri) & ((ci - ri) % FOX_HEADS == 0)).astype(F32)
        o_ref[...] = within + jnp.dot(later, tot, preferred_element_type=F32, precision=HIGHEST)


def fox_prep_s_call(page_table, cache_lf, *, layer, bsz, n_pages):
    pg = FOX_LF_PAGES
    assert n_pages % pg == 0
    n = n_pages * FOX_HEADS
    grid_spec = pltpu.PrefetchScalarGridSpec(
        num_scalar_prefetch=1,
        grid=(bsz, n_pages // pg),
        in_specs=[_page_spec((None, None, FOX_HEADS, PAGE), layer, pg, i, (0, 0)) for i in range(pg)],
        out_specs=pl.BlockSpec((None, n, LANES), lambda b, s, pt: (b, 0, 0)),
        scratch_shapes=[pltpu.VMEM((n, LANES), F32)],
    )
    return pl.pallas_call(
        functools.partial(_fox_prep_s_body, n_pages=n_pages),
        grid_spec=grid_spec,
        out_shape=jax.ShapeDtypeStruct((bsz, n, LANES), F32),
        compiler_params=_cparams(("parallel", "arbitrary")),
        name="fox_prep_s",
    )(page_table, *([cache_lf] * pg))


def _fox_attn_s_body(pt_ref, q_ref, k_ref, v_ref, f_ref, b_ref, rp_ref, *refs, n_real):
    pg = FOX_KV_PAGES
    page_refs = refs[:pg]
    o_ref, lf_ref, qbd_ref, rqp_ref, m_ref, l_ref, acc_ref = refs[pg:]
    s = pl.program_id(1)
    scale = HEAD_DIM ** -0.5
    nh = FOX_HEADS
    rows = nh * T_PAD
    row_h = lax.broadcasted_iota(jnp.int32, (rows, 1), 0) // T_PAD
    row_t = lax.broadcasted_iota(jnp.int32, (rows, 1), 0) % T_PAD

    def rep_heads(x8):
        return jnp.concatenate([jnp.broadcast_to(x8[h:h + 1, :], (T_PAD, x8.shape[1])) for h in range(nh)], axis=0)

    def attend(scores, values, m, l, acc):
        tile_max = scores[0]
        for sc in scores[1:]:
            tile_max = jnp.maximum(tile_max, sc)
        m_new = jnp.maximum(m, jnp.max(tile_max, axis=1, keepdims=True))
        alpha = jnp.exp(m - m_new)
        acc = alpha * acc
        psum = None
        for sc, pv in zip(scores, values):
            pr = jnp.exp(sc - m_new)
            psum = pr if psum is None else psum + pr
            acc = acc + pv(pr.astype(BF16))
        return m_new, alpha * l + jnp.sum(psum, axis=1, keepdims=True), acc

    @pl.when(s == 0)
    def _():
        lf = _log_sigmoid(f_ref[...] + b_ref[...])
        lf_ref[...] = lf
        tok = lax.broadcasted_iota(jnp.int32, (T_PAD, LANES), 0)
        lfm = jnp.where(tok < n_real, lf, 0.0)
        r_new = jnp.zeros((T_PAD, LANES), F32)
        for j in range(1, n_real):
            r_new = r_new + jnp.where(tok < j, lfm[j:j + 1, :], 0.0)
        tot = jnp.sum(lfm, axis=0, keepdims=True)
        lane = lax.broadcasted_iota(jnp.int32, (rows, LANES), 1)
        pick = lane == row_h
        r_q = jnp.sum(jnp.where(pick, jnp.concatenate([r_new] * nh, axis=0), 0.0), axis=1, keepdims=True)
        t_q = jnp.sum(jnp.where(pick, jnp.broadcast_to(tot, (rows, LANES)), 0.0), axis=1, keepdims=True)
        rqp_ref[...] = r_q - t_q
        col = lax.broadcasted_iota(jnp.int32, (rows, GW), 1)
        qbd = jnp.where(col // HEAD_DIM == row_h, jnp.concatenate([q_ref[...]] * nh, axis=0), 0.0)
        qbd_ref[...] = qbd.astype(BF16)
        zrow = jnp.zeros((PAGE - T_PAD, GW), F32)
        kk = jnp.concatenate([k_ref[...], zrow], axis=0)
        vv = jnp.concatenate([v_ref[...], zrow], axis=0)
        r_pad = jnp.concatenate([r_new, jnp.zeros((PAGE - T_PAD, LANES), F32)], axis=0)
        bias = rep_heads(r_pad.T[:nh, :]) - r_q
        key = lax.broadcasted_iota(jnp.int32, (rows, PAGE), 1)
        mask = (key <= row_t) & (key < n_real)
        sc = jnp.where(mask, _dot_nt(qbd.astype(BF16), kk.astype(BF16)) * scale + bias, -jnp.inf)
        m0 = jnp.full((rows, 1), -jnp.inf, F32)
        m, l, acc = attend([sc], [lambda pr: _dot(pr, vv.astype(BF16))], m0,
                           jnp.zeros((rows, 1), F32), jnp.zeros((rows, GW), F32))
        m_ref[...] = m
        l_ref[...] = l
        acc_ref[...] = acc

    @pl.when(s > 0)
    def _():
        rqp = rqp_ref[...]
        qbd = qbd_ref[...]
        scores = [_dot(qbd, page_refs[i][0:GW, :].astype(BF16)) * scale
                  + (rep_heads(rp_ref[i * nh:(i + 1) * nh, :]) - rqp) for i in range(pg)]
        values = [lambda pr, i=i: _dot_nt(pr, page_refs[i][GW:2 * GW, :].astype(BF16)) for i in range(pg)]
        m, l, acc = attend(scores, values, m_ref[...], l_ref[...], acc_ref[...])
        m_ref[...] = m
        l_ref[...] = l
        acc_ref[...] = acc

    @pl.when(s == pl.num_programs(1) - 1)
    def _():
        col = lax.broadcasted_iota(jnp.int32, (rows, GW), 1)
        o = jnp.where(col // HEAD_DIM == row_h, acc_ref[...] / l_ref[...], 0.0)
        o_ref[...] = jnp.sum(o.reshape(nh, T_PAD, GW), axis=0)


def fox_attn_s_call(page_table, p, bias, r_past, cache_kv, *, layer, bsz, n_pages, n_real):
    pg = FOX_KV_PAGES
    assert n_pages % pg == 0
    rows = FOX_HEADS * T_PAD
    new = lambda width, off: pl.BlockSpec((T_PAD, width), lambda b, s, pt: (b, off // width))
    grid_spec = pltpu.PrefetchScalarGridSpec(
        num_scalar_prefetch=1,
        grid=(bsz, 1 + n_pages // pg),
        in_specs=[new(GW, C_FQ), new(GW, C_FK), new(GW, C_FV), new(LANES, C_FF),
                  pl.BlockSpec((1, LANES), lambda b, s, pt: (0, 0)),
                  pl.BlockSpec((None, pg * FOX_HEADS, LANES), lambda b, s, pt: (b, jnp.maximum(s - 1, 0), 0))]
                 + [_page_spec((None, None, 2 * GW, PAGE), layer, pg, i, (0, 0), first_step=1) for i in range(pg)],
        out_specs=[pl.BlockSpec((T_PAD, GW), lambda b, s, pt: (b, 0)),
                   pl.BlockSpec((T_PAD, LANES), lambda b, s, pt: (b, 0))],
        scratch_shapes=[pltpu.VMEM((rows, GW), BF16), pltpu.VMEM((rows, 1), F32), pltpu.VMEM((rows, 1), F32),
                        pltpu.VMEM((rows, 1), F32), pltpu.VMEM((rows, GW), F32)],
    )
    return pl.pallas_call(
        functools.partial(_fox_attn_s_body, n_real=n_real),
        grid_spec=grid_spec,
        out_shape=[jax.ShapeDtypeStruct((bsz * T_PAD, GW), F32), jax.ShapeDtypeStruct((bsz * T_PAD, LANES), F32)],
        compiler_params=_cparams(("parallel", "arbitrary")),
        name="fox_attn_s",
    )(page_table, p, p, p, p, bias, r_past, *([cache_kv] * pg))


def _nsa_cmp_s_body(pt_ref, *refs, n_pages):
    pg = NSA_CMP_PAGES
    k_pages, v_pages = refs[:pg], refs[pg:2 * pg]
    pek_ref, pev_ref, mk_ref, mv_ref, o_ref, xk_ref, xv_ref = refs[2 * pg:]
    s = pl.program_id(1)
    for i in range(pg):
        rows = pl.ds(pl.multiple_of((s * pg + i) * PAGE, PAGE), PAGE)
        xk_ref[rows, :] = k_pages[i][...]
        xv_ref[rows, :] = v_pages[i][...]

    @pl.when(s == pl.num_programs(1) - 1)
    def _():
        per_page = PAGE // CMP_BLOCK
        width = per_page * HEAD_DIM
        for x_ref, pe_ref, m_ref, off in ((xk_ref, pek_ref, mk_ref, 0), (xv_ref, pev_ref, mv_ref, 2 * HEAD_DIM)):
            acc = jnp.zeros((NSA_KV_HEADS * n_pages, width), F32)
            for d in range(0, HEAD_DIM, 2):
                rows = jnp.concatenate(
                    [jnp.concatenate([x_ref[pl.ds(g * HEAD_DIM + dd, n_pages, stride=PAGE), :] + pe_ref[dd:dd + 1, :]
                                      for g in range(NSA_KV_HEADS)], axis=0) for dd in (d, d + 1)], axis=1)
                acc = acc + _dot(rows.astype(BF16), m_ref[d // 2])
            for g in range(NSA_KV_HEADS):
                for n in range(per_page):
                    o_ref[n * n_pages:(n + 1) * n_pages, off + g * HEAD_DIM:off + (g + 1) * HEAD_DIM] = (
                        acc[g * n_pages:(g + 1) * n_pages, n * HEAD_DIM:(n + 1) * HEAD_DIM])


def nsa_cmp_s_call(page_table, cache_kv, pek, pev, mk, mv, *, layer, bsz, n_pages):
    pg = NSA_CMP_PAGES
    assert n_pages % pg == 0
    per_page = PAGE // CMP_BLOCK
    nbc = n_pages * per_page
    const = lambda shape: pl.BlockSpec(shape, lambda b, s, pt: (0,) * len(shape))
    grid_spec = pltpu.PrefetchScalarGridSpec(
        num_scalar_prefetch=1,
        grid=(bsz, n_pages // pg),
        in_specs=[_page_spec((None, None, PAGE, PAGE), layer, pg, i, (0, 0)) for i in range(pg)]
                 + [_page_spec((None, None, PAGE, PAGE), layer, pg, i, (1, 0)) for i in range(pg)]
                 + [const((HEAD_DIM, PAGE)), const((HEAD_DIM, PAGE)),
                    const((HEAD_DIM // 2, 2 * PAGE, per_page * HEAD_DIM)),
                    const((HEAD_DIM // 2, 2 * PAGE, per_page * HEAD_DIM))],
        out_specs=pl.BlockSpec((None, nbc, 2 * LANES), lambda b, s, pt: (b, 0, 0)),
        scratch_shapes=[pltpu.VMEM((n_pages * PAGE, PAGE), F32), pltpu.VMEM((n_pages * PAGE, PAGE), F32)],
    )
    return pl.pallas_call(
        functools.partial(_nsa_cmp_s_body, n_pages=n_pages),
        grid_spec=grid_spec,
        out_shape=jax.ShapeDtypeStruct((bsz, nbc, 2 * LANES), F32),
        compiler_params=_cparams(("parallel", "arbitrary")),
        name="nsa_cmp_s",
    )(page_table, *([cache_kv] * (2 * pg)), pek, pev, mk, mv)


def _stack_heads(q_ref, g):
    hg_n = NSA_GROUP
    return jnp.concatenate(
        [q_ref[:, (g * hg_n + hg) * HEAD_DIM:(g * hg_n + hg + 1) * HEAD_DIM] for hg in range(hg_n)], axis=0)


def _nsa_topk_s_body(q_ref, cmp_ref, o_ref, idx_ref, *, q_off, nbc):
    hg_n = NSA_GROUP
    per_page = PAGE // CMP_BLOCK
    n_pages = nbc // per_page
    w = 2 * n_pages + LANES
    t_col = q_off + lax.broadcasted_iota(jnp.int32, (T_PAD, 1), 0)
    t_stack = jnp.concatenate([t_col] * hg_n, axis=0)
    c = lax.broadcasted_iota(jnp.int32, (1, nbc), 1)
    cmp_blk = per_page * (c % n_pages) + c // n_pages

    def slc_blk(i):
        return jnp.where(i < n_pages, 2 * i, jnp.where(i < 2 * n_pages, 2 * (i - n_pages) + 1, i))

    blk_i = slc_blk(lax.broadcasted_iota(jnp.int32, (w, w), 0))
    blk_j = slc_blk(lax.broadcasted_iota(jnp.int32, (w, w), 1))
    blk_row = slc_blk(lax.broadcasted_iota(jnp.int32, (1, w), 1))
    rr = lax.broadcasted_iota(jnp.int32, (TOP_N, w), 0)
    lane = lax.broadcasted_iota(jnp.int32, (TOP_N, LANES), 1)
    for g in range(NSA_KV_HEADS):
        qs = _stack_heads(q_ref, g).astype(BF16)
        o_cmp, pc = _cmp_attend(qs, cmp_ref, g, t_stack, cmp_blk, cmp_blk >= 0)
        for hg in range(hg_n):
            h = g * hg_n + hg
            o_ref[:, h * HEAD_DIM:(h + 1) * HEAD_DIM] = o_cmp[hg * T_PAD:(hg + 1) * T_PAD]
        imp = pc[0:T_PAD]
        for hg in range(1, hg_n):
            imp = imp + pc[hg * T_PAD:(hg + 1) * T_PAD]
        imp = jnp.concatenate([imp[:, 0:n_pages] + imp[:, n_pages:2 * n_pages],
                               imp[:, 2 * n_pages:3 * n_pages] + imp[:, 3 * n_pages:4 * n_pages],
                               jnp.zeros((T_PAD, LANES), F32)], axis=1)
        score = _select_score(imp, t_col, blk_row)
        score_t = jnp.concatenate([score, jnp.zeros((LANES - T_PAD, w), F32)], axis=0).T
        out = jnp.zeros((TOP_N, LANES), F32)
        for t in range(T_PAD):
            col = score_t[:, t:t + 1]
            row = score[t:t + 1, :]
            ahead = (col > row) | ((col == row) & (blk_i < blk_j))
            rank = jnp.sum(ahead.astype(F32), axis=0, keepdims=True)
            hit = (rank == rr.astype(F32)) & (row >= 0.0)
            found = jnp.sum(hit.astype(F32), axis=1, keepdims=True)
            which = jnp.sum(jnp.where(hit, blk_row.astype(F32), 0.0), axis=1, keepdims=True)
            out = jnp.where(lane == t, jnp.where(found > 0.5, which, -1.0), out)
        idx_ref[g] = out.astype(jnp.int32)


def nsa_topk_s_call(p, cmp, *, bsz, q_off):
    nbc = cmp.shape[1]
    return pl.pallas_call(
        functools.partial(_nsa_topk_s_body, q_off=q_off, nbc=nbc),
        grid=(bsz,),
        in_specs=[pl.BlockSpec((T_PAD, GW), lambda b: (b, C_NQ // GW)),
                  pl.BlockSpec((None, nbc, 2 * LANES), lambda b: (b, 0, 0))],
        out_specs=[pl.BlockSpec((T_PAD, GW), lambda b: (b, 0)),
                   pl.BlockSpec((None, NSA_KV_HEADS, TOP_N, LANES), lambda b: (b, 0, 0, 0))],
        out_shape=[jax.ShapeDtypeStruct((bsz * T_PAD, GW), F32),
                   jax.ShapeDtypeStruct((bsz, NSA_KV_HEADS, TOP_N, LANES), jnp.int32)],
        compiler_params=_cparams(("parallel",)),
        name="nsa_topk_s",
    )(p, cmp)


def _nsa_sel_s_body(pt_ref, idx_ref, q_ref, kn_ref, vn_ref, *refs, n_real, n_tok, q_off):
    k_blocks, v_blocks, o_ref = refs[:TOP_N], refs[TOP_N:2 * TOP_N], refs[2 * TOP_N]
    b, g, t = pl.program_id(0), pl.program_id(1), pl.program_id(2)
    scale = HEAD_DIM ** -0.5
    hg_n = NSA_GROUP
    base = ((b * NSA_KV_HEADS + g) * n_tok + t) * TOP_N
    new_blk = q_off // SLC_BLOCK
    qrow = q_ref[pl.ds(t, 1), :]
    qs = jnp.concatenate([qrow[:, hg * HEAD_DIM:(hg + 1) * HEAD_DIM] for hg in range(hg_n)]
                         + [jnp.zeros((SUBLANES - hg_n, HEAD_DIM), F32)], axis=0).astype(BF16)

    def pick(blk):
        return jnp.where(g == 0, blk[:, 0:HEAD_DIM], blk[:, HEAD_DIM:2 * HEAD_DIM])

    lane_half = lax.broadcasted_iota(jnp.int32, (SUBLANES, PAGE), 1) // SLC_BLOCK
    logits, values = [], []
    has_new = jnp.bool_(False)
    for r in range(TOP_N):
        j = idx_ref[base + r]
        from_cache = (j >= 0) & (j < new_blk)
        has_new = has_new | (j == new_blk)
        sc = _dot(qs, k_blocks[r][...].astype(BF16)) * scale
        logits.append(jnp.where(from_cache & (lane_half == j % 2), sc, -jnp.inf))
        values.append(lambda pr, r=r: _dot_nt(pr, v_blocks[r][...].astype(BF16)))
    sc = _dot_nt(qs, pick(kn_ref[...]).astype(BF16)) * scale
    key = lax.broadcasted_iota(jnp.int32, (SUBLANES, T_PAD), 1)
    logits.append(jnp.where(has_new & (key <= t) & (key < n_real), sc, -jnp.inf))
    values.append(lambda pr: _dot(pr, pick(vn_ref[...]).astype(BF16)))
    m = logits[0].max(axis=1, keepdims=True)
    for x in logits[1:]:
        m = jnp.maximum(m, x.max(axis=1, keepdims=True))
    l = jnp.zeros((SUBLANES, 1), F32)
    acc = jnp.zeros((SUBLANES, HEAD_DIM), F32)
    for x, pv in zip(logits, values):
        pr = jnp.exp(x - m)
        l = l + jnp.sum(pr, axis=1, keepdims=True)
        acc = acc + pv(pr.astype(BF16))
    o_ref[...] = acc / l


def nsa_sel_s_call(page_table, idx, p, cache_kv, *, layer, bsz, n_tok, n_real, q_off):
    n_cached = q_off // SLC_BLOCK
    blocks_per_page = PAGE // SLC_BLOCK

    def blk_spec(r, kind):
        def index_map(b, g, t, pt, ix):
            j = jnp.clip(ix[((b * NSA_KV_HEADS + g) * n_tok + t) * TOP_N + r], 0, n_cached - 1)
            return (pt[b, j // blocks_per_page], layer, kind * NSA_KV_HEADS + g, 0)
        return pl.BlockSpec((None, None, HEAD_DIM, PAGE), index_map)

    grid_spec = pltpu.PrefetchScalarGridSpec(
        num_scalar_prefetch=2,
        grid=(bsz, NSA_KV_HEADS, n_tok),
        in_specs=[pl.BlockSpec((T_PAD, 2 * LANES), lambda b, g, t, pt, ix: (b, C_NQ // (2 * LANES) + g)),
                  pl.BlockSpec((T_PAD, LANES), lambda b, g, t, pt, ix: (b, C_NKV // LANES + 2)),
                  pl.BlockSpec((T_PAD, LANES), lambda b, g, t, pt, ix: (b, C_NKV // LANES + 3))]
                 + [blk_spec(r, 2) for r in range(TOP_N)] + [blk_spec(r, 3) for r in range(TOP_N)],
        out_specs=pl.BlockSpec((None, None, None, SUBLANES, HEAD_DIM), lambda b, g, t, pt, ix: (b, g, t, 0, 0)),
    )
    return pl.pallas_call(
        functools.partial(_nsa_sel_s_body, n_real=n_real, n_tok=n_tok, q_off=q_off),
        grid_spec=grid_spec,
        out_shape=jax.ShapeDtypeStruct((bsz, NSA_KV_HEADS, n_tok, SUBLANES, HEAD_DIM), F32),
        compiler_params=_cparams(("parallel", "arbitrary", "arbitrary")),
        name="nsa_sel_s",
    )(page_table, idx, p, p, p, *([cache_kv] * (2 * TOP_N)))


def _nsa_win_s_body(q_ref, gl_ref, oc_ref, os_ref, win_ref, new_ref, o_ref, *, n_real, win_len):
    hg_n = NSA_GROUP
    rows = hg_n * T_PAD
    scale = HEAD_DIM ** -0.5
    gate = _sigmoid(gl_ref[...])
    t_row = lax.broadcasted_iota(jnp.int32, (rows, 1), 0) % T_PAD
    key = lax.broadcasted_iota(jnp.int32, (rows, win_len), 1)
    mask_c = key + WINDOW > t_row + win_len
    new_i = lax.broadcasted_iota(jnp.int32, (rows, PAGE), 1)
    mask_n = (new_i <= t_row) & (new_i < n_real)
    zpad = jnp.zeros((PAGE - T_PAD, HEAD_DIM), F32)
    for g in range(NSA_KV_HEADS):
        qs = _stack_heads(q_ref, g).astype(BF16)
        kt = win_ref[g * HEAD_DIM:(g + 1) * HEAD_DIM, :].astype(BF16)
        vt = win_ref[LANES + g * HEAD_DIM:LANES + (g + 1) * HEAD_DIM, :].astype(BF16)
        kn = jnp.concatenate([new_ref[:, g * HEAD_DIM:(g + 1) * HEAD_DIM], zpad], axis=0).astype(BF16)
        vn = jnp.concatenate([new_ref[:, LANES + g * HEAD_DIM:LANES + (g + 1) * HEAD_DIM], zpad], axis=0).astype(BF16)
        sc_c = jnp.where(mask_c, _dot(qs, kt) * scale, -jnp.inf)
        sc_n = jnp.where(mask_n, _dot_nt(qs, kn) * scale, -jnp.inf)
        m = jnp.maximum(jnp.max(sc_c, axis=1, keepdims=True), jnp.max(sc_n, axis=1, keepdims=True))
        pr_c = jnp.exp(sc_c - m)
        pr_n = jnp.exp(sc_n - m)
        l = jnp.sum(pr_c, axis=1, keepdims=True) + jnp.sum(pr_n, axis=1, keepdims=True)
        o_win = (_dot_nt(pr_c.astype(BF16), vt) + _dot(pr_n.astype(BF16), vn)) / l
        for hg in range(hg_n):
            h = g * hg_n + hg
            cs = slice(h * HEAD_DIM, (h + 1) * HEAD_DIM)
            o_ref[:, cs] = (gate[:, 3 * h:3 * h + 1] * oc_ref[:, cs] + gate[:, 3 * h + 1:3 * h + 2] * os_ref[:, cs]
                            + gate[:, 3 * h + 2:3 * h + 3] * o_win[hg * T_PAD:(hg + 1) * T_PAD])


def nsa_win_s_call(p, o_cmp, o_slc, cache_win, *, layer, bsz, n_real):
    win_len = cache_win.shape[3]
    assert win_len == WINDOW
    row = lambda width, off: pl.BlockSpec((T_PAD, width), lambda b: (b, off // width))
    return pl.pallas_call(
        functools.partial(_nsa_win_s_body, n_real=n_real, win_len=win_len),
        grid=(bsz,),
        in_specs=[row(GW, C_NQ), row(LANES, C_NG), row(GW, 0), row(GW, 0),
                  pl.BlockSpec((None, None, 2 * LANES, win_len), lambda b: (b, layer, 0, 0)),
                  row(2 * LANES, C_NKV + 4 * LANES)],
        out_specs=pl.BlockSpec((T_PAD, GW), lambda b: (b, 0)),
        out_shape=jax.ShapeDtypeStruct((bsz * T_PAD, GW), F32),
        compiler_params=_cparams(("parallel",)),
        name="nsa_win_s",
    )(p, p, o_cmp, o_slc, cache_win, p)


def _lane_pad(v):
    return jnp.pad(v, (0, LANES - v.shape[0]))[None, :]


def _layer_weights(w, l):
    win_t = w['w_in'][l].T
    o_fox = GW + SSD_CONV_DIM + SSD_HEADS
    o_nsa = o_fox + 3 * GW + FOX_HEADS
    o_s5 = o_nsa + GW + 6 * NSA_KV_HEADS * HEAD_DIM + 3 * NSA_HEADS
    cols = lambda a, b: win_t[a:b]
    zpad = lambda n: jnp.zeros((n, D_MODEL), F32)
    w_in = jnp.concatenate([
        cols(GW, GW + SSD_CONV_DIM),
        cols(GW + SSD_CONV_DIM, o_fox), zpad(LANES - SSD_HEADS),
        cols(o_fox + 3 * GW, o_nsa), zpad(LANES - FOX_HEADS),
        cols(0, GW),
        cols(o_s5, o_s5 + GW),
        cols(o_fox, o_fox + 3 * GW),
        cols(o_nsa, o_nsa + GW),
        cols(o_nsa + GW, o_nsa + GW + 6 * LANES),
        cols(o_nsa + GW + 6 * LANES, o_s5), zpad(LANES - 3 * NSA_HEADS),
        zpad(P_W - C_NG - LANES),
    ], axis=0).astype(BF16)
    assert w_in.shape == (P_W, D_MODEL)
    pe4, w4 = nsa_cmp_params(w['nsa_cmp_pe'][l], w['nsa_cmp_w'][l])
    s5 = s5_params(w['s5_lambda_re'][l], w['s5_lambda_im'][l], w['s5_log_dt'][l],
                   w['s5_b_re'][l], w['s5_b_im'][l], w['s5_c_re'][l], w['s5_c_im'][l])
    return dict(
        ffn1=(w['ffn1_norm'][l][None, :], w['ffn1_w1'], w['ffn1_w3'], w['ffn1_w2']),
        ffn2=(w['ffn2_norm'][l][None, :], w['ffn2_w1'], w['ffn2_w3'], w['ffn2_w2']),
        mix_norm=w['mix_norm'][l][None, :], w_in=w_in,
        ssd=(w['ssd_conv_w'][l], w['ssd_conv_b'][l][None, :], _lane_pad(w['ssd_dt_bias'][l]),
             _lane_pad(w['ssd_a_log'][l]), _lane_pad(w['ssd_d'][l])),
        fox_bias=_lane_pad(w['fox_f_bias'][l]),
        pe4=pe4, w4=w4, cmp_t=nsa_cmp_params_t(w['nsa_cmp_pe'][l], w['nsa_cmp_w'][l]), s5=s5, s5_d=w['s5_d'][l][None, :], s5_glu=w['s5_w_glu'][l].astype(BF16),
        gains=jnp.stack([w['ssd_norm'][l], w['fox_out_norm'][l], w['nsa_out_norm'][l], w['s5_out_norm'][l]]),
        w_out=w['w_out'],
    )


def _prompt_layer(x, lw, final_gain, *, layer, bsz, t_len, last):
    tm = 512
    x = ffn_call(x, *lw['ffn1'], final_gain, tm=FFN_ROWS, layer=layer, final_norm=False)
    p, fox_kv, nsa_kv, nsa_win = inproj_call(x, lw['mix_norm'], lw['w_in'], tm=tm, keys_minor=(bsz, t_len))
    conv0 = jnp.zeros((bsz, CONV_W - 1, SSD_CONV_DIM), F32)
    ssm0 = jnp.zeros((bsz, SSD_HEADS, HEAD_DIM, SSD_STATE), F32)
    y_ssd, ssm, conv = ssd_call(p, conv0, ssm0, *lw['ssd'], bsz=bsz, t_len=t_len, n_real=SSD_CHUNK)
    lf, cumt = fox_prep_call(p, lw['fox_bias'], bsz=bsz, t_len=t_len)
    y_fox = fox_attn_call(p, cumt, bsz=bsz, t_len=t_len)
    cmp = nsa_cmp_p_call(p, lw['pe4'], lw['w4'], bsz=bsz, t_len=t_len)
    y_nsa = nsa_attn_p_call(p, cmp, bsz=bsz, t_len=t_len)
    bre, bim, are, aim, cre, cim = lw['s5']
    x0 = jnp.zeros((SUBLANES, S5_N), F32)
    y_s5, xr, xi = s5_call(p.reshape(bsz, t_len, P_W), bre, bim, are, aim, x0, x0, cre, cim, lw['s5_d'], lw['s5_glu'],
                           t_len=t_len, steps=256, t_last=t_len - 1)
    y_s5 = y_s5.reshape(bsz * t_len, GW)
    x = outproj_call(x, y_ssd, y_fox, y_nsa, y_s5, lw['gains'], lw['w_out'], tm=tm, layer=layer)
    x = ffn_call(x, *lw['ffn2'], final_gain, tm=FFN_ROWS, layer=layer, final_norm=last)
    keep = min(WINDOW, t_len)
    off = ((t_len - 1) % (SUBLANES // bsz)) * bsz
    time_second = lambda a, *dims: jnp.moveaxis(a.reshape(bsz, *dims, a.shape[-1]), -1, 1)
    states = (
        time_second(fox_kv, 2, FOX_HEADS, HEAD_DIM),
        time_second(lf, FOX_HEADS),
        time_second(nsa_kv, 4, NSA_KV_HEADS, HEAD_DIM),
        time_second(nsa_win[:, :, t_len - keep:], 2, NSA_KV_HEADS, HEAD_DIM),
        ssm, conv,
        xr[off:off + bsz].reshape(bsz, S5_GROUPS, S5_STATE),
        xi[off:off + bsz].reshape(bsz, S5_GROUPS, S5_STATE),
    )
    return x, states


def _sample_layer(x, lw, final_gain, caches, page_table, *, layer, bsz, n_real, q_off, last):
    fox_kv_t, fox_lf_t, nsa_kv_t, nsa_win_t, cache_nsa_win, st_ssd, st_conv, st_re, st_im = caches
    tm = bsz * T_PAD
    n_pages = page_table.shape[1]
    assert q_off == n_pages * PAGE and q_off % SLC_BLOCK == 0 and bsz == SUBLANES
    x = ffn_call(x, *lw['ffn1'], final_gain, tm=tm, layer=layer, final_norm=False)
    p, fox_kv, nsa_kv, nsa_win = inproj_call(x, lw['mix_norm'], lw['w_in'], tm=tm)
    p3 = p.reshape(bsz, T_PAD, P_W)

    p_ssd = jnp.pad(p3[:, :, :C_U], ((0, 0), (0, SSD_CHUNK - T_PAD), (0, 0))).reshape(bsz * SSD_CHUNK, C_U)
    y_ssd, ssm, conv = ssd_call(p_ssd, st_conv[:, layer], st_ssd[:, layer], *lw['ssd'],
                                bsz=bsz, t_len=SSD_CHUNK, n_real=n_real)
    y_ssd = y_ssd.reshape(bsz, SSD_CHUNK, GW)[:, :T_PAD].reshape(tm, GW)

    r_past = fox_prep_s_call(page_table, fox_lf_t, layer=layer, bsz=bsz, n_pages=n_pages)
    y_fox, lf = fox_attn_s_call(page_table, p, lw['fox_bias'], r_past, fox_kv_t,
                                layer=layer, bsz=bsz, n_pages=n_pages, n_real=n_real)

    cmp = nsa_cmp_s_call(page_table, nsa_kv_t, *lw['cmp_t'], layer=layer, bsz=bsz, n_pages=n_pages)
    o_cmp, idx = nsa_topk_s_call(p, cmp, bsz=bsz, q_off=q_off)
    idx = jnp.swapaxes(idx[:, :, :, :n_real], 2, 3).reshape(-1)
    o_slc = nsa_sel_s_call(page_table, idx, p, nsa_kv_t,
                           layer=layer, bsz=bsz, n_tok=n_real, n_real=n_real, q_off=q_off)
    o_slc = jnp.transpose(o_slc[:, :, :, :NSA_GROUP], (0, 2, 1, 3, 4)).reshape(bsz, n_real, GW)
    o_slc = jnp.pad(o_slc, ((0, 0), (0, T_PAD - n_real), (0, 0))).reshape(tm, GW)
    y_nsa = nsa_win_s_call(p, o_cmp, o_slc, nsa_win_t, layer=layer, bsz=bsz, n_real=n_real)

    bre, bim, are, aim, cre, cim = lw['s5']
    y_s5, xr, xi = s5_call(p3, bre, bim, are, aim, st_re[:, layer].reshape(bsz, S5_N), st_im[:, layer].reshape(bsz, S5_N),
                           cre, cim, lw['s5_d'], lw['s5_glu'], t_len=T_PAD, steps=T_PAD, t_last=n_real - 1)
    y_s5 = y_s5.reshape(tm, GW)

    x = outproj_call(x, y_ssd, y_fox, y_nsa, y_s5, lw['gains'], lw['w_out'], tm=tm, layer=layer)
    x = ffn_call(x, *lw['ffn2'], final_gain, tm=tm, layer=layer, final_norm=last)
    real = lambda a: a.reshape(bsz, T_PAD, -1)[:, :n_real]
    win_rows = real(nsa_win).reshape(bsz, n_real, 2, NSA_KV_HEADS, HEAD_DIM)
    states = (
        real(fox_kv).reshape(bsz, n_real, 2, FOX_HEADS, HEAD_DIM),
        lf.reshape(bsz, T_PAD, LANES)[:, :n_real, :FOX_HEADS],
        real(nsa_kv).reshape(bsz, n_real, 4, NSA_KV_HEADS, HEAD_DIM),
        jnp.concatenate([cache_nsa_win[:, layer, n_real:], win_rows], axis=1),
        ssm, conv,
        xr.reshape(bsz, S5_GROUPS, S5_STATE), xi.reshape(bsz, S5_GROUPS, S5_STATE),
    )
    return x, states


def kernel(x_prompt, x_sample, cache_fox_kv, cache_fox_logf, cache_nsa_kv, cache_nsa_win_kv, state_ssd,
           state_ssd_conv, state_s5_re, state_s5_im, page_table, ffn1_norm, ffn1_w1, ffn1_w3, ffn1_w2, mix_norm,
           w_in, ssd_conv_w, ssd_conv_b, ssd_dt_bias, ssd_a_log, ssd_d, ssd_norm, fox_f_bias, fox_out_norm,
           nsa_cmp_pe, nsa_cmp_w, nsa_out_norm, s5_lambda_re, s5_lambda_im, s5_log_dt, s5_b_re, s5_b_im,
           s5_c_re, s5_c_im, s5_d, s5_w_glu, s5_out_norm, w_out, ffn2_norm, ffn2_w1, ffn2_w3, ffn2_w2, final_norm):
    w = dict(ffn1_norm=ffn1_norm, ffn1_w1=ffn1_w1, ffn1_w3=ffn1_w3, ffn1_w2=ffn1_w2, mix_norm=mix_norm, w_in=w_in,
             ssd_conv_w=ssd_conv_w, ssd_conv_b=ssd_conv_b, ssd_dt_bias=ssd_dt_bias, ssd_a_log=ssd_a_log, ssd_d=ssd_d,
             ssd_norm=ssd_norm, fox_f_bias=fox_f_bias, fox_out_norm=fox_out_norm, nsa_cmp_pe=nsa_cmp_pe,
             nsa_cmp_w=nsa_cmp_w, nsa_out_norm=nsa_out_norm, s5_lambda_re=s5_lambda_re, s5_lambda_im=s5_lambda_im,
             s5_log_dt=s5_log_dt, s5_b_re=s5_b_re, s5_b_im=s5_b_im, s5_c_re=s5_c_re, s5_c_im=s5_c_im, s5_d=s5_d,
             s5_w_glu=s5_w_glu, s5_out_norm=s5_out_norm, w_out=w_out, ffn2_norm=ffn2_norm, ffn2_w1=ffn2_w1,
             ffn2_w3=ffn2_w3, ffn2_w2=ffn2_w2)
    bsz_p, t_len, _ = x_prompt.shape
    bsz_s, n_real, _ = x_sample.shape
    depth = w_in.shape[0]
    q_off = page_table.shape[1] * PAGE
    fg = final_norm[None, :]
    n_pool = cache_fox_kv.shape[0]
    keys_minor = (0, 1, 3, 4, 5, 2)
    caches = (jnp.transpose(cache_fox_kv, keys_minor).reshape(n_pool, depth, 2 * GW, PAGE),
              jnp.transpose(cache_fox_logf, (0, 1, 3, 2)),
              jnp.transpose(cache_nsa_kv, keys_minor).reshape(n_pool, depth, 4 * LANES, PAGE),
              jnp.transpose(cache_nsa_win_kv, keys_minor).reshape(bsz_s, depth, 2 * LANES, WINDOW),
              cache_nsa_win_kv, state_ssd, state_ssd_conv, state_s5_re, state_s5_im)
    xp = x_prompt.reshape(bsz_p * t_len, D_MODEL)
    xs = jnp.pad(x_sample, ((0, 0), (0, T_PAD - n_real), (0, 0))).reshape(bsz_s * T_PAD, D_MODEL)
    st_p, st_s = [], []
    for l in range(depth):
        lw = _layer_weights(w, l)
        last = l == depth - 1
        xp, sp = _prompt_layer(xp, lw, fg, layer=l, bsz=bsz_p, t_len=t_len, last=last)
        xs, ss = _sample_layer(xs, lw, fg, caches, page_table, layer=l, bsz=bsz_s, n_real=n_real, q_off=q_off,
                               last=last)
        st_p.append(sp)
        st_s.append(ss)
    y_p = xp.reshape(bsz_p, t_len, D_MODEL)
    y_s = xs.reshape(bsz_s, T_PAD, D_MODEL)[:, :n_real]
    out = [y_p, y_s]
    for i in range(8):
        out.append(jnp.stack([s[i] for s in st_p], axis=1))
        out.append(jnp.stack([s[i] for s in st_s], axis=1))
    return tuple(out)
```

```python
import functools
import math

import jax
import jax.numpy as jnp
from jax import lax
from jax.experimental import pallas as pl
from jax.experimental.pallas import tpu as pltpu

F32 = jnp.float32
BF16 = jnp.bfloat16
HIGHEST = lax.Precision.HIGHEST

D_MODEL = 2048
DEPTH = 2
HEAD_DIM = 64
GW = D_MODEL // 4
D_FF = ((8 * D_MODEL // 3 + 127) // 128) * 128
EPS = 1e-6
TINY = 1e-30
SSD_HEADS = GW // HEAD_DIM
SSD_GROUPS = 2
SSD_STATE = 64
CONV_W = 4
SSD_CONV_DIM = GW + 2 * SSD_GROUPS * SSD_STATE
SSD_CHUNK = 128
FOX_HEADS = GW // HEAD_DIM
NSA_HEADS = GW // HEAD_DIM
NSA_KV_HEADS = 2
NSA_GROUP = NSA_HEADS // NSA_KV_HEADS
CMP_BLOCK = 32
SLC_BLOCK = 64
TOP_N = 16
WINDOW = 512
FORCE_SCORE = 1e4
S5_CH = 16
S5_GROUPS = GW // S5_CH
S5_STATE = 64
S5_N = S5_GROUPS * S5_STATE
PAGE = 128

LANES = 128
SUBLANES = 8
VMEM_LIMIT = 56 * 1024 * 1024

C_XBC = 0
C_DT = 768
C_FF = 896
C_Z = 1024
C_U = 1536
C_FQ = 2048
C_FK = 2560
C_FV = 3072
C_NQ = 3584
C_NKV = 4096
C_NG = 4864
P_W = 5120
FF_TILE = 256
FFN_ROWS = 1024


def _cparams(sem):
    return pltpu.CompilerParams(dimension_semantics=sem, vmem_limit_bytes=VMEM_LIMIT)


def _rms(x, g):
    ms = jnp.mean(x * x, axis=-1, keepdims=True)
    return x * lax.rsqrt(ms + EPS) * g


def _sigmoid(x):
    return 1.0 / (1.0 + jnp.exp(-x))


def _silu(x):
    return x * _sigmoid(x)


def _softplus(x):
    return jnp.maximum(x, 0.0) + jnp.log(1.0 + jnp.exp(-jnp.abs(x)))


def _log_sigmoid(x):
    return jnp.minimum(x, 0.0) - jnp.log(1.0 + jnp.exp(-jnp.abs(x)))


def _dot(a, b):
    return jnp.dot(a, b, preferred_element_type=F32)


def _dot_nt(a, b):
    return lax.dot_general(a, b, (((1,), (1,)), ((), ())), preferred_element_type=F32)


def _dot_tn(a, b):
    return lax.dot_general(a, b, (((0,), (0,)), ((), ())), preferred_element_type=F32)


def _ffn_body(x_ref, g_ref, w1_ref, w3_ref, w2_ref, fg_ref, o_ref, h_ref, acc_ref, *, final_norm, nk, tf):
    k = pl.program_id(1)
    tail = D_FF - (nk - 1) * tf

    @pl.when(k == 0)
    def _():
        h_ref[...] = _rms(x_ref[...], g_ref[...]).astype(BF16)
        acc_ref[...] = jnp.zeros_like(acc_ref)

    def partial_sum(width):
        h = h_ref[...]
        a = _dot(h, w1_ref[:, :width].astype(BF16))
        b = _dot(h, w3_ref[:, :width].astype(BF16))
        return _dot((_silu(a) * b).astype(BF16), w2_ref[:width, :].astype(BF16))

    @pl.when(k < nk - 1)
    def _():
        acc_ref[...] += partial_sum(tf)

    @pl.when(k == nk - 1)
    def _():
        y = x_ref[...] + 0.5 * (acc_ref[...] + partial_sum(tail))
        if final_norm:
            y = _rms(y, fg_ref[...])
        o_ref[...] = y


def ffn_call(x, g, w1, w3, w2, fg, *, tm, layer, final_norm, tf=FF_TILE):
    m = x.shape[0]
    nk = pl.cdiv(D_FF, tf)
    assert m % tm == 0 and w1.shape[1:] == (D_MODEL, D_FF) and w2.shape[1:] == (D_FF, D_MODEL)
    once = pl.Buffered(1)
    return pl.pallas_call(
        functools.partial(_ffn_body, final_norm=final_norm, nk=nk, tf=tf),
        grid=(m // tm, nk),
        in_specs=[
            pl.BlockSpec((tm, D_MODEL), lambda i, k: (i, 0), pipeline_mode=once),
            pl.BlockSpec((1, D_MODEL), lambda i, k: (0, 0)),
            pl.BlockSpec((None, D_MODEL, tf), lambda i, k: (layer, 0, k)),
            pl.BlockSpec((None, D_MODEL, tf), lambda i, k: (layer, 0, k)),
            pl.BlockSpec((None, tf, D_MODEL), lambda i, k: (layer, k, 0)),
            pl.BlockSpec((1, D_MODEL), lambda i, k: (0, 0)),
        ],
        out_specs=pl.BlockSpec((tm, D_MODEL), lambda i, k: (i, 0), pipeline_mode=once),
        out_shape=jax.ShapeDtypeStruct((m, D_MODEL), F32),
        scratch_shapes=[pltpu.VMEM((tm, D_MODEL), BF16), pltpu.VMEM((tm, D_MODEL), F32)],
        compiler_params=_cparams(("parallel", "arbitrary")),
        name="ffn",
    )(x, g, w1, w3, w2, fg)


IN_TILE = P_W // 2


def _inproj_body(x_ref, g_ref, w_ref, o_ref, fkv_ref, nkv_ref, win_ref, h_ref, *, keys_minor):
    j = pl.program_id(1)

    @pl.when(j == 0)
    def _():
        h_ref[...] = _rms(x_ref[...], g_ref[...]).astype(BF16)
        o_ref[...] = _dot_nt(h_ref[...], w_ref[...])

    @pl.when(j == 1)
    def _():
        y = _dot_nt(h_ref[...], w_ref[...])
        o_ref[...] = y
        fkv = y[:, C_FK - IN_TILE:C_FK - IN_TILE + 2 * GW]
        nkv = y[:, C_NKV - IN_TILE:C_NKV - IN_TILE + 4 * LANES]
        win = y[:, C_NKV - IN_TILE + 4 * LANES:C_NKV - IN_TILE + 6 * LANES]
        fkv_ref[...] = fkv.T if keys_minor else fkv
        nkv_ref[...] = nkv.T if keys_minor else nkv
        win_ref[...] = win.T if keys_minor else win


def inproj_call(x, g, w, *, tm, keys_minor=None):
    m = x.shape[0]
    assert m % tm == 0 and IN_TILE <= C_FK
    widths = (2 * GW, 4 * LANES, 2 * LANES)
    if keys_minor is None:
        side_specs = [pl.BlockSpec((tm, wd), lambda i, j: (i, 0)) for wd in widths]
        side_shapes = [jax.ShapeDtypeStruct((m, wd), F32) for wd in widths]
    else:
        bsz, t_len = keys_minor
        nt = t_len // tm
        assert t_len % tm == 0 and m == bsz * t_len
        side_specs = [pl.BlockSpec((None, wd, tm), lambda i, j: (i // nt, 0, i % nt)) for wd in widths]
        side_shapes = [jax.ShapeDtypeStruct((bsz, wd, t_len), F32) for wd in widths]
    return pl.pallas_call(
        functools.partial(_inproj_body, keys_minor=keys_minor is not None),
        grid=(m // tm, P_W // IN_TILE),
        in_specs=[
            pl.BlockSpec((tm, D_MODEL), lambda i, j: (i, 0)),
            pl.BlockSpec((1, D_MODEL), lambda i, j: (0, 0)),
            pl.BlockSpec((IN_TILE, D_MODEL), lambda i, j: (j, 0)),
        ],
        out_specs=[pl.BlockSpec((tm, IN_TILE), lambda i, j: (i, j))] + side_specs,
        out_shape=[jax.ShapeDtypeStruct((m, P_W), F32)] + side_shapes,
        scratch_shapes=[pltpu.VMEM((tm, D_MODEL), BF16)],
        compiler_params=_cparams(("parallel", "arbitrary")),
        name="inproj",
    )(x, g, w)


def _outproj_body(x_ref, a_ref, b_ref, c_ref, d_ref, gn_ref, w_ref, o_ref):
    y = x_ref[...]
    for i, r in enumerate((a_ref, b_ref, c_ref, d_ref)):
        y = y + _dot(_rms(r[...], gn_ref[i:i + 1, :]).astype(BF16), w_ref[i * GW:(i + 1) * GW, :].astype(BF16))
    o_ref[...] = y


def outproj_call(x, ya, yb, yc, yd, gains, w, *, tm, layer):
    m = x.shape[0]
    assert m % tm == 0
    yspec = pl.BlockSpec((tm, GW), lambda i: (i, 0))
    return pl.pallas_call(
        _outproj_body,
        grid=(m // tm,),
        in_specs=[
            pl.BlockSpec((tm, D_MODEL), lambda i: (i, 0)),
            yspec, yspec, yspec, yspec,
            pl.BlockSpec((4, GW), lambda i: (0, 0)),
            pl.BlockSpec((None, D_MODEL, D_MODEL), lambda i: (layer, 0, 0), pipeline_mode=pl.Buffered(1)),
        ],
        out_specs=pl.BlockSpec((tm, D_MODEL), lambda i: (i, 0)),
        out_shape=jax.ShapeDtypeStruct((m, D_MODEL), F32),
        compiler_params=_cparams(("parallel",)),
        name="outproj",
    )(x, ya, yb, yc, yd, gains, w)


def _ssd_body(xbc_ref, dt_ref, z_ref, conv0_ref, ssm0_ref, cw_ref, cb_ref, dtb_ref, alog_ref, dd_ref,
              y_ref, ssm_ref, conv_ref, xp_ref, act_ref, st_ref, *, n_real):
    c = pl.program_id(1)
    nc = pl.num_programs(1)
    q = SSD_CHUNK
    halo = SUBLANES

    @pl.when(c == 0)
    def _():
        xp_ref[halo - 3:halo, :] = conv0_ref[...]
        st_ref[...] = ssm0_ref[...]

    xr = xbc_ref[...]
    xp_ref[halo:halo + q, :] = xr
    conv = (cb_ref[...] + cw_ref[3:4, :] * xr
            + cw_ref[2:3, :] * xp_ref[halo - 1:halo - 1 + q, :]
            + cw_ref[1:2, :] * xp_ref[halo - 2:halo - 2 + q, :]
            + cw_ref[0:1, :] * xp_ref[halo - 3:halo - 3 + q, :])
    act_ref[...] = _silu(conv)

    row = lax.broadcasted_iota(jnp.int32, (q, LANES), 0)
    dt = jnp.where(row < n_real, _softplus(dt_ref[...] + dtb_ref[...]), 0.0)
    a = -jnp.exp(alog_ref[...])
    ti = lax.broadcasted_iota(jnp.int32, (q, q), 0)
    si = lax.broadcasted_iota(jnp.int32, (q, q), 1)
    causal = si <= ti
    acs = jnp.dot(causal.astype(F32), dt * a, preferred_element_type=F32, precision=HIGHEST)
    acs_t = acs.T
    e_acs = jnp.exp(acs)
    acs_last = acs[q - 1:q, :]
    w_end = jnp.exp(acs_last - acs) * dt
    e_last = jnp.exp(acs_last)

    for g in range(SSD_GROUPS):
        bm = act_ref[:, GW + g * SSD_STATE:GW + (g + 1) * SSD_STATE]
        cm = act_ref[:, GW + (SSD_GROUPS + g) * SSD_STATE:GW + (SSD_GROUPS + g + 1) * SSD_STATE]
        bm16 = bm.astype(BF16)
        cm16 = cm.astype(BF16)
        cb = _dot_nt(cm16, bm16)
        for hh in range(SSD_HEADS // SSD_GROUPS):
            h = g * (SSD_HEADS // SSD_GROUPS) + hh
            xs = act_ref[:, h * HEAD_DIM:(h + 1) * HEAD_DIM]
            seg = acs[:, h:h + 1] - acs_t[h:h + 1, :]
            decay = jnp.exp(jnp.where(causal, seg, -jnp.inf))
            y = _dot((cb * decay).astype(BF16), (xs * dt[:, h:h + 1]).astype(BF16))
            s_in = st_ref[h]
            y = y + _dot_nt(cm16, s_in.astype(BF16)) * e_acs[:, h:h + 1]
            y = y + dd_ref[:, h:h + 1] * xs
            cs = _dot_tn((xs * w_end[:, h:h + 1]).astype(BF16), bm16)
            st_ref[h] = e_last[:, h:h + 1] * s_in + cs
            zs = z_ref[:, h * HEAD_DIM:(h + 1) * HEAD_DIM]
            y_ref[:, h * HEAD_DIM:(h + 1) * HEAD_DIM] = y * _silu(zs)

    last_real = min(n_real, q)
    conv_ref[...] = xp_ref[halo + last_real - 3:halo + last_real, :]
    xp_ref[halo - 3:halo, :] = xp_ref[halo + q - 3:halo + q, :]

    @pl.when(c == nc - 1)
    def _():
        ssm_ref[...] = st_ref[...]


def ssd_call(p, conv0, ssm0, cw, cb, dtb, alog, dd, *, bsz, t_len, n_real):
    q = SSD_CHUNK
    nc = t_len // q
    assert t_len % q == 0 and (nc == 1 or n_real == q)
    row = lambda b, c: b * nc + c
    vec = lambda shape: pl.BlockSpec(shape, lambda b, c: (0, 0))
    return pl.pallas_call(
        functools.partial(_ssd_body, n_real=n_real),
        grid=(bsz, nc),
        in_specs=[
            pl.BlockSpec((q, SSD_CONV_DIM), lambda b, c: (row(b, c), C_XBC // SSD_CONV_DIM)),
            pl.BlockSpec((q, LANES), lambda b, c: (row(b, c), C_DT // LANES)),
            pl.BlockSpec((q, GW), lambda b, c: (row(b, c), C_Z // GW)),
            pl.BlockSpec((None, CONV_W - 1, SSD_CONV_DIM), lambda b, c: (b, 0, 0)),
            pl.BlockSpec((None, SSD_HEADS, HEAD_DIM, SSD_STATE), lambda b, c: (b, 0, 0, 0)),
            vec((CONV_W, SSD_CONV_DIM)), vec((1, SSD_CONV_DIM)), vec((1, LANES)), vec((1, LANES)), vec((1, LANES)),
        ],
        out_specs=[
            pl.BlockSpec((q, GW), lambda b, c: (row(b, c), 0)),
            pl.BlockSpec((None, SSD_HEADS, HEAD_DIM, SSD_STATE), lambda b, c: (b, 0, 0, 0)),
            pl.BlockSpec((None, CONV_W - 1, SSD_CONV_DIM), lambda b, c: (b, 0, 0)),
        ],
        out_shape=[
            jax.ShapeDtypeStruct((bsz * t_len, GW), F32),
            jax.ShapeDtypeStruct((bsz, SSD_HEADS, HEAD_DIM, SSD_STATE), F32),
            jax.ShapeDtypeStruct((bsz, CONV_W - 1, SSD_CONV_DIM), F32),
        ],
        scratch_shapes=[
            pltpu.VMEM((SUBLANES + q, SSD_CONV_DIM), F32),
            pltpu.VMEM((q, SSD_CONV_DIM), F32),
            pltpu.VMEM((SSD_HEADS, HEAD_DIM, SSD_STATE), F32),
        ],
        compiler_params=_cparams(("parallel", "arbitrary")),
        name="ssd",
    )(p, p, p, conv0, ssm0, cw, cb, dtb, alog, dd)


def _s5_body(u_ref, bre_ref, bim_ref, are_ref, aim_ref, x0r_ref, x0i_ref, cre_ref, cim_ref, d_ref, wg_ref,
             o_ref, xr_out, xi_out, xr_ref, xi_ref, sr_ref, si_ref, mix_ref, *, nb, steps, t_last):
    c = pl.program_id(0)
    per = SUBLANES // nb
    tiles = steps * nb // SUBLANES
    n_chunk = GW // LANES

    @pl.when(c == 0)
    def _():
        sr_ref[...] = x0r_ref[...]
        si_ref[...] = x0i_ref[...]

    for b in range(nb):
        for cc in range(n_chunk):
            mix_ref[cc, pl.ds(b, steps, stride=nb), :] = u_ref[b, :, cc * LANES:(cc + 1) * LANES]
    u = jnp.concatenate([mix_ref[cc] for cc in range(n_chunk)], axis=1)
    u16 = u.astype(BF16)
    xr_ref[...] = _dot(u16, bre_ref[...])
    xi_ref[...] = _dot(u16, bim_ref[...])
    ar = are_ref[...]
    ai = aim_ref[...]
    first = lax.broadcasted_iota(jnp.int32, (SUBLANES, S5_N), 0) < nb

    def step(j, carry):
        sr, si = carry
        rows = pl.ds(pl.multiple_of(j * SUBLANES, SUBLANES), SUBLANES)
        br = xr_ref[rows, :]
        bi = xi_ref[rows, :]
        vr = ar * sr - ai * si + br
        vi = ar * si + ai * sr + bi
        if per == 2:
            pr = pltpu.roll(vr, nb, 0)
            pi = pltpu.roll(vi, nb, 0)
            wr = ar * pr - ai * pi + br
            wi = ar * pi + ai * pr + bi
            outr = jnp.where(first, vr, wr)
            outi = jnp.where(first, vi, wi)
            nxt = (pltpu.roll(wr, nb, 0), pltpu.roll(wi, nb, 0))
        else:
            outr, outi, nxt = vr, vi, (vr, vi)
        xr_ref[rows, :] = outr
        xi_ref[rows, :] = outi

        @pl.when(c * tiles + j == t_last // per)
        def _():
            xr_out[...] = outr
            xi_out[...] = outi

        return nxt

    sr, si = lax.fori_loop(0, tiles, step, (sr_ref[...], si_ref[...]))
    sr_ref[...] = sr
    si_ref[...] = si

    y = _dot(xr_ref[...].astype(BF16), cre_ref[...]) - _dot(xi_ref[...].astype(BF16), cim_ref[...])
    y = y + d_ref[...] * u
    g = _dot(jax.nn.gelu(y).astype(BF16), wg_ref[...])
    o = g[:, :GW] * _sigmoid(g[:, GW:])
    for cc in range(n_chunk):
        mix_ref[cc] = o[:, cc * LANES:(cc + 1) * LANES]
    for b in range(nb):
        o_ref[b] = jnp.concatenate([mix_ref[cc, pl.ds(b, steps, stride=nb), :] for cc in range(n_chunk)], axis=1)


def s5_call(p3, bre, bim, are, aim, x0r, x0i, cre, cim, d, wg, *, t_len, steps, t_last):
    nb = p3.shape[0]
    assert t_len % steps == 0 and nb in (4, 8) and (steps * nb) % SUBLANES == 0
    rows = steps * nb
    const = lambda shape: pl.BlockSpec(shape, lambda c: (0, 0))
    return pl.pallas_call(
        functools.partial(_s5_body, nb=nb, steps=steps, t_last=t_last),
        grid=(t_len // steps,),
        in_specs=[
            pl.BlockSpec((nb, steps, GW), lambda c: (0, c, C_U // GW)),
            const((GW, S5_N)), const((GW, S5_N)), const((1, S5_N)), const((1, S5_N)),
            const((SUBLANES, S5_N)), const((SUBLANES, S5_N)),
            const((S5_N, GW)), const((S5_N, GW)), const((1, GW)), const((GW, 2 * GW)),
        ],
        out_specs=[pl.BlockSpec((nb, steps, GW), lambda c: (0, c, 0)), const((SUBLANES, S5_N)), const((SUBLANES, S5_N))],
        out_shape=[jax.ShapeDtypeStruct((nb, t_len, GW), F32),
                   jax.ShapeDtypeStruct((SUBLANES, S5_N), F32), jax.ShapeDtypeStruct((SUBLANES, S5_N), F32)],
        scratch_shapes=[pltpu.VMEM((rows, S5_N), F32), pltpu.VMEM((rows, S5_N), F32),
                        pltpu.VMEM((SUBLANES, S5_N), F32), pltpu.VMEM((SUBLANES, S5_N), F32),
                        pltpu.VMEM((GW // LANES, rows, LANES), F32)],
        compiler_params=_cparams(("arbitrary",)),
        name="s5",
    )(p3, bre, bim, are, aim, x0r, x0i, cre, cim, d, wg)


def s5_params(lam_re, lam_im, log_dt, b_re, b_im, c_re, c_im):
    dt = jnp.exp(log_dt)[:, None]
    mag = jnp.exp(lam_re * dt)
    ab_re = mag * jnp.cos(lam_im * dt)
    ab_im = mag * jnp.sin(lam_im * dt)
    den = lam_re * lam_re + lam_im * lam_im
    zr = ((ab_re - 1.0) * lam_re + ab_im * lam_im) / den
    zi = (ab_im * lam_re - (ab_re - 1.0) * lam_im) / den
    bb_re = zr[..., None] * b_re - zi[..., None] * b_im
    bb_im = zr[..., None] * b_im + zi[..., None] * b_re
    eye = jnp.eye(S5_GROUPS, dtype=F32)

    def in_mat(bb):
        return jnp.einsum('gnc,gh->gchn', bb, eye).reshape(GW, S5_N).astype(BF16)

    def out_mat(cc):
        return jnp.einsum('gcn,gh->gnhc', cc, eye).reshape(S5_N, GW).astype(BF16)

    return (in_mat(bb_re), in_mat(bb_im), ab_re.reshape(1, S5_N), ab_im.reshape(1, S5_N),
            out_mat(c_re), out_mat(c_im))


LOG2E = 1.4426950408889634
NEG = -1e30


def _fox_prep_body(f_ref, b_ref, lf_ref, cumt_ref, carry_ref):
    c = pl.program_id(1)

    @pl.when(c == 0)
    def _():
        carry_ref[...] = jnp.zeros_like(carry_ref)

    lf = _log_sigmoid(f_ref[...] + b_ref[...])
    lf_ref[...] = lf.T[:FOX_HEADS, :]
    tc = lf.shape[0]
    ti = lax.broadcasted_iota(jnp.int32, (tc, tc), 0)
    si = lax.broadcasted_iota(jnp.int32, (tc, tc), 1)
    cum = jnp.dot((si <= ti).astype(F32), lf, preferred_element_type=F32, precision=HIGHEST) + carry_ref[...]
    cumt_ref[...] = cum.T[:FOX_HEADS, :]
    carry_ref[...] = cum[tc - 1:tc, :]


def fox_prep_call(p, bias, *, bsz, t_len, tc=256):
    nc = t_len // tc
    assert t_len % tc == 0
    head_rows = pl.BlockSpec((None, FOX_HEADS, tc), lambda b, c: (b, 0, c))
    return pl.pallas_call(
        _fox_prep_body,
        grid=(bsz, nc),
        in_specs=[pl.BlockSpec((tc, LANES), lambda b, c: (b * nc + c, C_FF // LANES)),
                  pl.BlockSpec((1, LANES), lambda b, c: (0, 0))],
        out_specs=[head_rows, head_rows],
        out_shape=[jax.ShapeDtypeStruct((bsz, FOX_HEADS, t_len), F32),
                   jax.ShapeDtypeStruct((bsz, FOX_HEADS, t_len), F32)],
        scratch_shapes=[pltpu.VMEM((1, LANES), F32)],
        compiler_params=_cparams(("parallel", "arbitrary")),
        name="fox_prep",
    )(p, bias)


FOX_PAIRS_PER_STEP = 4


def _fox_attn_body(q_ref, k_ref, v_ref, cumt_ref, o_ref, *, tq, tk):
    hq = pl.program_id(1)
    qi = pl.program_id(2)
    q0 = qi * tq
    npair = FOX_PAIRS_PER_STEP
    lane = lax.broadcasted_iota(jnp.int32, (tq, LANES), 1)
    t_pos = q0 + lax.broadcasted_iota(jnp.int32, (tq, tk), 0)
    s_off = lax.broadcasted_iota(jnp.int32, (tq, tk), 1)
    qs = []
    for pp in range(npair):
        q = q_ref[:, pp * LANES:(pp + 1) * LANES] * (HEAD_DIM ** -0.5 * LOG2E)
        qs.append(jnp.concatenate([jnp.where(lane < HEAD_DIM, q, 0.0), jnp.where(lane >= HEAD_DIM, q, 0.0)],
                                  axis=0).astype(BF16))

    def chunk(kc, carry, masked):
        ks = pl.ds(pl.multiple_of(kc * tk, tk), tk)
        out = []
        for pp in range(npair):
            s = _dot_nt(qs[pp], k_ref[ks, pp * LANES:(pp + 1) * LANES].astype(BF16))
            vv = v_ref[ks, pp * LANES:(pp + 1) * LANES].astype(BF16)
            for hh in range(2):
                m, l, acc = carry[2 * pp + hh]
                head = 2 * (hq * npair + pp) + hh
                sh = s[hh * tq:(hh + 1) * tq] - cumt_ref[pl.ds(head, 1), ks] * LOG2E
                if masked:
                    sh = jnp.where(kc * tk + s_off <= t_pos, sh, NEG)
                m_new = jnp.maximum(m, jnp.max(sh, axis=1, keepdims=True))
                alpha = jnp.exp2(m - m_new)
                pr = jnp.exp2(sh - m_new)
                l = alpha * l + jnp.sum(pr, axis=1, keepdims=True)
                acc = alpha * acc + _dot(pr.astype(BF16), vv)
                out.append((m_new, l, acc))
        return tuple(out)

    init = tuple((jnp.full((tq, 1), NEG, F32), jnp.zeros((tq, 1), F32), jnp.zeros((tq, LANES), F32))
                 for _ in range(2 * npair))
    n_full = q0 // tk
    carry = lax.fori_loop(0, n_full, lambda kc, c: chunk(kc, c, False), init)
    res = chunk(n_full, carry, True)
    for pp in range(npair):
        (_, la, acca), (_, lb, accb) = res[2 * pp], res[2 * pp + 1]
        o_ref[:, pp * LANES:(pp + 1) * LANES] = jnp.where(lane < HEAD_DIM, acca / la, accb / lb)


def fox_attn_call(p, cumt, *, bsz, t_len, tq=128, tk=512):
    tk = min(tk, t_len)
    nq = t_len // tq
    assert t_len % tk == 0 and tk % tq == 0
    width = FOX_PAIRS_PER_STEP * LANES
    assert GW % width == 0
    return pl.pallas_call(
        functools.partial(_fox_attn_body, tq=tq, tk=tk),
        grid=(bsz, GW // width, nq),
        in_specs=[
            pl.BlockSpec((tq, width), lambda b, hq, qi: (b * nq + qi, C_FQ // width + hq)),
            pl.BlockSpec((t_len, width), lambda b, hq, qi: (b, C_FK // width + hq)),
            pl.BlockSpec((t_len, width), lambda b, hq, qi: (b, C_FV // width + hq)),
            pl.BlockSpec((None, FOX_HEADS, t_len), lambda b, hq, qi: (b, 0, 0)),
        ],
        out_specs=pl.BlockSpec((tq, width), lambda b, hq, qi: (b * nq + qi, hq)),
        out_shape=jax.ShapeDtypeStruct((bsz * t_len, GW), F32),
        compiler_params=_cparams(("parallel", "parallel", "arbitrary")),
        name="fox_attn",
    )(p, p, p, cumt)


def _compress_rows(xk_ref, xv_ref, pe_ref, w_ref, nbc):
    half = nbc // 2
    acc = jnp.zeros((nbc, 2 * LANES), F32)
    for r in range(CMP_BLOCK):
        ev = pl.ds(r, half, stride=2 * CMP_BLOCK)
        od = pl.ds(CMP_BLOCK + r, half, stride=2 * CMP_BLOCK)
        rows = jnp.concatenate([jnp.concatenate([xk_ref[ev, :], xv_ref[ev, :]], axis=1),
                                jnp.concatenate([xk_ref[od, :], xv_ref[od, :]], axis=1)], axis=0)
        acc = acc + _dot((rows + pe_ref[r:r + 1, :]).astype(BF16), w_ref[r])
    return acc


def _nsa_cmp_p_body(xk_ref, xv_ref, pe_ref, w_ref, o_ref, *, nbc, hp):
    half = nbc // 2
    acc = _compress_rows(xk_ref, xv_ref, pe_ref, w_ref, nbc)
    o_ref[...] = jnp.zeros_like(o_ref)
    o_ref[0:half, :] = acc[0:half]
    o_ref[hp:hp + half, :] = acc[half:nbc]


def nsa_cmp_p_call(p, pe4, w4, *, bsz, t_len):
    nbc = t_len // CMP_BLOCK
    hp = max(nbc // 2, HEAD_DIM)
    return pl.pallas_call(
        functools.partial(_nsa_cmp_p_body, nbc=nbc, hp=hp),
        grid=(bsz,),
        in_specs=[pl.BlockSpec((t_len, LANES), lambda b: (b, C_NKV // LANES)),
                  pl.BlockSpec((t_len, LANES), lambda b: (b, C_NKV // LANES + 1)),
                  pl.BlockSpec((CMP_BLOCK, 2 * LANES), lambda b: (0, 0)),
                  pl.BlockSpec((CMP_BLOCK, 2 * LANES, 2 * LANES), lambda b: (0, 0, 0))],
        out_specs=pl.BlockSpec((None, 2 * hp, 2 * LANES), lambda b: (b, 0, 0)),
        out_shape=jax.ShapeDtypeStruct((bsz, 2 * hp, 2 * LANES), F32),
        compiler_params=_cparams(("parallel",)),
        name="nsa_cmp_p",
    )(p, p, pe4, w4)


def _cmp_attend(qs, cmp_ref, g, t_pos, blk, slot_ok):
    scale = HEAD_DIM ** -0.5
    kc = cmp_ref[:, g * HEAD_DIM:(g + 1) * HEAD_DIM].astype(BF16)
    vc = cmp_ref[:, 2 * HEAD_DIM + g * HEAD_DIM:2 * HEAD_DIM + (g + 1) * HEAD_DIM].astype(BF16)
    s = _dot_nt(qs, kc) * scale
    valid = slot_ok & ((blk + 1) * CMP_BLOCK - 1 <= t_pos)
    s = jnp.where(valid, s, -jnp.inf)
    m = jnp.max(s, axis=1, keepdims=True)
    m = jnp.where(m > -jnp.inf, m, 0.0)
    e = jnp.where(valid, jnp.exp(s - m), 0.0)
    pc = e / jnp.maximum(jnp.sum(e, axis=1, keepdims=True), TINY)
    return _dot(pc.astype(BF16), vc), pc


def _select_score(imp, t_pos, blk):
    cur = t_pos // SLC_BLOCK
    forced = (blk == 0) | (blk == cur) | (blk == cur - 1)
    avail = blk * SLC_BLOCK <= t_pos
    return jnp.where(avail, jnp.where(forced, FORCE_SCORE, imp), -1.0)


def _nsa_attn_p_body(q_ref, cmp_ref, ks_ref, vs_ref, kw_ref, vw_ref, gl_ref, o_ref, *, tq, tk, nbc, hp):
    qi = pl.program_id(1)
    hg_n = NSA_GROUP
    rows = hg_n * tq
    half = nbc // 2
    nbs = (nbc + 1) // 2
    q0 = qi * tq
    scale = HEAD_DIM ** -0.5
    t_col = q0 + lax.broadcasted_iota(jnp.int32, (tq, 1), 0)
    t_lane = q0 + lax.broadcasted_iota(jnp.int32, (1, tq), 1)
    t_lane_stack = jnp.concatenate([t_lane] * hg_n, axis=1)
    gate = _sigmoid(gl_ref[...])
    crow = lax.broadcasted_iota(jnp.int32, (2 * hp, 1), 0)
    slot = jnp.where(crow < hp, crow, crow - hp)
    cmp_blk = 2 * slot + jnp.where(crow < hp, 0, 1)
    slc_blk = lax.broadcasted_iota(jnp.int32, (hp, 1), 0)

    def add_bias(s, bias):
        n = s.shape[1]
        return (s.reshape(hg_n, tq, n) + bias[None]).reshape(rows, n)

    per_g = []
    for g in range(NSA_KV_HEADS):
        q_f32 = jnp.concatenate(
            [q_ref[:, (g * hg_n + hg) * HEAD_DIM:(g * hg_n + hg + 1) * HEAD_DIM] for hg in range(hg_n)], axis=0)
        qs = q_f32.astype(BF16)
        qs2 = (q_f32 * (scale * LOG2E)).astype(BF16)

        kc = cmp_ref[:, g * HEAD_DIM:(g + 1) * HEAD_DIM].astype(BF16)
        vc = cmp_ref[:, 2 * HEAD_DIM + g * HEAD_DIM:2 * HEAD_DIM + (g + 1) * HEAD_DIM].astype(BF16)
        st = _dot_nt(kc, qs) * scale
        valid = (slot < half) & ((cmp_blk + 1) * CMP_BLOCK - 1 <= t_lane_stack)
        st = jnp.where(valid, st, -jnp.inf)
        m = jnp.max(st, axis=0, keepdims=True)
        m = jnp.where(m > -jnp.inf, m, 0.0)
        e = jnp.where(valid, jnp.exp(st - m), 0.0)
        pt = e / jnp.maximum(jnp.sum(e, axis=0, keepdims=True), TINY)
        o_cmp = _dot_tn(pt.astype(BF16), vc)

        imp = pt[:, 0:tq]
        for hg in range(1, hg_n):
            imp = imp + pt[:, hg * tq:(hg + 1) * tq]
        imp = imp[0:hp] + imp[hp:2 * hp]
        score = _select_score(imp, t_lane, slc_blk)
        rank = jnp.zeros((hp, tq), jnp.int32)
        for i in range(nbs):
            row = score[i:i + 1, :]
            rank = rank + ((row > score) | ((row == score) & (i < slc_blk))).astype(jnp.int32)
        sel_t = ((rank < TOP_N) & (score >= 0.0)).astype(BF16)

        per_g.append((qs2, o_cmp, sel_t))

    def slc_chunk(kc_i, carry):
        ks = pl.ds(pl.multiple_of(kc_i * tk, tk), tk)
        jb = lax.broadcasted_iota(jnp.int32, (hp, tk), 0)
        sp = kc_i * tk + lax.broadcasted_iota(jnp.int32, (hp, tk), 1)
        expand = (jb == sp // SLC_BLOCK).astype(BF16)
        causal = kc_i * tk + lax.broadcasted_iota(jnp.int32, (tq, tk), 1) <= t_col
        out = []
        for g in range(NSA_KV_HEADS):
            qs2, _, sel_t = per_g[g]
            m, l, acc = carry[g]
            hit = _dot_tn(sel_t, expand) > 0.5
            bias = jnp.where(hit & causal, 0.0, NEG)
            s = add_bias(_dot_nt(qs2, ks_ref[ks, g * HEAD_DIM:(g + 1) * HEAD_DIM].astype(BF16)), bias)
            m_new = jnp.maximum(m, jnp.max(s, axis=1, keepdims=True))
            alpha = jnp.exp2(m - m_new)
            pr = jnp.exp2(s - m_new)
            l = alpha * l + jnp.sum(pr, axis=1, keepdims=True)
            acc = alpha * acc + _dot(pr.astype(BF16), vs_ref[ks, g * HEAD_DIM:(g + 1) * HEAD_DIM].astype(BF16))
            out.append((m_new, l, acc))
        return tuple(out)

    init = tuple((jnp.full((rows, 1), NEG, F32), jnp.zeros((rows, 1), F32), jnp.zeros((rows, HEAD_DIM), F32))
                 for _ in range(NSA_KV_HEADS))
    slc = lax.fori_loop(0, (q0 + tq + tk - 1) // tk, slc_chunk, init)

    span = WINDOW + tq
    start = pl.multiple_of(jnp.maximum(q0 - WINDOW, 0), tq)
    ws = pl.ds(start, span)
    diff = t_col - (start + lax.broadcasted_iota(jnp.int32, (tq, span), 1))
    win_bias = jnp.where((diff >= 0) & (diff < WINDOW), 0.0, NEG)
    for g in range(NSA_KV_HEADS):
        qs2, o_cmp, _ = per_g[g]
        _, l, acc = slc[g]
        o_slc = acc / l
        s = add_bias(_dot_nt(qs2, kw_ref[ws, g * HEAD_DIM:(g + 1) * HEAD_DIM].astype(BF16)), win_bias)
        pr = jnp.exp2(s - jnp.max(s, axis=1, keepdims=True))
        o_win = (_dot(pr.astype(BF16), vw_ref[ws, g * HEAD_DIM:(g + 1) * HEAD_DIM].astype(BF16))
                 / jnp.sum(pr, axis=1, keepdims=True))
        for hg in range(hg_n):
            h = g * hg_n + hg
            rs = slice(hg * tq, (hg + 1) * tq)
            o = (gate[:, 3 * h:3 * h + 1] * o_cmp[rs] + gate[:, 3 * h + 1:3 * h + 2] * o_slc[rs]
                 + gate[:, 3 * h + 2:3 * h + 3] * o_win[rs])
            o_ref[:, h * HEAD_DIM:(h + 1) * HEAD_DIM] = o


def nsa_attn_p_call(p, cmp, *, bsz, t_len, tq=128, tk=512):
    nq = t_len // tq
    nbc = t_len // CMP_BLOCK
    hp = cmp.shape[1] // 2
    assert t_len % tk == 0 and WINDOW % tq == 0 and t_len >= WINDOW + tq
    kv = lambda off: pl.BlockSpec((t_len, LANES), lambda b, qi: (b, (C_NKV + off) // LANES))
    return pl.pallas_call(
        functools.partial(_nsa_attn_p_body, tq=tq, tk=tk, nbc=nbc, hp=hp),
        grid=(bsz, nq),
        in_specs=[
            pl.BlockSpec((tq, GW), lambda b, qi: (b * nq + qi, C_NQ // GW)),
            pl.BlockSpec((None, 2 * hp, 2 * LANES), lambda b, qi: (b, 0, 0)),
            kv(2 * LANES), kv(3 * LANES), kv(4 * LANES), kv(5 * LANES),
            pl.BlockSpec((tq, LANES), lambda b, qi: (b * nq + qi, C_NG // LANES)),
        ],
        out_specs=pl.BlockSpec((tq, GW), lambda b, qi: (b * nq + qi, 0)),
        out_shape=jax.ShapeDtypeStruct((bsz * t_len, GW), F32),
        compiler_params=_cparams(("parallel", "arbitrary")),
        name="nsa_attn_p",
    )(p, cmp, p, p, p, p, p)


def nsa_cmp_params(pe, cw):
    pe4 = jnp.concatenate([pe[0], pe[0], pe[1], pe[1]], axis=1)
    wk = cw[0].reshape(CMP_BLOCK, HEAD_DIM, HEAD_DIM)
    wv = cw[1].reshape(CMP_BLOCK, HEAD_DIM, HEAD_DIM)
    z = jnp.zeros_like(wk)
    rows = [jnp.concatenate([m if i == j else z for j in range(4)], axis=2) for i, m in enumerate((wk, wk, wv, wv))]
    return pe4, jnp.concatenate(rows, axis=1).astype(BF16)


def nsa_cmp_params_t(pe, cw):
    per_page = PAGE // CMP_BLOCK
    s_blk = lax.broadcasted_iota(jnp.int32, (PAGE, per_page * HEAD_DIM), 0) // CMP_BLOCK
    c_blk = lax.broadcasted_iota(jnp.int32, (PAGE, per_page * HEAD_DIM), 1) // HEAD_DIM

    def mat(w):
        wd = jnp.transpose(w.reshape(CMP_BLOCK, HEAD_DIM, HEAD_DIM), (1, 0, 2))
        full = jnp.where(s_blk == c_blk, jnp.tile(wd, (1, per_page, per_page)), 0.0)
        return full.astype(BF16).reshape(HEAD_DIM // 2, 2 * PAGE, per_page * HEAD_DIM)

    bias = lambda x: jnp.tile(x.T, (1, per_page))
    return bias(pe[0]), bias(pe[1]), mat(cw[0]), mat(cw[1])


T_PAD = SUBLANES
FOX_LF_PAGES = 16
FOX_KV_PAGES = 16
NSA_CMP_PAGES = 16


def _page_spec(block, layer, pages_per_step, i, tail, first_step=0):
    def index_map(b, s, pt):
        j = jnp.maximum(s - first_step, 0) * pages_per_step + i
        return (pt[b, j], layer) + tail
    return pl.BlockSpec(block, index_map)


def _fox_prep_s_body(pt_ref, *refs, n_pages):
    pg = FOX_LF_PAGES
    page_refs, o_ref, a_ref = refs[:pg], refs[pg], refs[pg + 1]
    s = pl.program_id(1)
    for i in range(pg):
        a_ref[pl.ds(pl.multiple_of((s * pg + i) * FOX_HEADS, FOX_HEADS), FOX_HEADS), :] = page_refs[i][...]

    @pl.when(s == pl.num_programs(1) - 1)
    def _():
        n = n_pages * FOX_HEADS
        a = a_ref[...]
        ji = lax.broadcasted_iota(jnp.int32, (PAGE, PAGE), 0)
        si = lax.broadcasted_iota(jnp.int32, (PAGE, PAGE), 1)
        within = jnp.dot(a, (ji > si).astype(F32), preferred_element_type=F32, precision=HIGHEST)
        tot = jnp.broadcast_to(jnp.sum(a, axis=1, keepdims=True), (n, LANES))
        ri = lax.broadcasted_iota(jnp.int32, (n, n), 0)
        ci = lax.broadcasted_iota(jnp.int32, (n, n), 1)
        later = ((ci > ri) & ((ci - ri) % FOX_HEADS == 0)).astype(F32)
        o_ref[...] = within + jnp.dot(later, tot, preferred_element_type=F32, precision=HIGHEST)


def fox_prep_s_call(page_table, cache_lf, *, layer, bsz, n_pages):
    pg = FOX_LF_PAGES
    assert n_pages % pg == 0
    n = n_pages * FOX_HEADS
    grid_spec = pltpu.PrefetchScalarGridSpec(
        num_scalar_prefetch=1,
        grid=(bsz, n_pages // pg),
        in_specs=[_page_spec((None, None, FOX_HEADS, PAGE), layer, pg, i, (0, 0)) for i in range(pg)],
        out_specs=pl.BlockSpec((None, n, LANES), lambda b, s, pt: (b, 0, 0)),
        scratch_shapes=[pltpu.VMEM((n, LANES), F32)],
    )
    return pl.pallas_call(
        functools.partial(_fox_prep_s_body, n_pages=n_pages),
        grid_spec=grid_spec,
        out_shape=jax.ShapeDtypeStruct((bsz, n, LANES), F32),
        compiler_params=_cparams(("parallel", "arbitrary")),
        name="fox_prep_s",
    )(page_table, *([cache_lf] * pg))


def _fox_attn_s_body(pt_ref, q_ref, k_ref, v_ref, f_ref, b_ref, rp_ref, *refs, n_real):
    pg = FOX_KV_PAGES
    page_refs = refs[:pg]
    o_ref, lf_ref, qbd_ref, rqp_ref, m_ref, l_ref, acc_ref = refs[pg:]
    s = pl.program_id(1)
    scale = HEAD_DIM ** -0.5
    nh = FOX_HEADS
    rows = nh * T_PAD
    row_h = lax.broadcasted_iota(jnp.int32, (rows, 1), 0) // T_PAD
    row_t = lax.broadcasted_iota(jnp.int32, (rows, 1), 0) % T_PAD

    def rep_heads(x8):
        return jnp.concatenate([jnp.broadcast_to(x8[h:h + 1, :], (T_PAD, x8.shape[1])) for h in range(nh)], axis=0)

    def attend(scores, values, m, l, acc):
        tile_max = scores[0]
        for sc in scores[1:]:
            tile_max = jnp.maximum(tile_max, sc)
        m_new = jnp.maximum(m, jnp.max(tile_max, axis=1, keepdims=True))
        alpha = jnp.exp(m - m_new)
        acc = alpha * acc
        psum = None
        for sc, pv in zip(scores, values):
            pr = jnp.exp(sc - m_new)
            psum = pr if psum is None else psum + pr
            acc = acc + pv(pr.astype(BF16))
        return m_new, alpha * l + jnp.sum(psum, axis=1, keepdims=True), acc

    @pl.when(s == 0)
    def _():
        lf = _log_sigmoid(f_ref[...] + b_ref[...])
        lf_ref[...] = lf
        tok = lax.broadcasted_iota(jnp.int32, (T_PAD, LANES), 0)
        lfm = jnp.where(tok < n_real, lf, 0.0)
        r_new = jnp.zeros((T_PAD, LANES), F32)
        for j in range(1, n_real):
            r_new = r_new + jnp.where(tok < j, lfm[j:j + 1, :], 0.0)
        tot = jnp.sum(lfm, axis=0, keepdims=True)
        lane = lax.broadcasted_iota(jnp.int32, (rows, LANES), 1)
        pick = lane == row_h
        r_q = jnp.sum(jnp.where(pick, jnp.concatenate([r_new] * nh, axis=0), 0.0), axis=1, keepdims=True)
        t_q = jnp.sum(jnp.where(pick, jnp.broadcast_to(tot, (rows, LANES)), 0.0), axis=1, keepdims=True)
        rqp_ref[...] = r_q - t_q
        col = lax.broadcasted_iota(jnp.int32, (rows, GW), 1)
        qbd = jnp.where(col // HEAD_DIM == row_h, jnp.concatenate([q_ref[...]] * nh, axis=0), 0.0)
        qbd_ref[...] = qbd.astype(BF16)
        zrow = jnp.zeros((PAGE - T_PAD, GW), F32)
        kk = jnp.concatenate([k_ref[...], zrow], axis=0)
        vv = jnp.concatenate([v_ref[...], zrow], axis=0)
        r_pad = jnp.concatenate([r_new, jnp.zeros((PAGE - T_PAD, LANES), F32)], axis=0)
        bias = rep_heads(r_pad.T[:nh, :]) - r_q
        key = lax.broadcasted_iota(jnp.int32, (rows, PAGE), 1)
        mask = (key <= row_t) & (key < n_real)
        sc = jnp.where(mask, _dot_nt(qbd.astype(BF16), kk.astype(BF16)) * scale + bias, -jnp.inf)
        m0 = jnp.full((rows, 1), -jnp.inf, F32)
        m, l, acc = attend([sc], [lambda pr: _dot(pr, vv.astype(BF16))], m0,
                           jnp.zeros((rows, 1), F32), jnp.zeros((rows, GW), F32))
        m_ref[...] = m
        l_ref[...] = l
        acc_ref[...] = acc

    @pl.when(s > 0)
    def _():
        rqp = rqp_ref[...]
        qbd = qbd_ref[...]
        scores = [_dot(qbd, page_refs[i][0:GW, :].astype(BF16)) * scale
                  + (rep_heads(rp_ref[i * nh:(i + 1) * nh, :]) - rqp) for i in range(pg)]
        values = [lambda pr, i=i: _dot_nt(pr, page_refs[i][GW:2 * GW, :].astype(BF16)) for i in range(pg)]
        m, l, acc = attend(scores, values, m_ref[...], l_ref[...], acc_ref[...])
        m_ref[...] = m
        l_ref[...] = l
        acc_ref[...] = acc

    @pl.when(s == pl.num_programs(1) - 1)
    def _():
        col = lax.broadcasted_iota(jnp.int32, (rows, GW), 1)
        o = jnp.where(col // HEAD_DIM == row_h, acc_ref[...] / l_ref[...], 0.0)
        o_ref[...] = jnp.sum(o.reshape(nh, T_PAD, GW), axis=0)


def fox_attn_s_call(page_table, p, bias, r_past, cache_kv, *, layer, bsz, n_pages, n_real):
    pg = FOX_KV_PAGES
    assert n_pages % pg == 0
    rows = FOX_HEADS * T_PAD
    new = lambda width, off: pl.BlockSpec((T_PAD, width), lambda b, s, pt: (b, off // width))
    grid_spec = pltpu.PrefetchScalarGridSpec(
        num_scalar_prefetch=1,
        grid=(bsz, 1 + n_pages // pg),
        in_specs=[new(GW, C_FQ), new(GW, C_FK), new(GW, C_FV), new(LANES, C_FF),
                  pl.BlockSpec((1, LANES), lambda b, s, pt: (0, 0)),
                  pl.BlockSpec((None, pg * FOX_HEADS, LANES), lambda b, s, pt: (b, jnp.maximum(s - 1, 0), 0))]
                 + [_page_spec((None, None, 2 * GW, PAGE), layer, pg, i, (0, 0), first_step=1) for i in range(pg)],
        out_specs=[pl.BlockSpec((T_PAD, GW), lambda b, s, pt: (b, 0)),
                   pl.BlockSpec((T_PAD, LANES), lambda b, s, pt: (b, 0))],
        scratch_shapes=[pltpu.VMEM((rows, GW), BF16), pltpu.VMEM((rows, 1), F32), pltpu.VMEM((rows, 1), F32),
                        pltpu.VMEM((rows, 1), F32), pltpu.VMEM((rows, GW), F32)],
    )
    return pl.pallas_call(
        functools.partial(_fox_attn_s_body, n_real=n_real),
        grid_spec=grid_spec,
        out_shape=[jax.ShapeDtypeStruct((bsz * T_PAD, GW), F32), jax.ShapeDtypeStruct((bsz * T_PAD, LANES), F32)],
        compiler_params=_cparams(("parallel", "arbitrary")),
        name="fox_attn_s",
    )(page_table, p, p, p, p, bias, r_past, *([cache_kv] * pg))


def _nsa_cmp_s_body(pt_ref, *refs, n_pages):
    pg = NSA_CMP_PAGES
    k_pages, v_pages = refs[:pg], refs[pg:2 * pg]
    pek_ref, pev_ref, mk_ref, mv_ref, o_ref, xk_ref, xv_ref = refs[2 * pg:]
    s = pl.program_id(1)
    for i in range(pg):
        rows = pl.ds(pl.multiple_of((s * pg + i) * PAGE, PAGE), PAGE)
        xk_ref[rows, :] = k_pages[i][...]
        xv_ref[rows, :] = v_pages[i][...]

    @pl.when(s == pl.num_programs(1) - 1)
    def _():
        per_page = PAGE // CMP_BLOCK
        width = per_page * HEAD_DIM
        for x_ref, pe_ref, m_ref, off in ((xk_ref, pek_ref, mk_ref, 0), (xv_ref, pev_ref, mv_ref, 2 * HEAD_DIM)):
            acc = jnp.zeros((NSA_KV_HEADS * n_pages, width), F32)
            for d in range(0, HEAD_DIM, 2):
                rows = jnp.concatenate(
                    [jnp.concatenate([x_ref[pl.ds(g * HEAD_DIM + dd, n_pages, stride=PAGE), :] + pe_ref[dd:dd + 1, :]
                                      for g in range(NSA_KV_HEADS)], axis=0) for dd in (d, d + 1)], axis=1)
                acc = acc + _dot(rows.astype(BF16), m_ref[d // 2])
            for g in range(NSA_KV_HEADS):
                for n in range(per_page):
                    o_ref[n * n_pages:(n + 1) * n_pages, off + g * HEAD_DIM:off + (g + 1) * HEAD_DIM] = (
                        acc[g * n_pages:(g + 1) * n_pages, n * HEAD_DIM:(n + 1) * HEAD_DIM])


def nsa_cmp_s_call(page_table, cache_kv, pek, pev, mk, mv, *, layer, bsz, n_pages):
    pg = NSA_CMP_PAGES
    assert n_pages % pg == 0
    per_page = PAGE // CMP_BLOCK
    nbc = n_pages * per_page
    const = lambda shape: pl.BlockSpec(shape, lambda b, s, pt: (0,) * len(shape))
    grid_spec = pltpu.PrefetchScalarGridSpec(
        num_scalar_prefetch=1,
        grid=(bsz, n_pages // pg),
        in_specs=[_page_spec((None, None, PAGE, PAGE), layer, pg, i, (0, 0)) for i in range(pg)]
                 + [_page_spec((None, None, PAGE, PAGE), layer, pg, i, (1, 0)) for i in range(pg)]
                 + [const((HEAD_DIM, PAGE)), const((HEAD_DIM, PAGE)),
                    const((HEAD_DIM // 2, 2 * PAGE, per_page * HEAD_DIM)),
                    const((HEAD_DIM // 2, 2 * PAGE, per_page * HEAD_DIM))],
        out_specs=pl.BlockSpec((None, nbc, 2 * LANES), lambda b, s, pt: (b, 0, 0)),
        scratch_shapes=[pltpu.VMEM((n_pages * PAGE, PAGE), F32), pltpu.VMEM((n_pages * PAGE, PAGE), F32)],
    )
    return pl.pallas_call(
        functools.partial(_nsa_cmp_s_body, n_pages=n_pages),
        grid_spec=grid_spec,
        out_shape=jax.ShapeDtypeStruct((bsz, nbc, 2 * LANES), F32),
        compiler_params=_cparams(("parallel", "arbitrary")),
        name="nsa_cmp_s",
    )(page_table, *([cache_kv] * (2 * pg)), pek, pev, mk, mv)


def _stack_heads(q_ref, g):
    hg_n = NSA_GROUP
    return jnp.concatenate(
        [q_ref[:, (g * hg_n + hg) * HEAD_DIM:(g * hg_n + hg + 1) * HEAD_DIM] for hg in range(hg_n)], axis=0)


def _nsa_topk_s_body(q_ref, cmp_ref, o_ref, idx_ref, *, q_off, nbc):
    hg_n = NSA_GROUP
    per_page = PAGE // CMP_BLOCK
    n_pages = nbc // per_page
    w = 2 * n_pages + LANES
    t_col = q_off + lax.broadcasted_iota(jnp.int32, (T_PAD, 1), 0)
    t_stack = jnp.concatenate([t_col] * hg_n, axis=0)
    c = lax.broadcasted_iota(jnp.int32, (1, nbc), 1)
    cmp_blk = per_page * (c % n_pages) + c // n_pages

    def slc_blk(i):
        return jnp.where(i < n_pages, 2 * i, jnp.where(i < 2 * n_pages, 2 * (i - n_pages) + 1, i))

    blk_i = slc_blk(lax.broadcasted_iota(jnp.int32, (w, w), 0))
    blk_j = slc_blk(lax.broadcasted_iota(jnp.int32, (w, w), 1))
    blk_row = slc_blk(lax.broadcasted_iota(jnp.int32, (1, w), 1))
    rr = lax.broadcasted_iota(jnp.int32, (TOP_N, w), 0)
    lane = lax.broadcasted_iota(jnp.int32, (TOP_N, LANES), 1)
    for g in range(NSA_KV_HEADS):
        qs = _stack_heads(q_ref, g).astype(BF16)
        o_cmp, pc = _cmp_attend(qs, cmp_ref, g, t_stack, cmp_blk, cmp_blk >= 0)
        for hg in range(hg_n):
            h = g * hg_n + hg
            o_ref[:, h * HEAD_DIM:(h + 1) * HEAD_DIM] = o_cmp[hg * T_PAD:(hg + 1) * T_PAD]
        imp = pc[0:T_PAD]
        for hg in range(1, hg_n):
            imp = imp + pc[hg * T_PAD:(hg + 1) * T_PAD]
        imp = jnp.concatenate([imp[:, 0:n_pages] + imp[:, n_pages:2 * n_pages],
                               imp[:, 2 * n_pages:3 * n_pages] + imp[:, 3 * n_pages:4 * n_pages],
                               jnp.zeros((T_PAD, LANES), F32)], axis=1)
        score = _select_score(imp, t_col, blk_row)
        score_t = jnp.concatenate([score, jnp.zeros((LANES - T_PAD, w), F32)], axis=0).T
        out = jnp.zeros((TOP_N, LANES), F32)
        for t in range(T_PAD):
            col = score_t[:, t:t + 1]
            row = score[t:t + 1, :]
            ahead = (col > row) | ((col == row) & (blk_i < blk_j))
            rank = jnp.sum(ahead.astype(F32), axis=0, keepdims=True)
            hit = (rank == rr.astype(F32)) & (row >= 0.0)
            found = jnp.sum(hit.astype(F32), axis=1, keepdims=True)
            which = jnp.sum(jnp.where(hit, blk_row.astype(F32), 0.0), axis=1, keepdims=True)
            out = jnp.where(lane == t, jnp.where(found > 0.5, which, -1.0), out)
        idx_ref[g] = out.astype(jnp.int32)


def nsa_topk_s_call(p, cmp, *, bsz, q_off):
    nbc = cmp.shape[1]
    return pl.pallas_call(
        functools.partial(_nsa_topk_s_body, q_off=q_off, nbc=nbc),
        grid=(bsz,),
        in_specs=[pl.BlockSpec((T_PAD, GW), lambda b: (b, C_NQ // GW)),
                  pl.BlockSpec((None, nbc, 2 * LANES), lambda b: (b, 0, 0))],
        out_specs=[pl.BlockSpec((T_PAD, GW), lambda b: (b, 0)),
                   pl.BlockSpec((None, NSA_KV_HEADS, TOP_N, LANES), lambda b: (b, 0, 0, 0))],
        out_shape=[jax.ShapeDtypeStruct((bsz * T_PAD, GW), F32),
                   jax.ShapeDtypeStruct((bsz, NSA_KV_HEADS, TOP_N, LANES), jnp.int32)],
        compiler_params=_cparams(("parallel",)),
        name="nsa_topk_s",
    )(p, cmp)


def _nsa_sel_s_body(pt_ref, idx_ref, q_ref, kn_ref, vn_ref, *refs, n_real, n_tok, q_off):
    k_blocks, v_blocks, o_ref = refs[:TOP_N], refs[TOP_N:2 * TOP_N], refs[2 * TOP_N]
    b, g, t = pl.program_id(0), pl.program_id(1), pl.program_id(2)
    scale = HEAD_DIM ** -0.5
    hg_n = NSA_GROUP
    base = ((b * NSA_KV_HEADS + g) * n_tok + t) * TOP_N
    new_blk = q_off // SLC_BLOCK
    qrow = q_ref[pl.ds(t, 1), :]
    qs = jnp.concatenate([qrow[:, hg * HEAD_DIM:(hg + 1) * HEAD_DIM] for hg in range(hg_n)]
                         + [jnp.zeros((SUBLANES - hg_n, HEAD_DIM), F32)], axis=0).astype(BF16)

    def pick(blk):
        return jnp.where(g == 0, blk[:, 0:HEAD_DIM], blk[:, HEAD_DIM:2 * HEAD_DIM])

    lane_half = lax.broadcasted_iota(jnp.int32, (SUBLANES, PAGE), 1) // SLC_BLOCK
    logits, values = [], []
    has_new = jnp.bool_(False)
    for r in range(TOP_N):
        j = idx_ref[base + r]
        from_cache = (j >= 0) & (j < new_blk)
        has_new = has_new | (j == new_blk)
        sc = _dot(qs, k_blocks[r][...].astype(BF16)) * scale
        logits.append(jnp.where(from_cache & (lane_half == j % 2), sc, -jnp.inf))
        values.append(lambda pr, r=r: _dot_nt(pr, v_blocks[r][...].astype(BF16)))
    sc = _dot_nt(qs, pick(kn_ref[...]).astype(BF16)) * scale
    key = lax.broadcasted_iota(jnp.int32, (SUBLANES, T_PAD), 1)
    logits.append(jnp.where(has_new & (key <= t) & (key < n_real), sc, -jnp.inf))
    values.append(lambda pr: _dot(pr, pick(vn_ref[...]).astype(BF16)))
    m = logits[0].max(axis=1, keepdims=True)
    for x in logits[1:]:
        m = jnp.maximum(m, x.max(axis=1, keepdims=True))
    l = jnp.zeros((SUBLANES, 1), F32)
    acc = jnp.zeros((SUBLANES, HEAD_DIM), F32)
    for x, pv in zip(logits, values):
        pr = jnp.exp(x - m)
        l = l + jnp.sum(pr, axis=1, keepdims=True)
        acc = acc + pv(pr.astype(BF16))
    o_ref[...] = acc / l


def nsa_sel_s_call(page_table, idx, p, cache_kv, *, layer, bsz, n_tok, n_real, q_off):
    n_cached = q_off // SLC_BLOCK
    blocks_per_page = PAGE // SLC_BLOCK

    def blk_spec(r, kind):
        def index_map(b, g, t, pt, ix):
            return (pt[((b * NSA_KV_HEADS + g) * n_tok + t) * TOP_N + r], layer, kind * NSA_KV_HEADS + g, 0)
        return pl.BlockSpec((None, None, HEAD_DIM, PAGE), index_map)

    grid_spec = pltpu.PrefetchScalarGridSpec(
        num_scalar_prefetch=2,
        grid=(bsz, NSA_KV_HEADS, n_tok),
        in_specs=[pl.BlockSpec((T_PAD, 2 * LANES), lambda b, g, t, pt, ix: (b, C_NQ // (2 * LANES) + g)),
                  pl.BlockSpec((T_PAD, LANES), lambda b, g, t, pt, ix: (b, C_NKV // LANES + 2)),
                  pl.BlockSpec((T_PAD, LANES), lambda b, g, t, pt, ix: (b, C_NKV // LANES + 3))]
                 + [blk_spec(r, 2) for r in range(TOP_N)] + [blk_spec(r, 3) for r in range(TOP_N)],
        out_specs=pl.BlockSpec((None, None, None, SUBLANES, HEAD_DIM), lambda b, g, t, pt, ix: (b, g, t, 0, 0)),
    )
    return pl.pallas_call(
        functools.partial(_nsa_sel_s_body, n_real=n_real, n_tok=n_tok, q_off=q_off),
        grid_spec=grid_spec,
        out_shape=jax.ShapeDtypeStruct((bsz, NSA_KV_HEADS, n_tok, SUBLANES, HEAD_DIM), F32),
        compiler_params=_cparams(("parallel", "arbitrary", "arbitrary")),
        name="nsa_sel_s",
    )(page_table, idx, p, p, p, *([cache_kv] * (2 * TOP_N)))


def _nsa_win_s_body(q_ref, gl_ref, oc_ref, os_ref, win_ref, new_ref, o_ref, *, n_real, win_len):
    hg_n = NSA_GROUP
    rows = hg_n * T_PAD
    scale = HEAD_DIM ** -0.5
    gate = _sigmoid(gl_ref[...])
    t_row = lax.broadcasted_iota(jnp.int32, (rows, 1), 0) % T_PAD
    key = lax.broadcasted_iota(jnp.int32, (rows, win_len), 1)
    mask_c = key + WINDOW > t_row + win_len
    new_i = lax.broadcasted_iota(jnp.int32, (rows, PAGE), 1)
    mask_n = (new_i <= t_row) & (new_i < n_real)
    zpad = jnp.zeros((PAGE - T_PAD, HEAD_DIM), F32)
    for g in range(NSA_KV_HEADS):
        qs = _stack_heads(q_ref, g).astype(BF16)
        kt = win_ref[g * HEAD_DIM:(g + 1) * HEAD_DIM, :].astype(BF16)
        vt = win_ref[LANES + g * HEAD_DIM:LANES + (g + 1) * HEAD_DIM, :].astype(BF16)
        kn = jnp.concatenate([new_ref[:, g * HEAD_DIM:(g + 1) * HEAD_DIM], zpad], axis=0).astype(BF16)
        vn = jnp.concatenate([new_ref[:, LANES + g * HEAD_DIM:LANES + (g + 1) * HEAD_DIM], zpad], axis=0).astype(BF16)
        sc_c = jnp.where(mask_c, _dot(qs, kt) * scale, -jnp.inf)
        sc_n = jnp.where(mask_n, _dot_nt(qs, kn) * scale, -jnp.inf)
        m = jnp.maximum(jnp.max(sc_c, axis=1, keepdims=True), jnp.max(sc_n, axis=1, keepdims=True))
        pr_c = jnp.exp(sc_c - m)
        pr_n = jnp.exp(sc_n - m)
        l = jnp.sum(pr_c, axis=1, keepdims=True) + jnp.sum(pr_n, axis=1, keepdims=True)
        o_win = (_dot_nt(pr_c.astype(BF16), vt) + _dot(pr_n.astype(BF16), vn)) / l
        for hg in range(hg_n):
            h = g * hg_n + hg
            cs = slice(h * HEAD_DIM, (h + 1) * HEAD_DIM)
            o_ref[:, cs] = (gate[:, 3 * h:3 * h + 1] * oc_ref[:, cs] + gate[:, 3 * h + 1:3 * h + 2] * os_ref[:, cs]
                            + gate[:, 3 * h + 2:3 * h + 3] * o_win[hg * T_PAD:(hg + 1) * T_PAD])


def nsa_win_s_call(p, o_cmp, o_slc, cache_win, *, layer, bsz, n_real):
    win_len = cache_win.shape[3]
    assert win_len == WINDOW
    row = lambda width, off: pl.BlockSpec((T_PAD, width), lambda b: (b, off // width))
    return pl.pallas_call(
        functools.partial(_nsa_win_s_body, n_real=n_real, win_len=win_len),
        grid=(bsz,),
        in_specs=[row(GW, C_NQ), row(LANES, C_NG), row(GW, 0), row(GW, 0),
                  pl.BlockSpec((None, None, 2 * LANES, win_len), lambda b: (b, layer, 0, 0)),
                  row(2 * LANES, C_NKV + 4 * LANES)],
        out_specs=pl.BlockSpec((T_PAD, GW), lambda b: (b, 0)),
        out_shape=jax.ShapeDtypeStruct((bsz * T_PAD, GW), F32),
        compiler_params=_cparams(("parallel",)),
        name="nsa_win_s",
    )(p, p, o_cmp, o_slc, cache_win, p)


def _lane_pad(v):
    return jnp.pad(v, (0, LANES - v.shape[0]))[None, :]


def _layer_weights(w, l):
    win_t = w['w_in'][l].T
    o_fox = GW + SSD_CONV_DIM + SSD_HEADS
    o_nsa = o_fox + 3 * GW + FOX_HEADS
    o_s5 = o_nsa + GW + 6 * NSA_KV_HEADS * HEAD_DIM + 3 * NSA_HEADS
    cols = lambda a, b: win_t[a:b]
    zpad = lambda n: jnp.zeros((n, D_MODEL), F32)
    w_in = jnp.concatenate([
        cols(GW, GW + SSD_CONV_DIM),
        cols(GW + SSD_CONV_DIM, o_fox), zpad(LANES - SSD_HEADS),
        cols(o_fox + 3 * GW, o_nsa), zpad(LANES - FOX_HEADS),
        cols(0, GW),
        cols(o_s5, o_s5 + GW),
        cols(o_fox, o_fox + 3 * GW),
        cols(o_nsa, o_nsa + GW),
        cols(o_nsa + GW, o_nsa + GW + 6 * LANES),
        cols(o_nsa + GW + 6 * LANES, o_s5), zpad(LANES - 3 * NSA_HEADS),
        zpad(P_W - C_NG - LANES),
    ], axis=0).astype(BF16)
    assert w_in.shape == (P_W, D_MODEL)
    pe4, w4 = nsa_cmp_params(w['nsa_cmp_pe'][l], w['nsa_cmp_w'][l])
    s5 = s5_params(w['s5_lambda_re'][l], w['s5_lambda_im'][l], w['s5_log_dt'][l],
                   w['s5_b_re'][l], w['s5_b_im'][l], w['s5_c_re'][l], w['s5_c_im'][l])
    return dict(
        ffn1=(w['ffn1_norm'][l][None, :], w['ffn1_w1'], w['ffn1_w3'], w['ffn1_w2']),
        ffn2=(w['ffn2_norm'][l][None, :], w['ffn2_w1'], w['ffn2_w3'], w['ffn2_w2']),
        mix_norm=w['mix_norm'][l][None, :], w_in=w_in,
        ssd=(w['ssd_conv_w'][l], w['ssd_conv_b'][l][None, :], _lane_pad(w['ssd_dt_bias'][l]),
             _lane_pad(w['ssd_a_log'][l]), _lane_pad(w['ssd_d'][l])),
        fox_bias=_lane_pad(w['fox_f_bias'][l]),
        pe4=pe4, w4=w4, cmp_t=nsa_cmp_params_t(w['nsa_cmp_pe'][l], w['nsa_cmp_w'][l]), s5=s5, s5_d=w['s5_d'][l][None, :], s5_glu=w['s5_w_glu'][l].astype(BF16),
        gains=jnp.stack([w['ssd_norm'][l], w['fox_out_norm'][l], w['nsa_out_norm'][l], w['s5_out_norm'][l]]),
        w_out=w['w_out'],
    )


def _prompt_layer(x, lw, final_gain, *, layer, bsz, t_len, last):
    tm = 512
    x = ffn_call(x, *lw['ffn1'], final_gain, tm=FFN_ROWS, layer=layer, final_norm=False)
    p, fox_kv, nsa_kv, nsa_win = inproj_call(x, lw['mix_norm'], lw['w_in'], tm=tm, keys_minor=(bsz, t_len))
    conv0 = jnp.zeros((bsz, CONV_W - 1, SSD_CONV_DIM), F32)
    ssm0 = jnp.zeros((bsz, SSD_HEADS, HEAD_DIM, SSD_STATE), F32)
    y_ssd, ssm, conv = ssd_call(p, conv0, ssm0, *lw['ssd'], bsz=bsz, t_len=t_len, n_real=SSD_CHUNK)
    lf, cumt = fox_prep_call(p, lw['fox_bias'], bsz=bsz, t_len=t_len)
    y_fox = fox_attn_call(p, cumt, bsz=bsz, t_len=t_len)
    cmp = nsa_cmp_p_call(p, lw['pe4'], lw['w4'], bsz=bsz, t_len=t_len)
    y_nsa = nsa_attn_p_call(p, cmp, bsz=bsz, t_len=t_len)
    bre, bim, are, aim, cre, cim = lw['s5']
    x0 = jnp.zeros((SUBLANES, S5_N), F32)
    y_s5, xr, xi = s5_call(p.reshape(bsz, t_len, P_W), bre, bim, are, aim, x0, x0, cre, cim, lw['s5_d'], lw['s5_glu'],
                           t_len=t_len, steps=256, t_last=t_len - 1)
    y_s5 = y_s5.reshape(bsz * t_len, GW)
    x = outproj_call(x, y_ssd, y_fox, y_nsa, y_s5, lw['gains'], lw['w_out'], tm=tm, layer=layer)
    x = ffn_call(x, *lw['ffn2'], final_gain, tm=FFN_ROWS, layer=layer, final_norm=last)
    keep = min(WINDOW, t_len)
    off = ((t_len - 1) % (SUBLANES // bsz)) * bsz
    time_second = lambda a, *dims: jnp.moveaxis(a.reshape(bsz, *dims, a.shape[-1]), -1, 1)
    states = (
        time_second(fox_kv, 2, FOX_HEADS, HEAD_DIM),
        time_second(lf, FOX_HEADS),
        time_second(nsa_kv, 4, NSA_KV_HEADS, HEAD_DIM),
        time_second(nsa_win[:, :, t_len - keep:], 2, NSA_KV_HEADS, HEAD_DIM),
        ssm, conv,
        xr[off:off + bsz].reshape(bsz, S5_GROUPS, S5_STATE),
        xi[off:off + bsz].reshape(bsz, S5_GROUPS, S5_STATE),
    )
    return x, states


def _sample_layer(x, lw, final_gain, caches, page_table, *, layer, bsz, n_real, q_off, last):
    fox_kv_t, fox_lf_t, nsa_kv_t, nsa_win_t, cache_nsa_win, st_ssd, st_conv, st_re, st_im = caches
    tm = bsz * T_PAD
    n_pages = page_table.shape[1]
    assert q_off == n_pages * PAGE and q_off % SLC_BLOCK == 0 and bsz == SUBLANES
    x = ffn_call(x, *lw['ffn1'], final_gain, tm=tm, layer=layer, final_norm=False)
    p, fox_kv, nsa_kv, nsa_win = inproj_call(x, lw['mix_norm'], lw['w_in'], tm=tm)
    p3 = p.reshape(bsz, T_PAD, P_W)

    p_ssd = jnp.pad(p3[:, :, :C_U], ((0, 0), (0, SSD_CHUNK - T_PAD), (0, 0))).reshape(bsz * SSD_CHUNK, C_U)
    y_ssd, ssm, conv = ssd_call(p_ssd, st_conv[:, layer], st_ssd[:, layer], *lw['ssd'],
                                bsz=bsz, t_len=SSD_CHUNK, n_real=n_real)
    y_ssd = y_ssd.reshape(bsz, SSD_CHUNK, GW)[:, :T_PAD].reshape(tm, GW)

    r_past = fox_prep_s_call(page_table, fox_lf_t, layer=layer, bsz=bsz, n_pages=n_pages)
    y_fox, lf = fox_attn_s_call(page_table, p, lw['fox_bias'], r_past, fox_kv_t,
                                layer=layer, bsz=bsz, n_pages=n_pages, n_real=n_real)

    cmp = nsa_cmp_s_call(page_table, nsa_kv_t, *lw['cmp_t'], layer=layer, bsz=bsz, n_pages=n_pages)
    o_cmp, idx = nsa_topk_s_call(p, cmp, bsz=bsz, q_off=q_off)
    idx = jnp.swapaxes(idx[:, :, :, :n_real], 2, 3).reshape(-1)
    sel_blocks = jnp.clip(idx, 0, q_off // SLC_BLOCK - 1).reshape(bsz, -1) // (PAGE // SLC_BLOCK)
    sel_pages = jnp.take_along_axis(page_table, sel_blocks, axis=1).reshape(-1)
    o_slc = nsa_sel_s_call(sel_pages, idx, p, nsa_kv_t,
                           layer=layer, bsz=bsz, n_tok=n_real, n_real=n_real, q_off=q_off)
    o_slc = jnp.transpose(o_slc[:, :, :, :NSA_GROUP], (0, 2, 1, 3, 4)).reshape(bsz, n_real, GW)
    o_slc = jnp.pad(o_slc, ((0, 0), (0, T_PAD - n_real), (0, 0))).reshape(tm, GW)
    y_nsa = nsa_win_s_call(p, o_cmp, o_slc, nsa_win_t, layer=layer, bsz=bsz, n_real=n_real)

    bre, bim, are, aim, cre, cim = lw['s5']
    y_s5, xr, xi = s5_call(p3, bre, bim, are, aim, st_re[:, layer].reshape(bsz, S5_N), st_im[:, layer].reshape(bsz, S5_N),
                           cre, cim, lw['s5_d'], lw['s5_glu'], t_len=T_PAD, steps=T_PAD, t_last=n_real - 1)
    y_s5 = y_s5.reshape(tm, GW)

    x = outproj_call(x, y_ssd, y_fox, y_nsa, y_s5, lw['gains'], lw['w_out'], tm=tm, layer=layer)
    x = ffn_call(x, *lw['ffn2'], final_gain, tm=tm, layer=layer, final_norm=last)
    real = lambda a: a.reshape(bsz, T_PAD, -1)[:, :n_real]
    win_rows = real(nsa_win).reshape(bsz, n_real, 2, NSA_KV_HEADS, HEAD_DIM)
    states = (
        real(fox_kv).reshape(bsz, n_real, 2, FOX_HEADS, HEAD_DIM),
        lf.reshape(bsz, T_PAD, LANES)[:, :n_real, :FOX_HEADS],
        real(nsa_kv).reshape(bsz, n_real, 4, NSA_KV_HEADS, HEAD_DIM),
        jnp.concatenate([cache_nsa_win[:, layer, n_real:], win_rows], axis=1),
        ssm, conv,
        xr.reshape(bsz, S5_GROUPS, S5_STATE), xi.reshape(bsz, S5_GROUPS, S5_STATE),
    )
    return x, states


def kernel(x_prompt, x_sample, cache_fox_kv, cache_fox_logf, cache_nsa_kv, cache_nsa_win_kv, state_ssd,
           state_ssd_conv, state_s5_re, state_s5_im, page_table, ffn1_norm, ffn1_w1, ffn1_w3, ffn1_w2, mix_norm,
           w_in, ssd_conv_w, ssd_conv_b, ssd_dt_bias, ssd_a_log, ssd_d, ssd_norm, fox_f_bias, fox_out_norm,
           nsa_cmp_pe, nsa_cmp_w, nsa_out_norm, s5_lambda_re, s5_lambda_im, s5_log_dt, s5_b_re, s5_b_im,
           s5_c_re, s5_c_im, s5_d, s5_w_glu, s5_out_norm, w_out, ffn2_norm, ffn2_w1, ffn2_w3, ffn2_w2, final_norm):
    w = dict(ffn1_norm=ffn1_norm, ffn1_w1=ffn1_w1, ffn1_w3=ffn1_w3, ffn1_w2=ffn1_w2, mix_norm=mix_norm, w_in=w_in,
             ssd_conv_w=ssd_conv_w, ssd_conv_b=ssd_conv_b, ssd_dt_bias=ssd_dt_bias, ssd_a_log=ssd_a_log, ssd_d=ssd_d,
             ssd_norm=ssd_norm, fox_f_bias=fox_f_bias, fox_out_norm=fox_out_norm, nsa_cmp_pe=nsa_cmp_pe,
             nsa_cmp_w=nsa_cmp_w, nsa_out_norm=nsa_out_norm, s5_lambda_re=s5_lambda_re, s5_lambda_im=s5_lambda_im,
             s5_log_dt=s5_log_dt, s5_b_re=s5_b_re, s5_b_im=s5_b_im, s5_c_re=s5_c_re, s5_c_im=s5_c_im, s5_d=s5_d,
             s5_w_glu=s5_w_glu, s5_out_norm=s5_out_norm, w_out=w_out, ffn2_norm=ffn2_norm, ffn2_w1=ffn2_w1,
             ffn2_w3=ffn2_w3, ffn2_w2=ffn2_w2)
    bsz_p, t_len, _ = x_prompt.shape
    bsz_s, n_real, _ = x_sample.shape
    depth = w_in.shape[0]
    q_off = page_table.shape[1] * PAGE
    fg = final_norm[None, :]
    n_pool = cache_fox_kv.shape[0]
    keys_minor = (0, 1, 3, 4, 5, 2)
    caches = (jnp.transpose(cache_fox_kv, keys_minor).reshape(n_pool, depth, 2 * GW, PAGE),
              jnp.transpose(cache_fox_logf, (0, 1, 3, 2)),
              jnp.transpose(cache_nsa_kv, keys_minor).reshape(n_pool, depth, 4 * LANES, PAGE),
              jnp.transpose(cache_nsa_win_kv, keys_minor).reshape(bsz_s, depth, 2 * LANES, WINDOW),
              cache_nsa_win_kv, state_ssd, state_ssd_conv, state_s5_re, state_s5_im)
    xp = x_prompt.reshape(bsz_p * t_len, D_MODEL)
    xs = jnp.pad(x_sample, ((0, 0), (0, T_PAD - n_real), (0, 0))).reshape(bsz_s * T_PAD, D_MODEL)
    st_p, st_s = [], []
    for l in range(depth):
        lw = _layer_weights(w, l)
        last = l == depth - 1
        xp, sp = _prompt_layer(xp, lw, fg, layer=l, bsz=bsz_p, t_len=t_len, last=last)
        xs, ss = _sample_layer(xs, lw, fg, caches, page_table, layer=l, bsz=bsz_s, n_real=n_real, q_off=q_off,
                               last=last)
        st_p.append(sp)
        st_s.append(ss)
    y_p = xp.reshape(bsz_p, t_len, D_MODEL)
    y_s = xs.reshape(bsz_s, T_PAD, D_MODEL)[:, :n_real]
    out = [y_p, y_s]
    for i in range(8):
        out.append(jnp.stack([s[i] for s in st_p], axis=1))
        out.append(jnp.stack([s[i] for s in st_s], axis=1))
    return tuple(out)
```
